```python
import jax, jax.numpy as jnp
from jax import lax
import numpy as np

D_MODEL = 1024
BATCH = 8
SEQ = 2048
DEPTH = 1

N_META = 16
CHUNK = 128
D_MIX = 2 * D_MODEL
RET_WIDTH = D_MIX // 2
RET_HEADS = 4
RET_DV = RET_WIDTH // RET_HEADS
RET_DK = RET_DV // 2
FOX_WIDTH = D_MIX - RET_WIDTH
FOX_HEAD_DIM = 64
FOX_HEADS = FOX_WIDTH // FOX_HEAD_DIM
ROPE_BASE = 10000.0
EPS = 1e-6
NEG_INF = -1e30
SPLIT_SIZES = (RET_HEADS * RET_DK, RET_HEADS * RET_DK, RET_WIDTH, RET_WIDTH,
               FOX_WIDTH, FOX_WIDTH, FOX_WIDTH, FOX_WIDTH, FOX_HEADS)
D_IN_PROJ = 2 * RET_HEADS * RET_DK + 2 * RET_WIDTH + 4 * FOX_WIDTH + FOX_HEADS

kernel_name = "hymba_retention_fox_hybrid"


def rmsnorm(x, g):
    xf = x.astype(jnp.float32)
    y = xf * lax.rsqrt(jnp.mean(xf * xf, axis=-1, keepdims=True) + EPS)
    return (y * g.astype(jnp.float32)).astype(x.dtype)


def head_rmsnorm(x):
    xf = x.astype(jnp.float32)
    y = xf * lax.rsqrt(jnp.mean(xf * xf, axis=-1, keepdims=True) + EPS)
    return y.astype(x.dtype)


def rotary(x, pos):
    d = x.shape[-1]
    inv = ROPE_BASE ** (-jnp.arange(0, d, 2, dtype=jnp.float32) / d)
    ang = pos[:, None] * inv[None, :]
    cos = jnp.cos(ang)[None, :, None, :]
    sin = jnp.sin(ang)[None, :, None, :]
    xf = x.astype(jnp.float32)
    x1, x2 = xf[..., : d // 2], xf[..., d // 2:]
    out = jnp.concatenate([x1 * cos - x2 * sin, x1 * sin + x2 * cos], axis=-1)
    return out.astype(x.dtype)


def retention_chunkwise(q, k, v):
    b, t, h, dk = q.shape
    dv = v.shape[-1]
    nc = t // CHUNK
    log_gamma = jnp.log1p(-jnp.exp2(-5.0 - jnp.arange(h, dtype=jnp.float32)))
    q = q.reshape(b, nc, CHUNK, h, dk)
    k = k.reshape(b, nc, CHUNK, h, dk)
    v = v.reshape(b, nc, CHUNK, h, dv)
    idx = jnp.arange(CHUNK, dtype=jnp.float32)
    diff = idx[:, None] - idx[None, :]
    dmask = jnp.where(diff[None] >= 0,
                      jnp.exp(log_gamma[:, None, None] * jnp.maximum(diff, 0.0)[None]),
                      0.0)
    scores = jnp.einsum('bnchk,bnshk->bnhcs', q, k) * dmask.astype(q.dtype)
    out_intra = jnp.einsum('bnhcs,bnshv->bnchv', scores, v)
    zeta = jnp.exp(log_gamma[:, None] * (CHUNK - 1.0 - idx)[None, :])
    kv = jnp.einsum('bnchk,bnchv,hc->nbhkv', k, v, zeta.astype(k.dtype)).astype(jnp.float32)
    chunk_decay = jnp.exp(log_gamma * CHUNK)[None, :, None, None]

    def step(state, kv_i):
        return state * chunk_decay + kv_i, state

    _, s_prev = lax.scan(step, jnp.zeros((b, h, dk, dv), jnp.float32), kv)
    xi = jnp.exp(log_gamma[:, None] * (idx + 1.0)[None, :])
    out_inter = jnp.einsum('bnchk,nbhkv,hc->bnchv', q, s_prev.astype(q.dtype), xi.astype(q.dtype))
    return (out_intra + out_inter).reshape(b, t, h, dv)


def forgetting_attention(q, k, v, log_f, valid):
    b, t, h, d = q.shape
    c = jnp.cumsum(log_f, axis=1).transpose(0, 2, 1)
    scale = d ** -0.5
    outs = []
    for i in range(t // CHUNK):
        lo, hi = i * CHUNK, (i + 1) * CHUNK
        s = jnp.einsum('bthd,bshd->bhts', q[:, lo:hi], k[:, :hi]).astype(jnp.float32) * scale
        s = s + (c[:, :, lo:hi, None] - c[:, :, None, :hi])
        tpos = jnp.arange(lo, hi)
        spos = jnp.arange(hi)
        mask = (spos[None, :] <= tpos[:, None]) & valid[None, :hi]
        s = jnp.where(mask[None, None], s, NEG_INF)
        p = jax.nn.softmax(s, axis=-1)
        outs.append(jnp.einsum('bhts,bshd->bthd', p.astype(v.dtype), v[:, :hi]))
    return jnp.concatenate(outs, axis=1)


def hybrid_layer(h_res, norm_g, w_in, b_f, w_out):
    b, l, _ = h_res.shape
    pad = CHUNK - N_META
    t = l + pad
    u = rmsnorm(h_res, norm_g)
    u = jnp.pad(u, ((0, 0), (pad, 0), (0, 0)))
    z = u @ w_in
    offsets = np.cumsum(SPLIT_SIZES)[:-1].tolist()
    rq, rk, rv, rg, fq, fk, fv, fg, ff = jnp.split(z, offsets, axis=-1)
    pos = jnp.arange(t, dtype=jnp.float32) - pad
    valid = jnp.arange(t) >= pad
    rq = rotary(rq.reshape(b, t, RET_HEADS, RET_DK), pos)
    rk = rotary(rk.reshape(b, t, RET_HEADS, RET_DK), pos) * (RET_DK ** -0.5)
    y_r = retention_chunkwise(rq, rk, rv.reshape(b, t, RET_HEADS, RET_DV))
    y_r = head_rmsnorm(y_r).reshape(b, t, RET_WIDTH) * jax.nn.silu(rg)
    log_f = jnp.where(valid[None, :, None],
                      jax.nn.log_sigmoid((ff + b_f).astype(jnp.float32)), 0.0)
    y_f = forgetting_attention(fq.reshape(b, t, FOX_HEADS, FOX_HEAD_DIM),
                               fk.reshape(b, t, FOX_HEADS, FOX_HEAD_DIM),
                               fv.reshape(b, t, FOX_HEADS, FOX_HEAD_DIM),
                               log_f, valid)
    y_f = y_f.reshape(b, t, FOX_WIDTH) * jax.nn.silu(fg)
    y = jnp.concatenate([y_r, y_f], axis=-1)[:, pad:]
    return h_res + y @ w_out


def setup_inputs(seed: int = 0) -> dict:
    key = jax.random.key(seed)
    k = jax.random.split(key, 7)
    x = jax.random.normal(k[0], (BATCH, SEQ, D_MODEL), jnp.float32)
    meta_tokens = jax.random.normal(k[1], (N_META, D_MODEL), jnp.float32)
    norm_g = 1.0 + 0.02 * jax.random.normal(k[2], (DEPTH, D_MODEL), jnp.float32)
    w_in = jax.random.normal(k[3], (DEPTH, D_MODEL, D_IN_PROJ), jnp.float32) * D_MODEL ** -0.5
    b_f = jax.random.uniform(k[4], (DEPTH, FOX_HEADS), jnp.float32, 1.0, 4.0)
    w_out = jax.random.normal(k[5], (DEPTH, D_MIX, D_MODEL), jnp.float32) * D_MIX ** -0.5
    final_g = 1.0 + 0.02 * jax.random.normal(k[6], (D_MODEL,), jnp.float32)
    return {"x": x, "meta_tokens": meta_tokens, "norm_g": norm_g, "w_in": w_in,
            "b_f": b_f, "w_out": w_out, "final_g": final_g}


def reference(x, meta_tokens, norm_g, w_in, b_f, w_out, final_g):
    b = x.shape[0]
    meta = jnp.broadcast_to(meta_tokens[None].astype(x.dtype), (b, N_META, D_MODEL))
    h = jnp.concatenate([meta, x], axis=1)
    for layer in range(DEPTH):
        h = hybrid_layer(h, norm_g[layer], w_in[layer], b_f[layer], w_out[layer])
    return rmsnorm(h[:, N_META:], final_g)
```

```python
import functools

import jax
import jax.numpy as jnp
import numpy as np
from jax import lax
from jax.experimental import pallas as pl
from jax.experimental.pallas import tpu as pltpu

D_MODEL = 1024
N_META = 16
CHUNK = 128
RET_HEADS = 4
RET_DK = 128
RET_DV = 256
FOX_HEADS = 16
FOX_HEAD_DIM = 64
ROPE_BASE = 10000.0
EPS = 1e-6
NEG_INF = -1e30

LANES = 128
D_MAIN = 7168
COL_RQ, COL_RK, COL_RV, COL_RG = 0, 4, 8, 16
COL_FQ, COL_FK, COL_FV, COL_FG = 24, 32, 40, 48
FOX_TQ = 256
FOX_TK = 256
VMEM_LIMIT = 48 * 1024 * 1024

_F32 = jnp.float32
_BF16 = jnp.bfloat16


def _dot(a, b):
    return jnp.dot(a, b, preferred_element_type=_F32)


def _dot_nt(a, b):
    return lax.dot_general(a, b, (((1,), (1,)), ((), ())), preferred_element_type=_F32)


def _silu(g):
    return g * (1.0 / (1.0 + jnp.exp(-g)))


def _inproj_kernel(x_ref, g_ref, w_ref, wff_ref, bf_ref, z_ref, c_ref, u_scr, lf_scr,
                   *, tm, sub, n_valid):
    j = pl.program_id(1)

    @pl.when(j == 0)
    def _():
        g = g_ref[...]
        for r in range(tm // sub):
            rows = pl.ds(r * sub, sub)
            xf = x_ref[rows, :]
            ms = jnp.mean(xf * xf, axis=-1, keepdims=True)
            u = (xf * lax.rsqrt(ms + EPS) * g).astype(_BF16)
            u_scr[rows, :] = u
            lf_scr[:, r * sub:(r + 1) * sub] = _dot_nt(wff_ref[...], u)
        lane = lax.broadcasted_iota(jnp.int32, (FOX_HEADS, LANES), 1)
        carry = jnp.zeros((FOX_HEADS, 1), _F32)
        bf = bf_ref[...]
        for ci in range(tm // LANES):
            v = lf_scr[:, ci * LANES:(ci + 1) * LANES] + bf
            blk = jnp.minimum(v, 0.0) - jnp.log1p(jnp.exp(-jnp.abs(v)))
            if (ci + 1) * LANES > n_valid:
                blk = jnp.where(lane + ci * LANES < n_valid, blk, 0.0)
            sh = 1
            while sh < LANES:
                rolled = pltpu.roll(blk, sh, axis=1)
                blk = blk + jnp.where(lane >= sh, rolled, 0.0)
                sh *= 2
            blk = blk + carry
            c_ref[0, :, ci * LANES:(ci + 1) * LANES] = blk
            carry = blk[:, LANES - 1:LANES]

    w = w_ref[...]
    for r in range(tm // sub):
        rows = pl.ds(r * sub, sub)
        z_ref[rows, :] = _dot(u_scr[rows, :], w).astype(_BF16)


def _inproj(x2d, g, w_main, w_ff_t, b_f, *, tm, tn, n_valid):
    rows = x2d.shape[0]
    sub = min(tm, 256)
    kern = functools.partial(_inproj_kernel, tm=tm, sub=sub, n_valid=n_valid)
    return pl.pallas_call(
        kern,
        grid=(rows // tm, D_MAIN // tn),
        in_specs=[
            pl.BlockSpec((tm, D_MODEL), lambda i, j: (i, 0)),
            pl.BlockSpec((1, D_MODEL), lambda i, j: (0, 0)),
            pl.BlockSpec((D_MODEL, tn), lambda i, j: (0, j)),
            pl.BlockSpec((FOX_HEADS, D_MODEL), lambda i, j: (0, 0)),
            pl.BlockSpec((FOX_HEADS, 1), lambda i, j: (0, 0)),
        ],
        out_specs=[
            pl.BlockSpec((tm, tn), lambda i, j: (i, j)),
            pl.BlockSpec((1, FOX_HEADS, tm), lambda i, j: (i, 0, 0)),
        ],
        out_shape=[
            jax.ShapeDtypeStruct((rows, D_MAIN), _BF16),
            jax.ShapeDtypeStruct((rows // tm, FOX_HEADS, tm), _F32),
        ],
        scratch_shapes=[
            pltpu.VMEM((tm, D_MODEL), _BF16),
            pltpu.VMEM((FOX_HEADS, tm), _F32),
        ],
        compiler_params=pltpu.CompilerParams(
            dimension_semantics=("arbitrary", "arbitrary"),
            vmem_limit_bytes=VMEM_LIMIT),
        name="inproj",
    )(x2d, g, w_main, w_ff_t, b_f)


def _rotary(x, cos2, sin2):
    return x * cos2 + pltpu.roll(x, RET_DK // 2, axis=1) * sin2


def _retention_kernel(lg_ref, q_ref, k_ref, v_ref, g_ref, km_ref, vm_ref,
                      cos_ref, sin_ref, cosm_ref, sinm_ref, o_ref, *, n_chunks):
    h = pl.program_id(1)
    lg = lg_ref[h]
    ri = lax.broadcasted_iota(jnp.int32, (CHUNK, CHUNK), 0)
    ci = lax.broadcasted_iota(jnp.int32, (CHUNK, CHUNK), 1)
    diff = (ri - ci).astype(_F32)
    dmask = jnp.where(diff >= 0, jnp.exp(lg * jnp.maximum(diff, 0.0)), 0.0)
    idx = lax.broadcasted_iota(jnp.int32, (CHUNK, 1), 0).astype(_F32)
    xi = jnp.exp(lg * (idx + 1.0))
    zeta = jnp.exp(lg * (CHUNK - 1.0 - idx))
    chunk_decay = jnp.exp(lg * jnp.full((1, 1), float(CHUNK), _F32))
    kscale = RET_DK ** -0.5

    zeta_m = jnp.exp(lg * (N_META - 1.0 - idx))
    km = _rotary(km_ref[...].astype(_F32), cosm_ref[...], sinm_ref[...]) * kscale
    state0 = _dot((km * zeta_m).T.astype(_BF16), vm_ref[...])

    def body(n, state):
        rows = pl.ds(pl.multiple_of(n * CHUNK, CHUNK), CHUNK)
        cos2 = cos_ref[rows, :]
        sin2 = sin_ref[rows, :]
        q = _rotary(q_ref[0, rows, :].astype(_F32), cos2, sin2)
        k = _rotary(k_ref[0, rows, :].astype(_F32), cos2, sin2) * kscale
        v = v_ref[0, rows, :]
        qb = q.astype(_BF16)
        scores = _dot_nt(qb, k.astype(_BF16)) * dmask
        out = _dot(scores.astype(_BF16), v)
        out = out + _dot(qb, state.astype(_BF16)) * xi
        new_state = state * chunk_decay + _dot((k * zeta).T.astype(_BF16), v)
        y = out * lax.rsqrt(jnp.mean(out * out, axis=-1, keepdims=True) + EPS)
        gate = g_ref[0, rows, :].astype(_F32)
        o_ref[0, rows, :] = (y * _silu(gate)).astype(_BF16)
        return new_state

    lax.fori_loop(0, n_chunks, body, state0)


def _retention(lg, z3, zm, cos2, sin2, cosm, sinm):
    b, s, _ = z3.shape
    kern = functools.partial(_retention_kernel, n_chunks=s // CHUNK)
    return pl.pallas_call(
        kern,
        grid=(b, RET_HEADS),
        in_specs=[
            pl.BlockSpec(memory_space=pltpu.SMEM),
            pl.BlockSpec((1, s, RET_DK), lambda i, h: (i, 0, COL_RQ + h)),
            pl.BlockSpec((1, s, RET_DK), lambda i, h: (i, 0, COL_RK + h)),
            pl.BlockSpec((1, s, RET_DV), lambda i, h: (i, 0, COL_RV // 2 + h)),
            pl.BlockSpec((1, s, RET_DV), lambda i, h: (i, 0, COL_RG // 2 + h)),
            pl.BlockSpec((CHUNK, RET_DK), lambda i, h: (0, COL_RK + h)),
            pl.BlockSpec((CHUNK, RET_DV), lambda i, h: (0, COL_RV // 2 + h)),
            pl.BlockSpec((s, RET_DK), lambda i, h: (0, 0)),
            pl.BlockSpec((s, RET_DK), lambda i, h: (0, 0)),
            pl.BlockSpec((CHUNK, RET_DK), lambda i, h: (0, 0)),
            pl.BlockSpec((CHUNK, RET_DK), lambda i, h: (0, 0)),
        ],
        out_specs=pl.BlockSpec((1, s, RET_DV), lambda i, h: (i, 0, h)),
        out_shape=jax.ShapeDtypeStruct((b, s, RET_HEADS * RET_DV), _BF16),
        compiler_params=pltpu.CompilerParams(
            dimension_semantics=("arbitrary", "arbitrary"),
            vmem_limit_bytes=VMEM_LIMIT),
        name="retention",
    )(lg, z3, z3, z3, z3, zm, zm, cos2, sin2, cosm, sinm)


def _fox_kernel(q_ref, k_ref, v_ref, g_ref, c_ref, km_ref, vm_ref, cm_ref, o_ref, bias_scr,
                *, n_tiles):
    half = FOX_HEAD_DIM
    lane_q = lax.broadcasted_iota(jnp.int32, (1, LANES), 1)
    lo = lane_q < half

    for jb in range(n_tiles):
        bias_scr[jb] = -c_ref[0, 0, :, jb * FOX_TK:(jb + 1) * FOX_TK]
    cm = cm_ref[0]
    bias_m = jnp.where(lane_q < N_META, cm[:, LANES - 1:LANES] - cm, NEG_INF)
    km = km_ref[...]
    vm = vm_ref[...]
    ri = lax.broadcasted_iota(jnp.int32, (FOX_TQ, FOX_TK), 0)
    ci = lax.broadcasted_iota(jnp.int32, (FOX_TQ, FOX_TK), 1)
    causal = ci <= ri

    def update(carry, qh, kblk, vblk, bias, mask):
        m, l, acc = carry
        s = _dot_nt(qh, kblk) + bias
        if mask is not None:
            s = jnp.where(mask, s, NEG_INF)
        m_new = jnp.maximum(m, jnp.max(s, axis=-1, keepdims=True))
        alpha = jnp.exp(m - m_new)
        p = jnp.exp(s - m_new)
        l_new = alpha * l + jnp.sum(p, axis=-1, keepdims=True)
        acc_new = alpha * acc + _dot(p.astype(_BF16), vblk)
        return m_new, l_new, acc_new

    def q_tile(i, _):
        r0 = pl.multiple_of(i * FOX_TQ, FOX_TQ)
        rows = pl.ds(r0, FOX_TQ)
        q = q_ref[0, rows, :]
        scale = jnp.asarray(FOX_HEAD_DIM ** -0.5, _BF16)
        zero = jnp.zeros_like(q)
        qs = (jnp.where(lo, q, zero) * scale, jnp.where(lo, zero, q) * scale)

        init = []
        for hh in range(2):
            c0 = (jnp.full((FOX_TQ, 1), NEG_INF, _F32), jnp.zeros((FOX_TQ, 1), _F32),
                  jnp.zeros((FOX_TQ, LANES), _F32))
            init.append(update(c0, qs[hh], km, vm, bias_m[hh:hh + 1, :], None))

        def kv_block(j, carry):
            krows = pl.ds(pl.multiple_of(j * FOX_TK, FOX_TK), FOX_TK)
            kblk = k_ref[0, krows, :]
            vblk = v_ref[0, krows, :]
            bias = bias_scr[j]
            return tuple(update(carry[hh], qs[hh], kblk, vblk, bias[hh:hh + 1, :], None)
                         for hh in range(2))

        carry = lax.fori_loop(0, i, kv_block, tuple(init))
        kblk = k_ref[0, rows, :]
        vblk = v_ref[0, rows, :]
        bias = bias_scr[i]
        outs = []
        for hh in range(2):
            m, l, acc = update(carry[hh], qs[hh], kblk, vblk, bias[hh:hh + 1, :], causal)
            outs.append(acc / l)
        y = jnp.where(lo, outs[0], outs[1])
        gate = g_ref[0, rows, :].astype(_F32)
        o_ref[0, rows, :] = (y * _silu(gate)).astype(_BF16)
        return 0

    lax.fori_loop(0, n_tiles, q_tile, 0)


def _fox(z3, c4, zm, cm3):
    b, s, _ = z3.shape
    n_pairs = FOX_HEADS // 2
    n_tiles = s // FOX_TQ
    kern = functools.partial(_fox_kernel, n_tiles=n_tiles)
    return pl.pallas_call(
        kern,
        grid=(b, n_pairs),
        in_specs=[
            pl.BlockSpec((1, s, LANES), lambda i, p: (i, 0, COL_FQ + p)),
            pl.BlockSpec((1, s, LANES), lambda i, p: (i, 0, COL_FK + p)),
            pl.BlockSpec((1, s, LANES), lambda i, p: (i, 0, COL_FV + p)),
            pl.BlockSpec((1, s, LANES), lambda i, p: (i, 0, COL_FG + p)),
            pl.BlockSpec((1, 1, 2, s), lambda i, p: (i, p, 0, 0)),
            pl.BlockSpec((CHUNK, LANES), lambda i, p: (0, COL_FK + p)),
            pl.BlockSpec((CHUNK, LANES), lambda i, p: (0, COL_FV + p)),
            pl.BlockSpec((1, 2, LANES), lambda i, p: (p, 0, 0)),
        ],
        out_specs=pl.BlockSpec((1, s, LANES), lambda i, p: (i, 0, p)),
        out_shape=jax.ShapeDtypeStruct((b, s, FOX_HEADS * FOX_HEAD_DIM), _BF16),
        scratch_shapes=[pltpu.VMEM((n_tiles, 2, FOX_TK), _F32)],
        compiler_params=pltpu.CompilerParams(
            dimension_semantics=("arbitrary", "arbitrary"),
            vmem_limit_bytes=VMEM_LIMIT),
        name="fox",
    )(z3, z3, z3, z3, c4, zm, zm, cm3)


def _outproj_kernel(yr_ref, yf_ref, x_ref, w_ref, g_ref, o_ref):
    half = w_ref.shape[0] // 2
    hres = x_ref[...] + _dot(yr_ref[...], w_ref[:half, :]) + _dot(yf_ref[...], w_ref[half:, :])
    ms = jnp.mean(hres * hres, axis=-1, keepdims=True)
    o_ref[...] = hres * lax.rsqrt(ms + EPS) * g_ref[...]


def _outproj(yr, yf, x2d, w_out, g, *, tm):
    rows = x2d.shape[0]
    return pl.pallas_call(
        _outproj_kernel,
        grid=(rows // tm,),
        in_specs=[
            pl.BlockSpec((tm, yr.shape[1]), lambda i: (i, 0)),
            pl.BlockSpec((tm, yf.shape[1]), lambda i: (i, 0)),
            pl.BlockSpec((tm, D_MODEL), lambda i: (i, 0)),
            pl.BlockSpec(w_out.shape, lambda i: (0, 0)),
            pl.BlockSpec((1, D_MODEL), lambda i: (0, 0)),
        ],
        out_specs=pl.BlockSpec((tm, D_MODEL), lambda i: (i, 0)),
        out_shape=jax.ShapeDtypeStruct((rows, D_MODEL), _F32),
        compiler_params=pltpu.CompilerParams(
            dimension_semantics=("arbitrary",),
            vmem_limit_bytes=VMEM_LIMIT),
        name="outproj",
    )(yr, yf, x2d, w_out, g)


def _rope_tables(pos):
    inv = ROPE_BASE ** (-jnp.arange(0, RET_DK, 2, dtype=_F32) / RET_DK)
    ang = pos[:, None] * inv[None, :]
    cos, sin = jnp.cos(ang), jnp.sin(ang)
    return jnp.concatenate([cos, cos], axis=-1), jnp.concatenate([-sin, sin], axis=-1)


def kernel(x, meta_tokens, norm_g, w_in, b_f, w_out, final_g):
    b, s, d = x.shape
    assert norm_g.shape[0] == 1 and d == D_MODEL and s % FOX_TQ == 0
    x2d = x.reshape(b * s, d)
    w = w_in[0]
    w_main = w[:, :D_MAIN].astype(_BF16)
    w_ff_t = w[:, D_MAIN:].T.astype(_BF16)
    g = norm_g[0].reshape(1, d)
    bf = b_f[0].reshape(FOX_HEADS, 1)
    meta_pad = jnp.pad(meta_tokens.astype(_F32), ((0, CHUNK - N_META), (0, 0)))

    z, c = _inproj(x2d, g, w_main, w_ff_t, bf, tm=s, tn=512, n_valid=s)
    zm, cm = _inproj(meta_pad, g, w_main, w_ff_t, bf, tm=CHUNK, tn=512, n_valid=N_META)

    z3 = z.reshape(b, s, D_MAIN)
    lg = jnp.log1p(-jnp.exp2(-5.0 - jnp.arange(RET_HEADS, dtype=_F32)))
    cos2, sin2 = _rope_tables(jnp.arange(s, dtype=_F32) + float(N_META))
    cosm, sinm = _rope_tables(jnp.arange(CHUNK, dtype=_F32))
    y_r = _retention(lg, z3, zm, cos2, sin2, cosm, sinm)

    c4 = c.reshape(b, FOX_HEADS // 2, 2, s)
    cm3 = cm.reshape(FOX_HEADS // 2, 2, CHUNK)
    y_f = _fox(z3, c4, zm, cm3)

    out = _outproj(y_r.reshape(b * s, -1), y_f.reshape(b * s, -1), x2d,
                   w_out[0].astype(_BF16), final_g.reshape(1, d), tm=512)
    return out.reshape(b, s, d)
```

```python
import functools

import jax
import jax.numpy as jnp
from jax import lax
from jax.experimental import pallas as pl
from jax.experimental.pallas import tpu as pltpu

D_MODEL = 1024
N_META = 16
CHUNK = 128
RET_HEADS = 4
RET_DK = 128
RET_DV = 256
FOX_HEADS = 16
FOX_HEAD_DIM = 64
ROPE_BASE = 10000.0
EPS = 1e-6
NEG_INF = -1e30
LOG2E = 1.4426950408889634

LANES = 128
D_MAIN = 7168
COL_RQ, COL_RK, COL_RV, COL_RG = 0, 4, 8, 16
COL_FQ, COL_FK, COL_FV, COL_FG = 24, 32, 40, 48
FOX_TQ = 256
FOX_TK = 256
FOX_PAIRS = 2
FOX_ACC_ROWS = FOX_HEAD_DIM + 16
VMEM_LIMIT = 48 * 1024 * 1024

_F32 = jnp.float32
_BF16 = jnp.bfloat16


def _dot(a, b):
    return jnp.dot(a, b, preferred_element_type=_F32)


def _dot_nt(a, b):
    return lax.dot_general(a, b, (((1,), (1,)), ((), ())), preferred_element_type=_F32)


def _silu(g):
    return g * (1.0 / (1.0 + jnp.exp(-g)))


def _inproj_kernel(x_ref, g_ref, w_ref, wff_ref, bf_ref, z_ref, c_ref, u_scr, lf_scr,
                   *, tm, sub, n_valid):
    j = pl.program_id(1)

    @pl.when(j == 0)
    def _():
        g = g_ref[...]
        for r in range(tm // sub):
            rows = pl.ds(r * sub, sub)
            xf = x_ref[rows, :]
            ms = jnp.mean(xf * xf, axis=-1, keepdims=True)
            u = (xf * lax.rsqrt(ms + EPS) * g).astype(_BF16)
            u_scr[rows, :] = u
            lf_scr[:, r * sub:(r + 1) * sub] = _dot_nt(wff_ref[...], u)
        lane = lax.broadcasted_iota(jnp.int32, (FOX_HEADS, LANES), 1)
        carry = jnp.zeros((FOX_HEADS, 1), _F32)
        bf = bf_ref[...]
        for ci in range(tm // LANES):
            v = lf_scr[:, ci * LANES:(ci + 1) * LANES] + bf
            blk = jnp.minimum(v, 0.0) - jnp.log1p(jnp.exp(-jnp.abs(v)))
            if (ci + 1) * LANES > n_valid:
                blk = jnp.where(lane + ci * LANES < n_valid, blk, 0.0)
            sh = 1
            while sh < LANES:
                rolled = pltpu.roll(blk, sh, axis=1)
                blk = blk + jnp.where(lane >= sh, rolled, 0.0)
                sh *= 2
            blk = blk + carry
            c_ref[0, :, ci * LANES:(ci + 1) * LANES] = blk
            carry = blk[:, LANES - 1:LANES]

    w = w_ref[...]
    for r in range(tm // sub):
        rows = pl.ds(r * sub, sub)
        z_ref[rows, :] = _dot(u_scr[rows, :], w).astype(_BF16)


def _inproj(x2d, g, w_main, w_ff_t, b_f, *, tm, tn, n_valid):
    rows = x2d.shape[0]
    sub = min(tm, 256)
    kern = functools.partial(_inproj_kernel, tm=tm, sub=sub, n_valid=n_valid)
    return pl.pallas_call(
        kern,
        grid=(rows // tm, D_MAIN // tn),
        in_specs=[
            pl.BlockSpec((tm, D_MODEL), lambda i, j: (i, 0)),
            pl.BlockSpec((1, D_MODEL), lambda i, j: (0, 0)),
            pl.BlockSpec((D_MODEL, tn), lambda i, j: (0, j)),
            pl.BlockSpec((FOX_HEADS, D_MODEL), lambda i, j: (0, 0)),
            pl.BlockSpec((FOX_HEADS, 1), lambda i, j: (0, 0)),
        ],
        out_specs=[
            pl.BlockSpec((tm, tn), lambda i, j: (i, j)),
            pl.BlockSpec((1, FOX_HEADS, tm), lambda i, j: (i, 0, 0)),
        ],
        out_shape=[
            jax.ShapeDtypeStruct((rows, D_MAIN), _BF16),
            jax.ShapeDtypeStruct((rows // tm, FOX_HEADS, tm), _F32),
        ],
        scratch_shapes=[
            pltpu.VMEM((tm, D_MODEL), _BF16),
            pltpu.VMEM((FOX_HEADS, tm), _F32),
        ],
        compiler_params=pltpu.CompilerParams(
            dimension_semantics=("arbitrary", "arbitrary"),
            vmem_limit_bytes=VMEM_LIMIT),
        name="inproj",
    )(x2d, g, w_main, w_ff_t, b_f)


def _rotary(x, cos2, sin2):
    return x * cos2 + pltpu.roll(x, RET_DK // 2, axis=1) * sin2


def _retention_kernel(lg_ref, q_ref, k_ref, v_ref, g_ref, km_ref, vm_ref,
                      cos_ref, sin_ref, cosm_ref, sinm_ref, o_ref, *, n_chunks):
    h = pl.program_id(1)
    lg = lg_ref[h]
    ri = lax.broadcasted_iota(jnp.int32, (CHUNK, CHUNK), 0)
    ci = lax.broadcasted_iota(jnp.int32, (CHUNK, CHUNK), 1)
    diff = (ri - ci).astype(_F32)
    dmask = jnp.where(diff >= 0, jnp.exp(lg * jnp.maximum(diff, 0.0)), 0.0)
    idx = lax.broadcasted_iota(jnp.int32, (CHUNK, 1), 0).astype(_F32)
    xi = jnp.exp(lg * (idx + 1.0))
    zeta = jnp.exp(lg * (CHUNK - 1.0 - idx))
    chunk_decay = jnp.exp(lg * jnp.full((1, 1), float(CHUNK), _F32))

    zeta_m = jnp.exp(lg * (N_META - 1.0 - idx))
    km = _rotary(km_ref[...].astype(_F32), cosm_ref[...], sinm_ref[...])
    state0 = _dot((km * zeta_m).T.astype(_BF16), vm_ref[...])

    def body(n, state):
        rows = pl.ds(pl.multiple_of(n * CHUNK, CHUNK), CHUNK)
        cos2 = cos_ref[rows, :]
        sin2 = sin_ref[rows, :]
        q = _rotary(q_ref[0, rows, :].astype(_F32), cos2, sin2)
        k = _rotary(k_ref[0, rows, :].astype(_F32), cos2, sin2)
        v = v_ref[0, rows, :]
        qb = q.astype(_BF16)
        scores = _dot_nt(qb, k.astype(_BF16)) * dmask
        out = _dot(scores.astype(_BF16), v)
        out = out + _dot(qb, state.astype(_BF16)) * xi
        new_state = state * chunk_decay + _dot((k * zeta).T.astype(_BF16), v)
        y = out * lax.rsqrt(jnp.mean(out * out, axis=-1, keepdims=True) + EPS)
        gate = g_ref[0, rows, :].astype(_F32)
        o_ref[0, rows, :] = (y * _silu(gate)).astype(_BF16)
        return new_state

    lax.fori_loop(0, n_chunks, body, state0)


def _retention(lg, z3, zm, cos2, sin2, cosm, sinm):
    b, s, _ = z3.shape
    kern = functools.partial(_retention_kernel, n_chunks=s // CHUNK)
    return pl.pallas_call(
        kern,
        grid=(b, RET_HEADS),
        in_specs=[
            pl.BlockSpec(memory_space=pltpu.SMEM),
            pl.BlockSpec((1, s, RET_DK), lambda i, h: (i, 0, COL_RQ + h)),
            pl.BlockSpec((1, s, RET_DK), lambda i, h: (i, 0, COL_RK + h)),
            pl.BlockSpec((1, s, RET_DV), lambda i, h: (i, 0, COL_RV // 2 + h)),
            pl.BlockSpec((1, s, RET_DV), lambda i, h: (i, 0, COL_RG // 2 + h)),
            pl.BlockSpec((CHUNK, RET_DK), lambda i, h: (0, COL_RK + h)),
            pl.BlockSpec((CHUNK, RET_DV), lambda i, h: (0, COL_RV // 2 + h)),
            pl.BlockSpec((s, RET_DK), lambda i, h: (0, 0)),
            pl.BlockSpec((s, RET_DK), lambda i, h: (0, 0)),
            pl.BlockSpec((CHUNK, RET_DK), lambda i, h: (0, 0)),
            pl.BlockSpec((CHUNK, RET_DK), lambda i, h: (0, 0)),
        ],
        out_specs=pl.BlockSpec((1, s, RET_DV), lambda i, h: (i, 0, h)),
        out_shape=jax.ShapeDtypeStruct((b, s, RET_HEADS * RET_DV), _BF16),
        compiler_params=pltpu.CompilerParams(
            dimension_semantics=("arbitrary", "arbitrary"),
            vmem_limit_bytes=VMEM_LIMIT),
        name="retention",
    )(lg, z3, z3, z3, z3, zm, zm, cos2, sin2, cosm, sinm)


def _fox_kernel(q_ref, k_ref, v_ref, g_ref, c_ref, km_ref, vm_ref, cm_ref, o_ref,
                vt_scr, vtm_scr, bias_scr, biasm_scr, qt_scr, acc_scr, m_scr, s_scr, pm_scr,
                *, n_tiles, n_pairs, seq):
    half = FOX_HEAD_DIM
    n_heads = 2 * n_pairs
    sub = lax.broadcasted_iota(jnp.int32, (LANES, 1), 0)

    def column_tile(row):
        return jnp.broadcast_to(row, (LANES, LANES)).T

    for p in range(n_pairs):
        lanes = slice(p * LANES, (p + 1) * LANES)
        for jb in range(n_tiles):
            rows = slice(jb * FOX_TK, (jb + 1) * FOX_TK)
            vt = v_ref[0, rows, lanes].astype(_F32).T.astype(_BF16)
            for hh in range(2):
                h = 2 * p + hh
                vt_scr[h, jb, 0:half, :] = vt[hh * half:(hh + 1) * half, :]
                vt_scr[h, jb, half:FOX_ACC_ROWS, :] = jnp.ones(
                    (FOX_ACC_ROWS - half, FOX_TK), _BF16)
        vtm = vm_ref[:, lanes].astype(_F32).T.astype(_BF16)
        for hh in range(2):
            h = 2 * p + hh
            vtm_scr[h, 0:half, :] = vtm[hh * half:(hh + 1) * half, :]
            vtm_scr[h, half:FOX_ACC_ROWS, :] = jnp.ones((FOX_ACC_ROWS - half, LANES), _BF16)
            for cb in range(seq // LANES):
                crow = c_ref[0, p, hh:hh + 1, cb * LANES:(cb + 1) * LANES]
                bias_scr[h, cb * LANES:(cb + 1) * LANES, :] = column_tile(crow * -LOG2E)
            cm = cm_ref[p, hh:hh + 1, :]
            biasm_scr[h] = column_tile((cm[:, LANES - 1:LANES] - cm) * LOG2E)[0:N_META, :]

    ri = lax.broadcasted_iota(jnp.int32, (FOX_TK, FOX_TQ), 0)
    ci = lax.broadcasted_iota(jnp.int32, (FOX_TK, FOX_TQ), 1)
    causal = ri <= ci

    def group_max(st):
        parts = [st[r:r + 8, :] for r in range(0, st.shape[0], 8)]
        while len(parts) > 1:
            nxt = [jnp.maximum(parts[a], parts[a + 1]) for a in range(0, len(parts) - 1, 2)]
            if len(parts) % 2:
                nxt.append(parts[-1])
            parts = nxt
        return parts[0]

    def update(h, st, pm, vt):
        m_old = m_scr[h]
        m_new = jnp.maximum(m_old, jnp.max(pm, axis=0, keepdims=True))
        alpha = jnp.exp2(m_old - m_new)
        pt = jnp.exp2(st - m_new).astype(_BF16)
        if pt.shape[0] < LANES:
            pt = jnp.concatenate(
                [pt, jnp.zeros((LANES - pt.shape[0], FOX_TQ), _BF16)], axis=0)
        acc_scr[h] = acc_scr[h] * alpha + _dot(vt, pt)
        m_scr[h] = m_new

    def q_tile(i, _):
        rows = pl.ds(pl.multiple_of(i * FOX_TQ, FOX_TQ), FOX_TQ)
        for p in range(n_pairs):
            lanes = slice(p * LANES, (p + 1) * LANES)
            qt = q_ref[0, rows, lanes].astype(_F32).T
            qt_scr[2 * p] = jnp.where(sub < half, qt, 0.0).astype(_BF16)
            qt_scr[2 * p + 1] = jnp.where(sub < half, 0.0, qt).astype(_BF16)
        for h in range(n_heads):
            m_scr[h] = jnp.full((1, FOX_TQ), NEG_INF, _F32)
            acc_scr[h] = jnp.zeros((FOX_ACC_ROWS, FOX_TQ), _F32)

        def scores(j, slot, mask):
            krows = pl.ds(pl.multiple_of(j * FOX_TK, FOX_TK), FOX_TK)
            for p in range(n_pairs):
                kblk = k_ref[0, krows, p * LANES:(p + 1) * LANES]
                for hh in range(2):
                    h = 2 * p + hh
                    bb = bias_scr[h, krows, :]
                    st = _dot(kblk, qt_scr[h]) + jnp.concatenate([bb, bb], axis=1)
                    if mask is not None:
                        st = jnp.where(mask, st, NEG_INF)
                    s_scr[slot, h] = st
                    pm_scr[slot, h] = group_max(st)

        def consume(j, slot):
            for h in range(n_heads):
                update(h, s_scr[slot, h], pm_scr[slot, h], vt_scr[h, j])

        scores(i, 0, causal)

        def step(n, carry):
            consume(jnp.where(n == 0, i, n - 1), n % 2)
            scores(n, (n + 1) % 2, None)
            return carry

        lax.fori_loop(0, i, step, 0)
        consume(jnp.maximum(i - 1, 0), i % 2)
        for p in range(n_pairs):
            km = km_ref[0:N_META, p * LANES:(p + 1) * LANES]
            for hh in range(2):
                h = 2 * p + hh
                bm = biasm_scr[h]
                st = _dot(km, qt_scr[h]) + jnp.concatenate([bm, bm], axis=1)
                update(h, st, group_max(st), vtm_scr[h])

        for p in range(n_pairs):
            lanes = slice(p * LANES, (p + 1) * LANES)
            outs = []
            for hh in range(2):
                acc = acc_scr[2 * p + hh]
                outs.append(acc[0:half, :] / acc[half:half + 1, :])
            y = jnp.concatenate(outs, axis=0).T
            gate = g_ref[0, rows, lanes].astype(_F32)
            o_ref[0, rows, lanes] = (y * _silu(gate)).astype(_BF16)
        return 0

    lax.fori_loop(0, n_tiles, q_tile, 0)


def _fox(z3, c4, zm, cm3):
    b, s, _ = z3.shape
    n_pairs = FOX_PAIRS
    n_groups = FOX_HEADS // 2 // n_pairs
    n_tiles = s // FOX_TQ
    w = n_pairs * LANES
    kern = functools.partial(_fox_kernel, n_tiles=n_tiles, n_pairs=n_pairs, seq=s)
    return pl.pallas_call(
        kern,
        grid=(b, n_groups),
        in_specs=[
            pl.BlockSpec((1, s, w), lambda i, p: (i, 0, COL_FQ // n_pairs + p)),
            pl.BlockSpec((1, s, w), lambda i, p: (i, 0, COL_FK // n_pairs + p)),
            pl.BlockSpec((1, s, w), lambda i, p: (i, 0, COL_FV // n_pairs + p)),
            pl.BlockSpec((1, s, w), lambda i, p: (i, 0, COL_FG // n_pairs + p)),
            pl.BlockSpec((1, n_pairs, 2, s), lambda i, p: (i, p, 0, 0)),
            pl.BlockSpec((CHUNK, w), lambda i, p: (0, COL_FK // n_pairs + p)),
            pl.BlockSpec((CHUNK, w), lambda i, p: (0, COL_FV // n_pairs + p)),
            pl.BlockSpec((n_pairs, 2, LANES), lambda i, p: (p, 0, 0)),
        ],
        out_specs=pl.BlockSpec((1, s, w), lambda i, p: (i, 0, p)),
        out_shape=jax.ShapeDtypeStruct((b, s, FOX_HEADS * FOX_HEAD_DIM), _BF16),
        scratch_shapes=[
            pltpu.VMEM((2 * n_pairs, n_tiles, FOX_ACC_ROWS, FOX_TK), _BF16),
            pltpu.VMEM((2 * n_pairs, FOX_ACC_ROWS, LANES), _BF16),
            pltpu.VMEM((2 * n_pairs, s, LANES), _F32),
            pltpu.VMEM((2 * n_pairs, N_META, LANES), _F32),
            pltpu.VMEM((2 * n_pairs, LANES, FOX_TQ), _BF16),
            pltpu.VMEM((2 * n_pairs, FOX_ACC_ROWS, FOX_TQ), _F32),
            pltpu.VMEM((2 * n_pairs, 1, FOX_TQ), _F32),
            pltpu.VMEM((2, 2 * n_pairs, FOX_TK, FOX_TQ), _F32),
            pltpu.VMEM((2, 2 * n_pairs, 8, FOX_TQ), _F32),
        ],
        compiler_params=pltpu.CompilerParams(
            dimension_semantics=("arbitrary", "arbitrary"),
            vmem_limit_bytes=VMEM_LIMIT),
        name="fox",
    )(z3, z3, z3, z3, c4, zm, zm, cm3)


def _outproj_kernel(yr_ref, yf_ref, x_ref, w_ref, g_ref, o_ref):
    half = w_ref.shape[0] // 2
    hres = x_ref[...] + _dot(yr_ref[...], w_ref[:half, :]) + _dot(yf_ref[...], w_ref[half:, :])
    ms = jnp.mean(hres * hres, axis=-1, keepdims=True)
    o_ref[...] = hres * lax.rsqrt(ms + EPS) * g_ref[...]


def _outproj(yr, yf, x2d, w_out, g, *, tm):
    rows = x2d.shape[0]
    return pl.pallas_call(
        _outproj_kernel,
        grid=(rows // tm,),
        in_specs=[
            pl.BlockSpec((tm, yr.shape[1]), lambda i: (i, 0)),
            pl.BlockSpec((tm, yf.shape[1]), lambda i: (i, 0)),
            pl.BlockSpec((tm, D_MODEL), lambda i: (i, 0)),
            pl.BlockSpec(w_out.shape, lambda i: (0, 0)),
            pl.BlockSpec((1, D_MODEL), lambda i: (0, 0)),
        ],
        out_specs=pl.BlockSpec((tm, D_MODEL), lambda i: (i, 0)),
        out_shape=jax.ShapeDtypeStruct((rows, D_MODEL), _F32),
        compiler_params=pltpu.CompilerParams(
            dimension_semantics=("arbitrary",),
            vmem_limit_bytes=VMEM_LIMIT),
        name="outproj",
    )(yr, yf, x2d, w_out, g)


def _rope_tables(pos):
    inv = ROPE_BASE ** (-jnp.arange(0, RET_DK, 2, dtype=_F32) / RET_DK)
    ang = pos[:, None] * inv[None, :]
    cos, sin = jnp.cos(ang), jnp.sin(ang)
    return jnp.concatenate([cos, cos], axis=-1), jnp.concatenate([-sin, sin], axis=-1)


def kernel(x, meta_tokens, norm_g, w_in, b_f, w_out, final_g):
    b, s, d = x.shape
    assert norm_g.shape[0] == 1 and d == D_MODEL and s % FOX_TQ == 0
    x2d = x.reshape(b * s, d)
    w = w_in[0]
    col_scale = jnp.ones((D_MAIN,), _F32)
    col_scale = col_scale.at[COL_RK * LANES:COL_RV * LANES].set(RET_DK ** -0.5)
    col_scale = col_scale.at[COL_FQ * LANES:COL_FK * LANES].set(FOX_HEAD_DIM ** -0.5 * LOG2E)
    w_main = (w[:, :D_MAIN] * col_scale).astype(_BF16)
    w_ff_t = w[:, D_MAIN:].T.astype(_BF16)
    g = norm_g[0].reshape(1, d)
    bf = b_f[0].reshape(FOX_HEADS, 1)
    meta_pad = jnp.pad(meta_tokens.astype(_F32), ((0, CHUNK - N_META), (0, 0)))

    z, c = _inproj(x2d, g, w_main, w_ff_t, bf, tm=s, tn=512, n_valid=s)
    zm, cm = _inproj(meta_pad, g, w_main, w_ff_t, bf, tm=CHUNK, tn=512, n_valid=N_META)

    z3 = z.reshape(b, s, D_MAIN)
    lg = jnp.log1p(-jnp.exp2(-5.0 - jnp.arange(RET_HEADS, dtype=_F32)))
    cos2, sin2 = _rope_tables(jnp.arange(s, dtype=_F32) + float(N_META))
    cosm, sinm = _rope_tables(jnp.arange(CHUNK, dtype=_F32))
    y_r = _retention(lg, z3, zm, cos2, sin2, cosm, sinm)

    c4 = c.reshape(b, FOX_HEADS // 2, 2, s)
    cm3 = cm.reshape(FOX_HEADS // 2, 2, CHUNK)
    y_f = _fox(z3, c4, zm, cm3)

    out = _outproj(y_r.reshape(b * s, -1), y_f.reshape(b * s, -1), x2d,
                   w_out[0].astype(_BF16), final_g.reshape(1, d), tm=512)
    return out.reshape(b, s, d)
```

```python
import functools

import jax
import jax.numpy as jnp
from jax import lax
from jax.experimental import pallas as pl
from jax.experimental.pallas import tpu as pltpu

D_MODEL = 1024
N_META = 16
CHUNK = 128
RET_HEADS = 4
RET_DK = 128
RET_DV = 256
FOX_HEADS = 16
FOX_HEAD_DIM = 64
ROPE_BASE = 10000.0
EPS = 1e-6
NEG_INF = -1e30
LOG2E = 1.4426950408889634

LANES = 128
D_MAIN = 7168
COL_RQ, COL_RK, COL_RV, COL_RG = 0, 4, 8, 16
COL_FQ, COL_FK, COL_FV, COL_FG = 24, 32, 40, 48
FOX_TQ = 256
FOX_TK = 256
FOX_PAIRS = 2
FOX_ACC_ROWS = FOX_HEAD_DIM + 16
VMEM_LIMIT = 48 * 1024 * 1024

_F32 = jnp.float32
_BF16 = jnp.bfloat16


def _dot(a, b):
    return jnp.dot(a, b, preferred_element_type=_F32)


def _dot_nt(a, b):
    return lax.dot_general(a, b, (((1,), (1,)), ((), ())), preferred_element_type=_F32)


def _silu(g):
    return g * (1.0 / (1.0 + jnp.exp(-g)))


def _inproj_kernel(x_ref, g_ref, w_ref, wff_ref, bf_ref, z_ref, c_ref, u_scr, lf_scr,
                   *, tm, sub, n_valid):
    j = pl.program_id(1)

    @pl.when(j == 0)
    def _():
        g = g_ref[...]
        for r in range(tm // sub):
            rows = pl.ds(r * sub, sub)
            xf = x_ref[rows, :]
            ms = jnp.mean(xf * xf, axis=-1, keepdims=True)
            u = (xf * lax.rsqrt(ms + EPS) * g).astype(_BF16)
            u_scr[rows, :] = u
            lf_scr[:, r * sub:(r + 1) * sub] = _dot_nt(wff_ref[...], u)
        lane = lax.broadcasted_iota(jnp.int32, (FOX_HEADS, LANES), 1)
        carry = jnp.zeros((FOX_HEADS, 1), _F32)
        bf = bf_ref[...]
        for ci in range(tm // LANES):
            v = lf_scr[:, ci * LANES:(ci + 1) * LANES] + bf
            blk = jnp.minimum(v, 0.0) - jnp.log1p(jnp.exp(-jnp.abs(v)))
            if (ci + 1) * LANES > n_valid:
                blk = jnp.where(lane + ci * LANES < n_valid, blk, 0.0)
            sh = 1
            while sh < LANES:
                rolled = pltpu.roll(blk, sh, axis=1)
                blk = blk + jnp.where(lane >= sh, rolled, 0.0)
                sh *= 2
            blk = blk + carry
            c_ref[0, :, ci * LANES:(ci + 1) * LANES] = blk
            carry = blk[:, LANES - 1:LANES]

    w = w_ref[...]
    for r in range(tm // sub):
        rows = pl.ds(r * sub, sub)
        z_ref[rows, :] = _dot(u_scr[rows, :], w).astype(_BF16)


def _inproj(x2d, g, w_main, w_ff_t, b_f, *, tm, tn, n_valid):
    rows = x2d.shape[0]
    sub = min(tm, 256)
    kern = functools.partial(_inproj_kernel, tm=tm, sub=sub, n_valid=n_valid)
    return pl.pallas_call(
        kern,
        grid=(rows // tm, D_MAIN // tn),
        in_specs=[
            pl.BlockSpec((tm, D_MODEL), lambda i, j: (i, 0)),
            pl.BlockSpec((1, D_MODEL), lambda i, j: (0, 0)),
            pl.BlockSpec((D_MODEL, tn), lambda i, j: (0, j)),
            pl.BlockSpec((FOX_HEADS, D_MODEL), lambda i, j: (0, 0)),
            pl.BlockSpec((FOX_HEADS, 1), lambda i, j: (0, 0)),
        ],
        out_specs=[
            pl.BlockSpec((tm, tn), lambda i, j: (i, j)),
            pl.BlockSpec((1, FOX_HEADS, tm), lambda i, j: (i, 0, 0)),
        ],
        out_shape=[
            jax.ShapeDtypeStruct((rows, D_MAIN), _BF16),
            jax.ShapeDtypeStruct((rows // tm, FOX_HEADS, tm), _F32),
        ],
        scratch_shapes=[
            pltpu.VMEM((tm, D_MODEL), _BF16),
            pltpu.VMEM((FOX_HEADS, tm), _F32),
        ],
        compiler_params=pltpu.CompilerParams(
            dimension_semantics=("arbitrary", "arbitrary"),
            vmem_limit_bytes=VMEM_LIMIT),
        name="inproj",
    )(x2d, g, w_main, w_ff_t, b_f)


def _rotary(x, cos2, sin2):
    return x * cos2 + pltpu.roll(x, RET_DK // 2, axis=1) * sin2


def _retention_kernel(lg_ref, q_ref, k_ref, v_ref, g_ref, km_ref, vm_ref,
                      cos_ref, sin_ref, cosm_ref, sinm_ref, o_ref, *, n_chunks):
    h = pl.program_id(1)
    lg = lg_ref[h]
    ri = lax.broadcasted_iota(jnp.int32, (CHUNK, CHUNK), 0)
    ci = lax.broadcasted_iota(jnp.int32, (CHUNK, CHUNK), 1)
    diff = (ri - ci).astype(_F32)
    dmask = jnp.where(diff >= 0, jnp.exp(lg * jnp.maximum(diff, 0.0)), 0.0)
    idx = lax.broadcasted_iota(jnp.int32, (CHUNK, 1), 0).astype(_F32)
    xi = jnp.exp(lg * (idx + 1.0))
    zeta = jnp.exp(lg * (CHUNK - 1.0 - idx))
    chunk_decay = jnp.exp(lg * jnp.full((1, 1), float(CHUNK), _F32))

    zeta_m = jnp.exp(lg * (N_META - 1.0 - idx))
    km = _rotary(km_ref[...].astype(_F32), cosm_ref[...], sinm_ref[...])
    state0 = _dot((km * zeta_m).T.astype(_BF16), vm_ref[...])

    def body(n, state):
        rows = pl.ds(pl.multiple_of(n * CHUNK, CHUNK), CHUNK)
        cos2 = cos_ref[rows, :]
        sin2 = sin_ref[rows, :]
        q = _rotary(q_ref[0, rows, :].astype(_F32), cos2, sin2)
        k = _rotary(k_ref[0, rows, :].astype(_F32), cos2, sin2)
        v = v_ref[0, rows, :]
        qb = q.astype(_BF16)
        scores = _dot_nt(qb, k.astype(_BF16)) * dmask
        out = _dot(scores.astype(_BF16), v)
        out = out + _dot(qb, state.astype(_BF16)) * xi
        new_state = state * chunk_decay + _dot((k * zeta).T.astype(_BF16), v)
        y = out * lax.rsqrt(jnp.mean(out * out, axis=-1, keepdims=True) + EPS)
        gate = g_ref[0, rows, :].astype(_F32)
        o_ref[0, rows, :] = (y * _silu(gate)).astype(_BF16)
        return new_state

    state = state0
    for n in range(n_chunks):
        state = body(n, state)


def _retention(lg, z3, zm, cos2, sin2, cosm, sinm):
    b, s, _ = z3.shape
    kern = functools.partial(_retention_kernel, n_chunks=s // CHUNK)
    return pl.pallas_call(
        kern,
        grid=(b, RET_HEADS),
        in_specs=[
            pl.BlockSpec(memory_space=pltpu.SMEM),
            pl.BlockSpec((1, s, RET_DK), lambda i, h: (i, 0, COL_RQ + h)),
            pl.BlockSpec((1, s, RET_DK), lambda i, h: (i, 0, COL_RK + h)),
            pl.BlockSpec((1, s, RET_DV), lambda i, h: (i, 0, COL_RV // 2 + h)),
            pl.BlockSpec((1, s, RET_DV), lambda i, h: (i, 0, COL_RG // 2 + h)),
            pl.BlockSpec((CHUNK, RET_DK), lambda i, h: (0, COL_RK + h)),
            pl.BlockSpec((CHUNK, RET_DV), lambda i, h: (0, COL_RV // 2 + h)),
            pl.BlockSpec((s, RET_DK), lambda i, h: (0, 0)),
            pl.BlockSpec((s, RET_DK), lambda i, h: (0, 0)),
            pl.BlockSpec((CHUNK, RET_DK), lambda i, h: (0, 0)),
            pl.BlockSpec((CHUNK, RET_DK), lambda i, h: (0, 0)),
        ],
        out_specs=pl.BlockSpec((1, s, RET_DV), lambda i, h: (i, 0, h)),
        out_shape=jax.ShapeDtypeStruct((b, s, RET_HEADS * RET_DV), _BF16),
        compiler_params=pltpu.CompilerParams(
            dimension_semantics=("arbitrary", "arbitrary"),
            vmem_limit_bytes=VMEM_LIMIT),
        name="retention",
    )(lg, z3, z3, z3, z3, zm, zm, cos2, sin2, cosm, sinm)


def _fox_kernel(q_ref, k_ref, v_ref, g_ref, c_ref, km_ref, vm_ref, cm_ref, o_ref,
                vt_scr, vtm_scr, bias_scr, biasm_scr, qt_scr, acc_scr, m_scr, s_scr, pm_scr,
                sm_scr, pmm_scr, *, n_tiles, n_pairs, seq):
    half = FOX_HEAD_DIM
    n_heads = 2 * n_pairs
    sub = lax.broadcasted_iota(jnp.int32, (LANES, 1), 0)

    def column_tile(row):
        return jnp.broadcast_to(row, (LANES, LANES)).T

    for p in range(n_pairs):
        lanes = slice(p * LANES, (p + 1) * LANES)
        for jb in range(n_tiles):
            rows = slice(jb * FOX_TK, (jb + 1) * FOX_TK)
            vt = v_ref[0, rows, lanes].astype(_F32).T.astype(_BF16)
            for hh in range(2):
                h = 2 * p + hh
                vt_scr[h, jb, 0:half, :] = vt[hh * half:(hh + 1) * half, :]
                vt_scr[h, jb, half:FOX_ACC_ROWS, :] = jnp.ones(
                    (FOX_ACC_ROWS - half, FOX_TK), _BF16)
        vtm = vm_ref[:, lanes].astype(_F32).T.astype(_BF16)
        for hh in range(2):
            h = 2 * p + hh
            vtm_scr[h, 0:half, :] = vtm[hh * half:(hh + 1) * half, :]
            vtm_scr[h, half:FOX_ACC_ROWS, :] = jnp.ones((FOX_ACC_ROWS - half, LANES), _BF16)
            for cb in range(seq // LANES):
                crow = c_ref[0, p, hh:hh + 1, cb * LANES:(cb + 1) * LANES]
                bias_scr[h, cb * LANES:(cb + 1) * LANES, :] = column_tile(crow * -LOG2E)
            cm = cm_ref[p, hh:hh + 1, :]
            biasm_scr[h] = column_tile((cm[:, LANES - 1:LANES] - cm) * LOG2E)[0:N_META, :]

    ri = lax.broadcasted_iota(jnp.int32, (FOX_TK, FOX_TQ), 0)
    ci = lax.broadcasted_iota(jnp.int32, (FOX_TK, FOX_TQ), 1)
    causal = ri <= ci

    def group_max(st):
        parts = [st[r:r + 8, :] for r in range(0, st.shape[0], 8)]
        while len(parts) > 1:
            nxt = [jnp.maximum(parts[a], parts[a + 1]) for a in range(0, len(parts) - 1, 2)]
            if len(parts) % 2:
                nxt.append(parts[-1])
            parts = nxt
        return parts[0]

    def update(h, tile, first, st, pm, vt):
        buf = tile % 2
        m_blk = jnp.max(pm, axis=0, keepdims=True)
        if first:
            m_new = m_blk
        else:
            m_old = m_scr[buf, h]
            m_new = jnp.maximum(m_old, m_blk)
            alpha = jnp.exp2(m_old - m_new)
        pt = jnp.exp2(st - m_new).astype(_BF16)
        if pt.shape[0] < LANES:
            pt = jnp.concatenate(
                [pt, jnp.zeros((LANES - pt.shape[0], FOX_TQ), _BF16)], axis=0)
        pv = _dot(vt, pt)
        acc_scr[buf, h] = pv if first else acc_scr[buf, h] * alpha + pv
        m_scr[buf, h] = m_new

    def prepare_queries(i):
        rows = slice(i * FOX_TQ, (i + 1) * FOX_TQ)
        for p in range(n_pairs):
            qt = q_ref[0, rows, p * LANES:(p + 1) * LANES].astype(_F32).T
            qt_scr[i % 2, 2 * p] = jnp.where(sub < half, qt, 0.0).astype(_BF16)
            qt_scr[i % 2, 2 * p + 1] = jnp.where(sub < half, 0.0, qt).astype(_BF16)

    def scores(item):
        kind, i, j, slot = item
        if kind == "meta":
            for p in range(n_pairs):
                km = km_ref[0:N_META, p * LANES:(p + 1) * LANES]
                for hh in range(2):
                    h = 2 * p + hh
                    bm = biasm_scr[h]
                    st = _dot(km, qt_scr[i % 2, h]) + jnp.concatenate([bm, bm], axis=1)
                    sm_scr[h] = st
                    pmm_scr[h] = group_max(st)
            return
        krows = slice(j * FOX_TK, (j + 1) * FOX_TK)
        for p in range(n_pairs):
            kblk = k_ref[0, krows, p * LANES:(p + 1) * LANES]
            for hh in range(2):
                h = 2 * p + hh
                bb = bias_scr[h, krows, :]
                st = _dot(kblk, qt_scr[i % 2, h]) + jnp.concatenate([bb, bb], axis=1)
                if i == j:
                    st = jnp.where(causal, st, NEG_INF)
                s_scr[slot, h] = st
                pm_scr[slot, h] = group_max(st)

    def consume(item):
        kind, i, j, slot = item
        for h in range(n_heads):
            if kind == "meta":
                update(h, i, True, sm_scr[h], pmm_scr[h], vtm_scr[h])
            else:
                update(h, i, False, s_scr[slot, h], pm_scr[slot, h], vt_scr[h, j])

    def finalize(i):
        rows = slice(i * FOX_TQ, (i + 1) * FOX_TQ)
        for p in range(n_pairs):
            lanes = slice(p * LANES, (p + 1) * LANES)
            outs = []
            for hh in range(2):
                acc = acc_scr[i % 2, 2 * p + hh]
                outs.append(acc[0:half, :] / acc[half:half + 1, :])
            y = jnp.concatenate(outs, axis=0).T
            gate = g_ref[0, rows, lanes].astype(_F32)
            o_ref[0, rows, lanes] = (y * _silu(gate)).astype(_BF16)

    items = []
    for i in range(n_tiles):
        items.append(("meta", i, None, None))
        for j in [i] + list(range(i)):
            items.append(("blk", i, j, sum(it[0] == "blk" for it in items) % 2))
    prepare_queries(0)
    scores(items[0])
    for t, item in enumerate(items):
        if t + 1 < len(items):
            nxt = items[t + 1]
            if nxt[0] == "meta":
                prepare_queries(nxt[1])
            scores(nxt)
        consume(item)
        if t + 1 == len(items) or items[t + 1][1] != item[1]:
            finalize(item[1])


def _fox(z3, c4, zm, cm3):
    b, s, _ = z3.shape
    n_pairs = FOX_PAIRS
    n_groups = FOX_HEADS // 2 // n_pairs
    n_tiles = s // FOX_TQ
    w = n_pairs * LANES
    kern = functools.partial(_fox_kernel, n_tiles=n_tiles, n_pairs=n_pairs, seq=s)
    return pl.pallas_call(
        kern,
        grid=(b, n_groups),
        in_specs=[
            pl.BlockSpec((1, s, w), lambda i, p: (i, 0, COL_FQ // n_pairs + p)),
            pl.BlockSpec((1, s, w), lambda i, p: (i, 0, COL_FK // n_pairs + p)),
            pl.BlockSpec((1, s, w), lambda i, p: (i, 0, COL_FV // n_pairs + p)),
            pl.BlockSpec((1, s, w), lambda i, p: (i, 0, COL_FG // n_pairs + p)),
            pl.BlockSpec((1, n_pairs, 2, s), lambda i, p: (i, p, 0, 0)),
            pl.BlockSpec((CHUNK, w), lambda i, p: (0, COL_FK // n_pairs + p)),
            pl.BlockSpec((CHUNK, w), lambda i, p: (0, COL_FV // n_pairs + p)),
            pl.BlockSpec((n_pairs, 2, LANES), lambda i, p: (p, 0, 0)),
        ],
        out_specs=pl.BlockSpec((1, s, w), lambda i, p: (i, 0, p)),
        out_shape=jax.ShapeDtypeStruct((b, s, FOX_HEADS * FOX_HEAD_DIM), _BF16),
        scratch_shapes=[
            pltpu.VMEM((2 * n_pairs, n_tiles, FOX_ACC_ROWS, FOX_TK), _BF16),
            pltpu.VMEM((2 * n_pairs, FOX_ACC_ROWS, LANES), _BF16),
            pltpu.VMEM((2 * n_pairs, s, LANES), _F32),
            pltpu.VMEM((2 * n_pairs, N_META, LANES), _F32),
            pltpu.VMEM((2, 2 * n_pairs, LANES, FOX_TQ), _BF16),
            pltpu.VMEM((2, 2 * n_pairs, FOX_ACC_ROWS, FOX_TQ), _F32),
            pltpu.VMEM((2, 2 * n_pairs, 1, FOX_TQ), _F32),
            pltpu.VMEM((2, 2 * n_pairs, FOX_TK, FOX_TQ), _F32),
            pltpu.VMEM((2, 2 * n_pairs, 8, FOX_TQ), _F32),
            pltpu.VMEM((2 * n_pairs, N_META, FOX_TQ), _F32),
            pltpu.VMEM((2 * n_pairs, 8, FOX_TQ), _F32),
        ],
        compiler_params=pltpu.CompilerParams(
            dimension_semantics=("arbitrary", "arbitrary"),
            vmem_limit_bytes=VMEM_LIMIT),
        name="fox",
    )(z3, z3, z3, z3, c4, zm, zm, cm3)


def _outproj_kernel(yr_ref, yf_ref, x_ref, w_ref, g_ref, o_ref):
    half = w_ref.shape[0] // 2
    hres = x_ref[...] + _dot(yr_ref[...], w_ref[:half, :]) + _dot(yf_ref[...], w_ref[half:, :])
    ms = jnp.mean(hres * hres, axis=-1, keepdims=True)
    o_ref[...] = hres * lax.rsqrt(ms + EPS) * g_ref[...]


def _outproj(yr, yf, x2d, w_out, g, *, tm):
    rows = x2d.shape[0]
    return pl.pallas_call(
        _outproj_kernel,
        grid=(rows // tm,),
        in_specs=[
            pl.BlockSpec((tm, yr.shape[1]), lambda i: (i, 0)),
            pl.BlockSpec((tm, yf.shape[1]), lambda i: (i, 0)),
            pl.BlockSpec((tm, D_MODEL), lambda i: (i, 0)),
            pl.BlockSpec(w_out.shape, lambda i: (0, 0)),
            pl.BlockSpec((1, D_MODEL), lambda i: (0, 0)),
        ],
        out_specs=pl.BlockSpec((tm, D_MODEL), lambda i: (i, 0)),
        out_shape=jax.ShapeDtypeStruct((rows, D_MODEL), _F32),
        compiler_params=pltpu.CompilerParams(
            dimension_semantics=("arbitrary",),
            vmem_limit_bytes=VMEM_LIMIT),
        name="outproj",
    )(yr, yf, x2d, w_out, g)


def _rope_tables(pos):
    inv = ROPE_BASE ** (-jnp.arange(0, RET_DK, 2, dtype=_F32) / RET_DK)
    ang = pos[:, None] * inv[None, :]
    cos, sin = jnp.cos(ang), jnp.sin(ang)
    return jnp.concatenate([cos, cos], axis=-1), jnp.concatenate([-sin, sin], axis=-1)


def kernel(x, meta_tokens, norm_g, w_in, b_f, w_out, final_g):
    b, s, d = x.shape
    assert norm_g.shape[0] == 1 and d == D_MODEL and s % FOX_TQ == 0
    x2d = x.reshape(b * s, d)
    w = w_in[0]
    col_scale = jnp.ones((D_MAIN,), _F32)
    col_scale = col_scale.at[COL_RK * LANES:COL_RV * LANES].set(RET_DK ** -0.5)
    col_scale = col_scale.at[COL_FQ * LANES:COL_FK * LANES].set(FOX_HEAD_DIM ** -0.5 * LOG2E)
    w_main = (w[:, :D_MAIN] * col_scale).astype(_BF16)
    w_ff_t = w[:, D_MAIN:].T.astype(_BF16)
    g = norm_g[0].reshape(1, d)
    bf = b_f[0].reshape(FOX_HEADS, 1)
    meta_pad = jnp.pad(meta_tokens.astype(_F32), ((0, CHUNK - N_META), (0, 0)))

    z, c = _inproj(x2d, g, w_main, w_ff_t, bf, tm=s, tn=512, n_valid=s)
    zm, cm = _inproj(meta_pad, g, w_main, w_ff_t, bf, tm=CHUNK, tn=512, n_valid=N_META)

    z3 = z.reshape(b, s, D_MAIN)
    lg = jnp.log1p(-jnp.exp2(-5.0 - jnp.arange(RET_HEADS, dtype=_F32)))
    cos2, sin2 = _rope_tables(jnp.arange(s, dtype=_F32) + float(N_META))
    cosm, sinm = _rope_tables(jnp.arange(CHUNK, dtype=_F32))
    y_r = _retention(lg, z3, zm, cos2, sin2, cosm, sinm)

    c4 = c.reshape(b, FOX_HEADS // 2, 2, s)
    cm3 = cm.reshape(FOX_HEADS // 2, 2, CHUNK)
    y_f = _fox(z3, c4, zm, cm3)

    out = _outproj(y_r.reshape(b * s, -1), y_f.reshape(b * s, -1), x2d,
                   w_out[0].astype(_BF16), final_g.reshape(1, d), tm=512)
    return out.reshape(b, s, d)
```

```python
import functools

import jax
import jax.numpy as jnp
from jax import lax
from jax.experimental import pallas as pl
from jax.experimental.pallas import tpu as pltpu

D_MODEL = 1024
N_META = 16
CHUNK = 128
RET_HEADS = 4
RET_DK = 128
RET_DV = 256
FOX_HEADS = 16
FOX_HEAD_DIM = 64
ROPE_BASE = 10000.0
EPS = 1e-6
NEG_INF = -1e30
LOG2E = 1.4426950408889634

LANES = 128
D_MAIN = 7168
COL_RQ, COL_RK, COL_RV, COL_RG = 0, 4, 8, 16
COL_FQ, COL_FK, COL_FV, COL_FG = 24, 32, 40, 48
FOX_TQ = 256
FOX_TK = 256
FOX_PAIRS = 2
FOX_ACC_ROWS = FOX_HEAD_DIM + 16
VMEM_LIMIT = 48 * 1024 * 1024

_F32 = jnp.float32
_BF16 = jnp.bfloat16


def _dot(a, b):
    return jnp.dot(a, b, preferred_element_type=_F32)


def _dot_nt(a, b):
    return lax.dot_general(a, b, (((1,), (1,)), ((), ())), preferred_element_type=_F32)


def _silu(g):
    return g * (1.0 / (1.0 + jnp.exp(-g)))


def _rotary(x, cos2, sin2):
    return x * cos2 + pltpu.roll(x, RET_DK // 2, axis=1) * sin2


def _inproj_kernel(x_ref, g_ref, w_ref, cs_ref, wff_ref, bf_ref, cos_ref, sin_ref,
                   z_ref, c_ref, u_scr, w_scr, lf_scr, *, tm, tn, sub, n_valid):
    j = pl.program_id(1)
    n_sub = tm // sub

    def normalize(r):
        rows = pl.ds(r * sub, sub)
        xf = x_ref[rows, :]
        ms = jnp.mean(xf * xf, axis=-1, keepdims=True)
        u = (xf * lax.rsqrt(ms + EPS) * g_ref[...]).astype(_BF16)
        u_scr[rows, :] = u
        lf_scr[:, r * sub:(r + 1) * sub] = _dot_nt(wff_ref[...], u)

    def project(r, rotate):
        if r == 0:
            w_scr[...] = (w_ref[0] * cs_ref[...]).astype(_BF16)
        rows = pl.ds(r * sub, sub)
        acc = _dot(u_scr[rows, :], w_scr[...])
        if rotate:
            cos2, sin2 = cos_ref[rows, :], sin_ref[rows, :]
            acc = jnp.concatenate(
                [_rotary(acc[:, a:a + RET_DK], cos2, sin2) for a in range(0, tn, RET_DK)],
                axis=1)
        z_ref[rows, :] = acc.astype(_BF16)

    def forget_cumsum():
        lane = lax.broadcasted_iota(jnp.int32, (FOX_HEADS, LANES), 1)
        carry = jnp.zeros((FOX_HEADS, 1), _F32)
        bf = bf_ref[...]
        for ci in range(tm // LANES):
            v = lf_scr[:, ci * LANES:(ci + 1) * LANES] + bf
            blk = jnp.minimum(v, 0.0) - jnp.log1p(jnp.exp(-jnp.abs(v)))
            if (ci + 1) * LANES > n_valid:
                blk = jnp.where(lane + ci * LANES < n_valid, blk, 0.0)
            sh = 1
            while sh < LANES:
                rolled = pltpu.roll(blk, sh, axis=1)
                blk = blk + jnp.where(lane >= sh, rolled, 0.0)
                sh *= 2
            blk = blk + carry
            c_ref[0, :, ci * LANES:(ci + 1) * LANES] = blk
            carry = blk[:, LANES - 1:LANES]

    rot_tiles = (COL_RV * LANES) // tn

    @pl.when(j == 0)
    def _():
        for r in range(n_sub):
            normalize(r)
            project(r, True)
        forget_cumsum()

    @pl.when((j > 0) & (j < rot_tiles))
    def _():
        for r in range(n_sub):
            project(r, True)

    @pl.when(j >= rot_tiles)
    def _():
        for r in range(n_sub):
            project(r, False)


def _inproj(x2d, g, w_in, col_scale, w_ff_t, b_f, cos2, sin2, *, tm, tn, n_valid):
    rows = x2d.shape[0]
    sub = min(tm, 256)
    assert (COL_RV * LANES) % tn == 0 and tn % RET_DK == 0
    kern = functools.partial(_inproj_kernel, tm=tm, tn=tn, sub=sub, n_valid=n_valid)
    return pl.pallas_call(
        kern,
        grid=(rows // tm, D_MAIN // tn),
        in_specs=[
            pl.BlockSpec((tm, D_MODEL), lambda i, j: (i, 0)),
            pl.BlockSpec((1, D_MODEL), lambda i, j: (0, 0)),
            pl.BlockSpec((1, D_MODEL, tn), lambda i, j: (0, 0, j)),
            pl.BlockSpec((1, tn), lambda i, j: (0, j)),
            pl.BlockSpec((FOX_HEADS, D_MODEL), lambda i, j: (0, 0)),
            pl.BlockSpec((FOX_HEADS, 1), lambda i, j: (0, 0)),
            pl.BlockSpec((tm, RET_DK), lambda i, j: (0, 0)),
            pl.BlockSpec((tm, RET_DK), lambda i, j: (0, 0)),
        ],
        out_specs=[
            pl.BlockSpec((tm, tn), lambda i, j: (i, j)),
            pl.BlockSpec((1, FOX_HEADS, tm), lambda i, j: (i, 0, 0)),
        ],
        out_shape=[
            jax.ShapeDtypeStruct((rows, D_MAIN), _BF16),
            jax.ShapeDtypeStruct((rows // tm, FOX_HEADS, tm), _F32),
        ],
        scratch_shapes=[
            pltpu.VMEM((tm, D_MODEL), _BF16),
            pltpu.VMEM((D_MODEL, tn), _BF16),
            pltpu.VMEM((FOX_HEADS, tm), _F32),
        ],
        compiler_params=pltpu.CompilerParams(
            dimension_semantics=("arbitrary", "arbitrary"),
            vmem_limit_bytes=VMEM_LIMIT),
        name="inproj",
    )(x2d, g, w_in, col_scale, w_ff_t, b_f, cos2, sin2)


def _retention_kernel(lg_ref, q_ref, k_ref, v_ref, g_ref, km_ref, vm_ref, o_ref, *, n_chunks):
    h = pl.program_id(1)
    lg = lg_ref[h]
    ri = lax.broadcasted_iota(jnp.int32, (CHUNK, CHUNK), 0)
    ci = lax.broadcasted_iota(jnp.int32, (CHUNK, CHUNK), 1)
    diff = (ri - ci).astype(_F32)
    dmask = jnp.where(diff >= 0, jnp.exp(lg * jnp.maximum(diff, 0.0)), 0.0)
    idx = lax.broadcasted_iota(jnp.int32, (CHUNK, 1), 0).astype(_F32)
    xi = jnp.exp(lg * (idx + 1.0))
    zeta = jnp.exp(lg * (CHUNK - 1.0 - idx))
    chunk_decay = jnp.exp(lg * jnp.full((1, 1), float(CHUNK), _F32))

    zeta_m = jnp.exp(lg * (N_META - 1.0 - idx))
    km = km_ref[...].astype(_F32)
    state0 = _dot((km * zeta_m).T.astype(_BF16), vm_ref[...])

    def chunk_products(n):
        rows = slice(n * CHUNK, (n + 1) * CHUNK)
        kb = k_ref[0, rows, :]
        v = v_ref[0, rows, :]
        scores = (_dot_nt(q_ref[0, rows, :], kb) * dmask).astype(_BF16)
        kv = _dot((kb.astype(_F32) * zeta).T.astype(_BF16), v)
        return scores, kv

    def chunk_output(n, scores, state):
        rows = slice(n * CHUNK, (n + 1) * CHUNK)
        out = _dot(scores, v_ref[0, rows, :])
        out = out + _dot(q_ref[0, rows, :], state.astype(_BF16)) * xi
        y = out * lax.rsqrt(jnp.mean(out * out, axis=-1, keepdims=True) + EPS)
        gate = g_ref[0, rows, :].astype(_F32)
        o_ref[0, rows, :] = (y * _silu(gate)).astype(_BF16)

    state = state0
    scores, kv = chunk_products(0)
    for n in range(n_chunks):
        if n + 1 < n_chunks:
            nxt = chunk_products(n + 1)
        chunk_output(n, scores, state)
        state = state * chunk_decay + kv
        if n + 1 < n_chunks:
            scores, kv = nxt


def _retention(lg, z3, zm):
    b, s, _ = z3.shape
    kern = functools.partial(_retention_kernel, n_chunks=s // CHUNK)
    return pl.pallas_call(
        kern,
        grid=(b, RET_HEADS),
        in_specs=[
            pl.BlockSpec(memory_space=pltpu.SMEM),
            pl.BlockSpec((1, s, RET_DK), lambda i, h: (i, 0, COL_RQ + h)),
            pl.BlockSpec((1, s, RET_DK), lambda i, h: (i, 0, COL_RK + h)),
            pl.BlockSpec((1, s, RET_DV), lambda i, h: (i, 0, COL_RV // 2 + h)),
            pl.BlockSpec((1, s, RET_DV), lambda i, h: (i, 0, COL_RG // 2 + h)),
            pl.BlockSpec((CHUNK, RET_DK), lambda i, h: (0, COL_RK + h)),
            pl.BlockSpec((CHUNK, RET_DV), lambda i, h: (0, COL_RV // 2 + h)),
        ],
        out_specs=pl.BlockSpec((1, s, RET_DV), lambda i, h: (i, 0, h)),
        out_shape=jax.ShapeDtypeStruct((b, s, RET_HEADS * RET_DV), _BF16),
        compiler_params=pltpu.CompilerParams(
            dimension_semantics=("arbitrary", "arbitrary"),
            vmem_limit_bytes=VMEM_LIMIT),
        name="retention",
    )(lg, z3, z3, z3, z3, zm, zm)


def _fox_kernel(q_ref, k_ref, v_ref, g_ref, c_ref, km_ref, vm_ref, cm_ref, o_ref,
                vt_scr, vtm_scr, cb_scr, cbm_scr, qt_scr, acc_scr, m_scr, s_scr, pm_scr,
                sm_scr, pmm_scr, *, n_tiles, n_pairs, seq):
    half = FOX_HEAD_DIM
    n_heads = 2 * n_pairs
    sub = lax.broadcasted_iota(jnp.int32, (LANES, 1), 0)
    sub16 = lax.broadcasted_iota(jnp.int32, (16, 1), 0)

    def bias_lanes(rows):
        stacked = jnp.zeros((16, LANES), _F32)
        for h, row in enumerate(rows):
            rest = row
            for piece in range(3):
                part = rest.astype(_BF16).astype(_F32)
                stacked = jnp.where(sub16 == 3 * h + piece, part, stacked)
                rest = rest - part
        padded = jnp.concatenate([stacked, jnp.zeros((LANES - 16, LANES), _F32)], axis=0)
        return padded.T.astype(_BF16)

    assert 3 * n_heads <= 16
    for cb in range(seq // LANES):
        ks = slice(cb * LANES, (cb + 1) * LANES)
        cb_scr[ks, :] = bias_lanes(
            [c_ref[0, h // 2, h % 2:h % 2 + 1, ks] * -LOG2E for h in range(n_heads)])
    cms = [cm_ref[h // 2, h % 2:h % 2 + 1, :] for h in range(n_heads)]
    cbm_scr[...] = bias_lanes(
        [(cm[:, LANES - 1:LANES] - cm) * LOG2E for cm in cms])[0:N_META, :]
    for h in range(n_heads):
        ones_rows = jnp.where((sub >= 3 * h) & (sub < 3 * h + 3), 1.0, 0.0)
        for buf in range(2):
            qt_scr[buf, h, LANES:2 * LANES, :] = jnp.broadcast_to(
                ones_rows, (LANES, FOX_TQ)).astype(_BF16)
    for p in range(n_pairs):
        lanes = slice(p * LANES, (p + 1) * LANES)
        for jb in range(n_tiles):
            rows = slice(jb * FOX_TK, (jb + 1) * FOX_TK)
            vt = v_ref[0, rows, lanes].astype(_F32).T.astype(_BF16)
            for hh in range(2):
                h = 2 * p + hh
                vt_scr[h, jb, 0:half, :] = vt[hh * half:(hh + 1) * half, :]
                vt_scr[h, jb, half:FOX_ACC_ROWS, :] = jnp.ones(
                    (FOX_ACC_ROWS - half, FOX_TK), _BF16)
        vtm = vm_ref[:, lanes].astype(_F32).T.astype(_BF16)
        for hh in range(2):
            h = 2 * p + hh
            vtm_scr[h, 0:half, :] = vtm[hh * half:(hh + 1) * half, :]
            vtm_scr[h, half:FOX_ACC_ROWS, :] = jnp.ones((FOX_ACC_ROWS - half, LANES), _BF16)

    ri = lax.broadcasted_iota(jnp.int32, (FOX_TK, FOX_TQ), 0)
    ci = lax.broadcasted_iota(jnp.int32, (FOX_TK, FOX_TQ), 1)
    causal = ri <= ci

    def group_max(st):
        parts = [st[r:r + 8, :] for r in range(0, st.shape[0], 8)]
        while len(parts) > 1:
            nxt = [jnp.maximum(parts[a], parts[a + 1]) for a in range(0, len(parts) - 1, 2)]
            if len(parts) % 2:
                nxt.append(parts[-1])
            parts = nxt
        return parts[0]

    def update(h, tile, first, st, pm, vt):
        buf = tile % 2
        m_blk = jnp.max(pm, axis=0, keepdims=True)
        if first:
            m_new = m_blk
        else:
            m_old = m_scr[buf, h]
            m_new = jnp.maximum(m_old, m_blk)
            alpha = jnp.exp2(m_old - m_new)
        pt = jnp.exp2(st - m_new).astype(_BF16)
        if pt.shape[0] < LANES:
            pt = jnp.concatenate(
                [pt, jnp.zeros((LANES - pt.shape[0], FOX_TQ), _BF16)], axis=0)
        pv = _dot(vt, pt)
        acc_scr[buf, h] = pv if first else acc_scr[buf, h] * alpha + pv
        m_scr[buf, h] = m_new

    def prepare_queries(i):
        rows = slice(i * FOX_TQ, (i + 1) * FOX_TQ)
        for p in range(n_pairs):
            qt = q_ref[0, rows, p * LANES:(p + 1) * LANES].astype(_F32).T
            qt_scr[i % 2, 2 * p, 0:LANES, :] = jnp.where(sub < half, qt, 0.0).astype(_BF16)
            qt_scr[i % 2, 2 * p + 1, 0:LANES, :] = jnp.where(sub < half, 0.0, qt).astype(_BF16)

    def scores(item):
        kind, i, j, slot = item
        if kind == "meta":
            for p in range(n_pairs):
                km = jnp.concatenate(
                    [km_ref[0:N_META, p * LANES:(p + 1) * LANES], cbm_scr[...]], axis=1)
                for hh in range(2):
                    h = 2 * p + hh
                    st = _dot(km, qt_scr[i % 2, h])
                    sm_scr[h] = st
                    pmm_scr[h] = group_max(st)
            return
        krows = slice(j * FOX_TK, (j + 1) * FOX_TK)
        for p in range(n_pairs):
            kblk = jnp.concatenate(
                [k_ref[0, krows, p * LANES:(p + 1) * LANES], cb_scr[krows, :]], axis=1)
            for hh in range(2):
                h = 2 * p + hh
                st = _dot(kblk, qt_scr[i % 2, h])
                if i == j:
                    st = jnp.where(causal, st, NEG_INF)
                s_scr[slot, h] = st
                pm_scr[slot, h] = group_max(st)

    def consume(item):
        kind, i, j, slot = item
        for h in range(n_heads):
            if kind == "meta":
                update(h, i, True, sm_scr[h], pmm_scr[h], vtm_scr[h])
            else:
                update(h, i, False, s_scr[slot, h], pm_scr[slot, h], vt_scr[h, j])

    def finalize(i):
        rows = slice(i * FOX_TQ, (i + 1) * FOX_TQ)
        for p in range(n_pairs):
            lanes = slice(p * LANES, (p + 1) * LANES)
            outs = []
            for hh in range(2):
                acc = acc_scr[i % 2, 2 * p + hh]
                outs.append(acc[0:half, :] / acc[half:half + 1, :])
            y = jnp.concatenate(outs, axis=0).T
            gate = g_ref[0, rows, lanes].astype(_F32)
            o_ref[0, rows, lanes] = (y * _silu(gate)).astype(_BF16)

    items = []
    for i in range(n_tiles):
        items.append(("meta", i, None, None))
        for j in [i] + list(range(i)):
            items.append(("blk", i, j, sum(it[0] == "blk" for it in items) % 2))
    prepare_queries(0)
    scores(items[0])
    for t, item in enumerate(items):
        if t + 1 < len(items):
            nxt = items[t + 1]
            if nxt[0] == "meta":
                prepare_queries(nxt[1])
            scores(nxt)
        consume(item)
        if t + 1 == len(items) or items[t + 1][1] != item[1]:
            finalize(item[1])


def _fox(z3, c4, zm, cm3):
    b, s, _ = z3.shape
    n_pairs = FOX_PAIRS
    n_groups = FOX_HEADS // 2 // n_pairs
    n_tiles = s // FOX_TQ
    w = n_pairs * LANES
    kern = functools.partial(_fox_kernel, n_tiles=n_tiles, n_pairs=n_pairs, seq=s)
    return pl.pallas_call(
        kern,
        grid=(b, n_groups),
        in_specs=[
            pl.BlockSpec((1, s, w), lambda i, p: (i, 0, COL_FQ // n_pairs + p)),
            pl.BlockSpec((1, s, w), lambda i, p: (i, 0, COL_FK // n_pairs + p)),
            pl.BlockSpec((1, s, w), lambda i, p: (i, 0, COL_FV // n_pairs + p)),
            pl.BlockSpec((1, s, w), lambda i, p: (i, 0, COL_FG // n_pairs + p)),
            pl.BlockSpec((1, n_pairs, 2, s), lambda i, p: (i, p, 0, 0)),
            pl.BlockSpec((CHUNK, w), lambda i, p: (0, COL_FK // n_pairs + p)),
            pl.BlockSpec((CHUNK, w), lambda i, p: (0, COL_FV // n_pairs + p)),
            pl.BlockSpec((n_pairs, 2, LANES), lambda i, p: (p, 0, 0)),
        ],
        out_specs=pl.BlockSpec((1, s, w), lambda i, p: (i, 0, p)),
        out_shape=jax.ShapeDtypeStruct((b, s, FOX_HEADS * FOX_HEAD_DIM), _BF16),
        scratch_shapes=[
            pltpu.VMEM((2 * n_pairs, n_tiles, FOX_ACC_ROWS, FOX_TK), _BF16),
            pltpu.VMEM((2 * n_pairs, FOX_ACC_ROWS, LANES), _BF16),
            pltpu.VMEM((s, LANES), _BF16),
            pltpu.VMEM((N_META, LANES), _BF16),
            pltpu.VMEM((2, 2 * n_pairs, 2 * LANES, FOX_TQ), _BF16),
            pltpu.VMEM((2, 2 * n_pairs, FOX_ACC_ROWS, FOX_TQ), _F32),
            pltpu.VMEM((2, 2 * n_pairs, 1, FOX_TQ), _F32),
            pltpu.VMEM((2, 2 * n_pairs, FOX_TK, FOX_TQ), _F32),
            pltpu.VMEM((2, 2 * n_pairs, 8, FOX_TQ), _F32),
            pltpu.VMEM((2 * n_pairs, N_META, FOX_TQ), _F32),
            pltpu.VMEM((2 * n_pairs, 8, FOX_TQ), _F32),
        ],
        compiler_params=pltpu.CompilerParams(
            dimension_semantics=("arbitrary", "arbitrary"),
            vmem_limit_bytes=VMEM_LIMIT),
        name="fox",
    )(z3, z3, z3, z3, c4, zm, zm, cm3)


def _outproj_kernel(yr_ref, yf_ref, x_ref, w_ref, g_ref, o_ref):
    half = w_ref.shape[0] // 2
    hres = x_ref[...] + _dot(yr_ref[...], w_ref[:half, :]) + _dot(yf_ref[...], w_ref[half:, :])
    ms = jnp.mean(hres * hres, axis=-1, keepdims=True)
    o_ref[...] = hres * lax.rsqrt(ms + EPS) * g_ref[...]


def _outproj(yr, yf, x2d, w_out, g, *, tm):
    rows = x2d.shape[0]
    return pl.pallas_call(
        _outproj_kernel,
        grid=(rows // tm,),
        in_specs=[
            pl.BlockSpec((tm, yr.shape[1]), lambda i: (i, 0)),
            pl.BlockSpec((tm, yf.shape[1]), lambda i: (i, 0)),
            pl.BlockSpec((tm, D_MODEL), lambda i: (i, 0)),
            pl.BlockSpec(w_out.shape, lambda i: (0, 0)),
            pl.BlockSpec((1, D_MODEL), lambda i: (0, 0)),
        ],
        out_specs=pl.BlockSpec((tm, D_MODEL), lambda i: (i, 0)),
        out_shape=jax.ShapeDtypeStruct((rows, D_MODEL), _F32),
        compiler_params=pltpu.CompilerParams(
            dimension_semantics=("arbitrary",),
            vmem_limit_bytes=VMEM_LIMIT),
        name="outproj",
    )(yr, yf, x2d, w_out, g)


def _rope_tables(pos):
    inv = ROPE_BASE ** (-jnp.arange(0, RET_DK, 2, dtype=_F32) / RET_DK)
    ang = pos[:, None] * inv[None, :]
    cos, sin = jnp.cos(ang), jnp.sin(ang)
    return jnp.concatenate([cos, cos], axis=-1), jnp.concatenate([-sin, sin], axis=-1)


def kernel(x, meta_tokens, norm_g, w_in, b_f, w_out, final_g):
    b, s, d = x.shape
    assert norm_g.shape[0] == 1 and d == D_MODEL and s % FOX_TQ == 0
    x2d = x.reshape(b * s, d)
    col_scale = jnp.ones((1, D_MAIN), _F32)
    col_scale = col_scale.at[:, COL_RK * LANES:COL_RV * LANES].set(RET_DK ** -0.5)
    col_scale = col_scale.at[:, COL_FQ * LANES:COL_FK * LANES].set(FOX_HEAD_DIM ** -0.5 * LOG2E)
    w_ff_t = w_in[0, :, D_MAIN:].T.astype(_BF16)
    g = norm_g[0].reshape(1, d)
    bf = b_f[0].reshape(FOX_HEADS, 1)
    meta_pad = jnp.pad(meta_tokens.astype(_F32), ((0, CHUNK - N_META), (0, 0)))
    cos2, sin2 = _rope_tables(jnp.arange(s, dtype=_F32) + float(N_META))
    cosm, sinm = _rope_tables(jnp.arange(CHUNK, dtype=_F32))

    z, c = _inproj(x2d, g, w_in, col_scale, w_ff_t, bf, cos2, sin2, tm=s, tn=512, n_valid=s)
    zm, cm = _inproj(meta_pad, g, w_in, col_scale, w_ff_t, bf, cosm, sinm,
                     tm=CHUNK, tn=512, n_valid=N_META)

    z3 = z.reshape(b, s, D_MAIN)
    lg = jnp.log1p(-jnp.exp2(-5.0 - jnp.arange(RET_HEADS, dtype=_F32)))
    y_r = _retention(lg, z3, zm)

    c4 = c.reshape(b, FOX_HEADS // 2, 2, s)
    cm3 = cm.reshape(FOX_HEADS // 2, 2, CHUNK)
    y_f = _fox(z3, c4, zm, cm3)

    out = _outproj(y_r.reshape(b * s, -1), y_f.reshape(b * s, -1), x2d,
                   w_out[0].astype(_BF16), final_g.reshape(1, d), tm=512)
    return out.reshape(b, s, d)
```

```python
import functools

import jax
import jax.numpy as jnp
from jax import lax
from jax.experimental import pallas as pl
from jax.experimental.pallas import tpu as pltpu

D_MODEL = 1024
N_META = 16
CHUNK = 128
RET_HEADS = 4
RET_DK = 128
RET_DV = 256
FOX_HEADS = 16
FOX_HEAD_DIM = 64
ROPE_BASE = 10000.0
EPS = 1e-6
NEG_INF = -1e30
LOG2E = 1.4426950408889634

LANES = 128
D_MAIN = 7168
COL_RQ, COL_RK, COL_RV, COL_RG = 0, 4, 8, 16
COL_FQ, COL_FK, COL_FV, COL_FG = 24, 32, 40, 48
FOX_TQ = 256
FOX_TK = 256
FOX_PAIRS = 2
FOX_ACC_ROWS = FOX_HEAD_DIM + 16
VMEM_LIMIT = 48 * 1024 * 1024

_F32 = jnp.float32
_BF16 = jnp.bfloat16


def _dot(a, b):
    return jnp.dot(a, b, preferred_element_type=_F32)


def _dot_nt(a, b):
    return lax.dot_general(a, b, (((1,), (1,)), ((), ())), preferred_element_type=_F32)


def _silu(g):
    return g * (1.0 / (1.0 + jnp.exp(-g)))


def _rotary(x, cos2, sin2):
    return x * cos2 + pltpu.roll(x, RET_DK // 2, axis=1) * sin2


def _inproj_kernel(x_ref, g_ref, w_ref, wff_ref, bf_ref, cos_ref, sin_ref,
                   z_ref, c_ref, u_scr, lf_scr, *, tm, tn, sub, n_valid):
    j = pl.program_id(1)
    n_sub = tm // sub

    def normalize(r):
        rows = pl.ds(r * sub, sub)
        xf = x_ref[rows, :]
        ms = jnp.mean(xf * xf, axis=-1, keepdims=True)
        u = (xf * lax.rsqrt(ms + EPS) * g_ref[...]).astype(_BF16)
        u_scr[rows, :] = u
        lf_scr[:, r * sub:(r + 1) * sub] = _dot_nt(wff_ref[...].astype(_BF16), u)

    def project(r, rotate):
        rows = pl.ds(r * sub, sub)
        acc = _dot_nt(u_scr[rows, :], w_ref[...])
        if rotate:
            cos2, sin2 = cos_ref[rows, :], sin_ref[rows, :]
            acc = jnp.concatenate(
                [_rotary(acc[:, a:a + RET_DK], cos2, sin2) for a in range(0, tn, RET_DK)],
                axis=1)
        z_ref[rows, :] = acc.astype(_BF16)

    def forget_cumsum():
        lane = lax.broadcasted_iota(jnp.int32, (FOX_HEADS, LANES), 1)
        carry = jnp.zeros((FOX_HEADS, 1), _F32)
        bf = bf_ref[...]
        for ci in range(tm // LANES):
            v = lf_scr[:, ci * LANES:(ci + 1) * LANES] + bf
            blk = jnp.minimum(v, 0.0) - jnp.log1p(jnp.exp(-jnp.abs(v)))
            if (ci + 1) * LANES > n_valid:
                blk = jnp.where(lane + ci * LANES < n_valid, blk, 0.0)
            sh = 1
            while sh < LANES:
                rolled = pltpu.roll(blk, sh, axis=1)
                blk = blk + jnp.where(lane >= sh, rolled, 0.0)
                sh *= 2
            blk = blk + carry
            c_ref[0, :, ci * LANES:(ci + 1) * LANES] = blk
            carry = blk[:, LANES - 1:LANES]

    rot_tiles = (COL_RV * LANES) // tn

    @pl.when(j == 0)
    def _():
        for r in range(n_sub):
            normalize(r)
            project(r, True)
        forget_cumsum()

    @pl.when((j > 0) & (j < rot_tiles))
    def _():
        for r in range(n_sub):
            project(r, True)

    @pl.when(j >= rot_tiles)
    def _():
        for r in range(n_sub):
            project(r, False)


def _inproj(x2d, g, w_t, w_ff_t, b_f, cos2, sin2, *, tm, tn, n_valid):
    rows = x2d.shape[0]
    sub = min(tm, 512)
    assert (COL_RV * LANES) % tn == 0 and tn % RET_DK == 0
    kern = functools.partial(_inproj_kernel, tm=tm, tn=tn, sub=sub, n_valid=n_valid)
    return pl.pallas_call(
        kern,
        grid=(rows // tm, D_MAIN // tn),
        in_specs=[
            pl.BlockSpec((tm, D_MODEL), lambda i, j: (i, 0)),
            pl.BlockSpec((1, D_MODEL), lambda i, j: (0, 0)),
            pl.BlockSpec((tn, D_MODEL), lambda i, j: (j, 0)),
            pl.BlockSpec((FOX_HEADS, D_MODEL), lambda i, j: (0, 0)),
            pl.BlockSpec((FOX_HEADS, 1), lambda i, j: (0, 0)),
            pl.BlockSpec((tm, RET_DK), lambda i, j: (0, 0)),
            pl.BlockSpec((tm, RET_DK), lambda i, j: (0, 0)),
        ],
        out_specs=[
            pl.BlockSpec((tm, tn), lambda i, j: (i, j)),
            pl.BlockSpec((1, FOX_HEADS, tm), lambda i, j: (i, 0, 0)),
        ],
        out_shape=[
            jax.ShapeDtypeStruct((rows, D_MAIN), _BF16),
            jax.ShapeDtypeStruct((rows // tm, FOX_HEADS, tm), _F32),
        ],
        scratch_shapes=[
            pltpu.VMEM((tm, D_MODEL), _BF16),
            pltpu.VMEM((FOX_HEADS, tm), _F32),
        ],
        compiler_params=pltpu.CompilerParams(
            dimension_semantics=("arbitrary", "arbitrary"),
            vmem_limit_bytes=VMEM_LIMIT),
        name="inproj",
    )(x2d, g, w_t, w_ff_t, b_f, cos2, sin2)


def _retention_kernel(lg_ref, q_ref, k_ref, v_ref, g_ref, km_ref, vm_ref, o_ref, *, n_chunks):
    h = pl.program_id(1)
    lg = lg_ref[h]
    ri = lax.broadcasted_iota(jnp.int32, (CHUNK, CHUNK), 0)
    ci = lax.broadcasted_iota(jnp.int32, (CHUNK, CHUNK), 1)
    diff = (ri - ci).astype(_F32)
    dmask = jnp.where(diff >= 0, jnp.exp(lg * jnp.maximum(diff, 0.0)), 0.0)
    idx = lax.broadcasted_iota(jnp.int32, (CHUNK, 1), 0).astype(_F32)
    xi = jnp.exp(lg * (idx + 1.0))
    zeta = jnp.exp(lg * (CHUNK - 1.0 - idx))
    chunk_decay = jnp.exp(lg * jnp.full((1, 1), float(CHUNK), _F32))

    zeta_m = jnp.exp(lg * (N_META - 1.0 - idx))
    km = km_ref[...].astype(_F32)
    state0 = _dot((km * zeta_m).T.astype(_BF16), vm_ref[...])

    def chunk_products(n):
        rows = slice(n * CHUNK, (n + 1) * CHUNK)
        kb = k_ref[0, rows, :]
        v = v_ref[0, rows, :]
        scores = (_dot_nt(q_ref[0, rows, :], kb) * dmask).astype(_BF16)
        kv = _dot((kb.astype(_F32) * zeta).T.astype(_BF16), v)
        return scores, kv

    def chunk_output(n, scores, state):
        rows = slice(n * CHUNK, (n + 1) * CHUNK)
        out = _dot(scores, v_ref[0, rows, :])
        out = out + _dot(q_ref[0, rows, :], state.astype(_BF16)) * xi
        y = out * lax.rsqrt(jnp.mean(out * out, axis=-1, keepdims=True) + EPS)
        gate = g_ref[0, rows, :].astype(_F32)
        o_ref[0, rows, :] = (y * _silu(gate)).astype(_BF16)

    state = state0
    scores, kv = chunk_products(0)
    for n in range(n_chunks):
        if n + 1 < n_chunks:
            nxt = chunk_products(n + 1)
        chunk_output(n, scores, state)
        state = state * chunk_decay + kv
        if n + 1 < n_chunks:
            scores, kv = nxt


def _retention(lg, z3, zm):
    b, s, _ = z3.shape
    kern = functools.partial(_retention_kernel, n_chunks=s // CHUNK)
    return pl.pallas_call(
        kern,
        grid=(b, RET_HEADS),
        in_specs=[
            pl.BlockSpec(memory_space=pltpu.SMEM),
            pl.BlockSpec((1, s, RET_DK), lambda i, h: (i, 0, COL_RQ + h)),
            pl.BlockSpec((1, s, RET_DK), lambda i, h: (i, 0, COL_RK + h)),
            pl.BlockSpec((1, s, RET_DV), lambda i, h: (i, 0, COL_RV // 2 + h)),
            pl.BlockSpec((1, s, RET_DV), lambda i, h: (i, 0, COL_RG // 2 + h)),
            pl.BlockSpec((CHUNK, RET_DK), lambda i, h: (0, COL_RK + h)),
            pl.BlockSpec((CHUNK, RET_DV), lambda i, h: (0, COL_RV // 2 + h)),
        ],
        out_specs=pl.BlockSpec((1, s, RET_DV), lambda i, h: (i, 0, h)),
        out_shape=jax.ShapeDtypeStruct((b, s, RET_HEADS * RET_DV), _BF16),
        compiler_params=pltpu.CompilerParams(
            dimension_semantics=("arbitrary", "arbitrary"),
            vmem_limit_bytes=VMEM_LIMIT),
        name="retention",
    )(lg, z3, z3, z3, z3, zm, zm)


def _fox_kernel(q_ref, k_ref, v_ref, g_ref, c_ref, km_ref, vm_ref, cm_ref, o_ref,
                vt_scr, vtm_scr, bias_scr, biasm_scr, qt_scr, acc_scr, m_scr, s_scr, pm_scr,
                sm_scr, pmm_scr, *, n_tiles, n_pairs, seq):
    half = FOX_HEAD_DIM
    n_heads = 2 * n_pairs
    sub = lax.broadcasted_iota(jnp.int32, (LANES, 1), 0)

    def column_tile(row):
        return jnp.broadcast_to(row, (LANES, LANES)).T

    for p in range(n_pairs):
        lanes = slice(p * LANES, (p + 1) * LANES)
        for jb in range(n_tiles):
            rows = slice(jb * FOX_TK, (jb + 1) * FOX_TK)
            vt = v_ref[0, rows, lanes].astype(_F32).T.astype(_BF16)
            for hh in range(2):
                h = 2 * p + hh
                vt_scr[h, jb, 0:half, :] = vt[hh * half:(hh + 1) * half, :]
                vt_scr[h, jb, half:FOX_ACC_ROWS, :] = jnp.ones(
                    (FOX_ACC_ROWS - half, FOX_TK), _BF16)
        vtm = vm_ref[:, lanes].astype(_F32).T.astype(_BF16)
        for hh in range(2):
            h = 2 * p + hh
            vtm_scr[h, 0:half, :] = vtm[hh * half:(hh + 1) * half, :]
            vtm_scr[h, half:FOX_ACC_ROWS, :] = jnp.ones((FOX_ACC_ROWS - half, LANES), _BF16)
            for cb in range(seq // LANES):
                crow = c_ref[0, p, hh:hh + 1, cb * LANES:(cb + 1) * LANES]
                bias_scr[h, cb * LANES:(cb + 1) * LANES, :] = column_tile(crow * -LOG2E)
            cm = cm_ref[p, hh:hh + 1, :]
            biasm_scr[h] = column_tile((cm[:, LANES - 1:LANES] - cm) * LOG2E)[0:N_META, :]

    ri = lax.broadcasted_iota(jnp.int32, (FOX_TK, FOX_TQ), 0)
    ci = lax.broadcasted_iota(jnp.int32, (FOX_TK, FOX_TQ), 1)
    causal = ri <= ci

    def group_max(st):
        parts = [st[r:r + 8, :] for r in range(0, st.shape[0], 8)]
        while len(parts) > 1:
            nxt = [jnp.maximum(parts[a], parts[a + 1]) for a in range(0, len(parts) - 1, 2)]
            if len(parts) % 2:
                nxt.append(parts[-1])
            parts = nxt
        return parts[0]

    def update(h, tile, first, st, pm, vt):
        buf = tile % 2
        m_blk = jnp.max(pm, axis=0, keepdims=True)
        if first:
            m_new = m_blk
        else:
            m_old = m_scr[buf, h]
            m_new = jnp.maximum(m_old, m_blk)
            alpha = jnp.exp2(m_old - m_new)
        pt = jnp.exp2(st - m_new).astype(_BF16)
        if pt.shape[0] < LANES:
            pt = jnp.concatenate(
                [pt, jnp.zeros((LANES - pt.shape[0], FOX_TQ), _BF16)], axis=0)
        pv = _dot(vt, pt)
        acc_scr[buf, h] = pv if first else acc_scr[buf, h] * alpha + pv
        m_scr[buf, h] = m_new

    def prepare_queries(i):
        rows = slice(i * FOX_TQ, (i + 1) * FOX_TQ)
        for p in range(n_pairs):
            qt = q_ref[0, rows, p * LANES:(p + 1) * LANES].astype(_F32).T
            qt_scr[i % 2, 2 * p] = jnp.where(sub < half, qt, 0.0).astype(_BF16)
            qt_scr[i % 2, 2 * p + 1] = jnp.where(sub < half, 0.0, qt).astype(_BF16)

    def scores(item):
        kind, i, j, slot = item
        if kind == "meta":
            for p in range(n_pairs):
                km = km_ref[0:N_META, p * LANES:(p + 1) * LANES]
                for hh in range(2):
                    h = 2 * p + hh
                    bm = biasm_scr[h]
                    st = _dot(km, qt_scr[i % 2, h]) + jnp.concatenate([bm, bm], axis=1)
                    sm_scr[h] = st
                    pmm_scr[h] = group_max(st)
            return
        krows = slice(j * FOX_TK, (j + 1) * FOX_TK)
        for p in range(n_pairs):
            kblk = k_ref[0, krows, p * LANES:(p + 1) * LANES]
            for hh in range(2):
                h = 2 * p + hh
                bb = bias_scr[h, krows, :]
                st = _dot(kblk, qt_scr[i % 2, h]) + jnp.concatenate([bb, bb], axis=1)
                if i == j:
                    st = jnp.where(causal, st, NEG_INF)
                s_scr[slot, h] = st
                pm_scr[slot, h] = group_max(st)

    def consume(item):
        kind, i, j, slot = item
        for h in range(n_heads):
            if kind == "meta":
                update(h, i, True, sm_scr[h], pmm_scr[h], vtm_scr[h])
            else:
                update(h, i, False, s_scr[slot, h], pm_scr[slot, h], vt_scr[h, j])

    def finalize(i):
        rows = slice(i * FOX_TQ, (i + 1) * FOX_TQ)
        for p in range(n_pairs):
            lanes = slice(p * LANES, (p + 1) * LANES)
            outs = []
            for hh in range(2):
                acc = acc_scr[i % 2, 2 * p + hh]
                outs.append(acc[0:half, :] / acc[half:half + 1, :])
            y = jnp.concatenate(outs, axis=0).T
            gate = g_ref[0, rows, lanes].astype(_F32)
            o_ref[0, rows, lanes] = (y * _silu(gate)).astype(_BF16)

    items = []
    for i in range(n_tiles):
        items.append(("meta", i, None, None))
        for j in [i] + list(range(i)):
            items.append(("blk", i, j, sum(it[0] == "blk" for it in items) % 2))
    prepare_queries(0)
    scores(items[0])
    for t, item in enumerate(items):
        if t + 1 < len(items):
            nxt = items[t + 1]
            if nxt[0] == "meta":
                prepare_queries(nxt[1])
            scores(nxt)
        consume(item)
        if t + 1 == len(items) or items[t + 1][1] != item[1]:
            finalize(item[1])


def _fox(z3, c4, zm, cm3):
    b, s, _ = z3.shape
    n_pairs = FOX_PAIRS
    n_groups = FOX_HEADS // 2 // n_pairs
    n_tiles = s // FOX_TQ
    w = n_pairs * LANES
    kern = functools.partial(_fox_kernel, n_tiles=n_tiles, n_pairs=n_pairs, seq=s)
    return pl.pallas_call(
        kern,
        grid=(b, n_groups),
        in_specs=[
            pl.BlockSpec((1, s, w), lambda i, p: (i, 0, COL_FQ // n_pairs + p)),
            pl.BlockSpec((1, s, w), lambda i, p: (i, 0, COL_FK // n_pairs + p)),
            pl.BlockSpec((1, s, w), lambda i, p: (i, 0, COL_FV // n_pairs + p)),
            pl.BlockSpec((1, s, w), lambda i, p: (i, 0, COL_FG // n_pairs + p)),
            pl.BlockSpec((1, n_pairs, 2, s), lambda i, p: (i, p, 0, 0)),
            pl.BlockSpec((CHUNK, w), lambda i, p: (0, COL_FK // n_pairs + p)),
            pl.BlockSpec((CHUNK, w), lambda i, p: (0, COL_FV // n_pairs + p)),
            pl.BlockSpec((n_pairs, 2, LANES), lambda i, p: (p, 0, 0)),
        ],
        out_specs=pl.BlockSpec((1, s, w), lambda i, p: (i, 0, p)),
        out_shape=jax.ShapeDtypeStruct((b, s, FOX_HEADS * FOX_HEAD_DIM), _BF16),
        scratch_shapes=[
            pltpu.VMEM((2 * n_pairs, n_tiles, FOX_ACC_ROWS, FOX_TK), _BF16),
            pltpu.VMEM((2 * n_pairs, FOX_ACC_ROWS, LANES), _BF16),
            pltpu.VMEM((2 * n_pairs, s, LANES), _F32),
            pltpu.VMEM((2 * n_pairs, N_META, LANES), _F32),
            pltpu.VMEM((2, 2 * n_pairs, LANES, FOX_TQ), _BF16),
            pltpu.VMEM((2, 2 * n_pairs, FOX_ACC_ROWS, FOX_TQ), _F32),
            pltpu.VMEM((2, 2 * n_pairs, 1, FOX_TQ), _F32),
            pltpu.VMEM((2, 2 * n_pairs, FOX_TK, FOX_TQ), _F32),
            pltpu.VMEM((2, 2 * n_pairs, 8, FOX_TQ), _F32),
            pltpu.VMEM((2 * n_pairs, N_META, FOX_TQ), _F32),
            pltpu.VMEM((2 * n_pairs, 8, FOX_TQ), _F32),
        ],
        compiler_params=pltpu.CompilerParams(
            dimension_semantics=("arbitrary", "arbitrary"),
            vmem_limit_bytes=VMEM_LIMIT),
        name="fox",
    )(z3, z3, z3, z3, c4, zm, zm, cm3)


def _outproj_kernel(yr_ref, yf_ref, x_ref, w_ref, g_ref, o_ref):
    half = w_ref.shape[0] // 2
    hres = x_ref[...] + _dot(yr_ref[...], w_ref[:half, :]) + _dot(yf_ref[...], w_ref[half:, :])
    ms = jnp.mean(hres * hres, axis=-1, keepdims=True)
    o_ref[...] = hres * lax.rsqrt(ms + EPS) * g_ref[...]


def _outproj(yr, yf, x2d, w_out, g, *, tm):
    rows = x2d.shape[0]
    return pl.pallas_call(
        _outproj_kernel,
        grid=(rows // tm,),
        in_specs=[
            pl.BlockSpec((tm, yr.shape[1]), lambda i: (i, 0)),
            pl.BlockSpec((tm, yf.shape[1]), lambda i: (i, 0)),
            pl.BlockSpec((tm, D_MODEL), lambda i: (i, 0)),
            pl.BlockSpec(w_out.shape, lambda i: (0, 0)),
            pl.BlockSpec((1, D_MODEL), lambda i: (0, 0)),
        ],
        out_specs=pl.BlockSpec((tm, D_MODEL), lambda i: (i, 0)),
        out_shape=jax.ShapeDtypeStruct((rows, D_MODEL), _F32),
        compiler_params=pltpu.CompilerParams(
            dimension_semantics=("arbitrary",),
            vmem_limit_bytes=VMEM_LIMIT),
        name="outproj",
    )(yr, yf, x2d, w_out, g)


def _rope_tables(pos):
    inv = ROPE_BASE ** (-jnp.arange(0, RET_DK, 2, dtype=_F32) / RET_DK)
    ang = pos[:, None] * inv[None, :]
    cos, sin = jnp.cos(ang), jnp.sin(ang)
    return jnp.concatenate([cos, cos], axis=-1), jnp.concatenate([-sin, sin], axis=-1)


def kernel(x, meta_tokens, norm_g, w_in, b_f, w_out, final_g):
    b, s, d = x.shape
    assert norm_g.shape[0] == 1 and d == D_MODEL and s % FOX_TQ == 0
    x2d = x.reshape(b * s, d)
    col_scale = jnp.ones((D_MAIN, 1), _F32)
    col_scale = col_scale.at[COL_RK * LANES:COL_RV * LANES].set(RET_DK ** -0.5)
    col_scale = col_scale.at[COL_FQ * LANES:COL_FK * LANES].set(FOX_HEAD_DIM ** -0.5 * LOG2E)
    w_all_t = jnp.swapaxes(w_in, 1, 2)[0]
    w_t = (w_all_t[:D_MAIN, :] * col_scale).astype(_BF16)
    w_ff_t = w_all_t[D_MAIN:, :]
    g = norm_g[0].reshape(1, d)
    bf = b_f[0].reshape(FOX_HEADS, 1)
    meta_pad = jnp.pad(meta_tokens.astype(_F32), ((0, CHUNK - N_META), (0, 0)))
    cos2, sin2 = _rope_tables(jnp.arange(s, dtype=_F32) + float(N_META))
    cosm, sinm = _rope_tables(jnp.arange(CHUNK, dtype=_F32))

    z, c = _inproj(x2d, g, w_t, w_ff_t, bf, cos2, sin2, tm=s, tn=512, n_valid=s)
    zm, cm = _inproj(meta_pad, g, w_t, w_ff_t, bf, cosm, sinm,
                     tm=CHUNK, tn=512, n_valid=N_META)

    z3 = z.reshape(b, s, D_MAIN)
    lg = jnp.log1p(-jnp.exp2(-5.0 - jnp.arange(RET_HEADS, dtype=_F32)))
    y_r = _retention(lg, z3, zm)

    c4 = c.reshape(b, FOX_HEADS // 2, 2, s)
    cm3 = cm.reshape(FOX_HEADS // 2, 2, CHUNK)
    y_f = _fox(z3, c4, zm, cm3)

    out = _outproj(y_r.reshape(b * s, -1), y_f.reshape(b * s, -1), x2d,
                   w_out[0].astype(_BF16), final_g.reshape(1, d), tm=512)
    return out.reshape(b, s, d)
```

```python
import functools

import jax
import jax.numpy as jnp
from jax import lax
from jax.experimental import pallas as pl
from jax.experimental.pallas import tpu as pltpu

D_MODEL = 1024
N_META = 16
CHUNK = 128
RET_HEADS = 4
RET_DK = 128
RET_DV = 256
FOX_HEADS = 16
FOX_HEAD_DIM = 64
ROPE_BASE = 10000.0
EPS = 1e-6
NEG_INF = -1e30
LOG2E = 1.4426950408889634

LANES = 128
D_MAIN = 7168
COL_RQ, COL_RK, COL_RV, COL_RG = 0, 4, 8, 16
COL_FQ, COL_FK, COL_FV, COL_FG = 24, 32, 40, 48
FOX_TQ = 256
FOX_TK = 256
FOX_PAIRS = 2
FOX_ACC_ROWS = FOX_HEAD_DIM + 16
FOX_BIAS_PIECES = 3
VMEM_LIMIT = 48 * 1024 * 1024

_F32 = jnp.float32
_BF16 = jnp.bfloat16


def _dot(a, b):
    return jnp.dot(a, b, preferred_element_type=_F32)


def _dot_nt(a, b):
    return lax.dot_general(a, b, (((1,), (1,)), ((), ())), preferred_element_type=_F32)


def _silu(g):
    return g * (1.0 / (1.0 + jnp.exp(-g)))


def _rotary(x, cos2, sin2):
    return x * cos2 + pltpu.roll(x, RET_DK // 2, axis=1) * sin2


def _inproj_kernel(x_ref, g_ref, w_ref, wff_ref, bf_ref, cos_ref, sin_ref,
                   z_ref, c_ref, u_scr, lf_scr, *, tm, tn, sub, n_valid):
    j = pl.program_id(1)
    n_sub = tm // sub

    def normalize(r):
        rows = pl.ds(r * sub, sub)
        xf = x_ref[rows, :]
        ms = jnp.mean(xf * xf, axis=-1, keepdims=True)
        u = (xf * lax.rsqrt(ms + EPS) * g_ref[...]).astype(_BF16)
        u_scr[rows, :] = u
        lf_scr[:, r * sub:(r + 1) * sub] = _dot_nt(wff_ref[...].astype(_BF16), u)

    def project(r, rotate):
        rows = pl.ds(r * sub, sub)
        acc = _dot_nt(u_scr[rows, :], w_ref[...])
        if rotate:
            cos2, sin2 = cos_ref[rows, :], sin_ref[rows, :]
            acc = jnp.concatenate(
                [_rotary(acc[:, a:a + RET_DK], cos2, sin2) for a in range(0, tn, RET_DK)],
                axis=1)
        z_ref[rows, :] = acc.astype(_BF16)

    def forget_cumsum():
        lane = lax.broadcasted_iota(jnp.int32, (FOX_HEADS, LANES), 1)
        carry = jnp.zeros((FOX_HEADS, 1), _F32)
        bf = bf_ref[...]
        for ci in range(tm // LANES):
            v = lf_scr[:, ci * LANES:(ci + 1) * LANES] + bf
            blk = jnp.minimum(v, 0.0) - jnp.log1p(jnp.exp(-jnp.abs(v)))
            if (ci + 1) * LANES > n_valid:
                blk = jnp.where(lane + ci * LANES < n_valid, blk, 0.0)
            sh = 1
            while sh < LANES:
                rolled = pltpu.roll(blk, sh, axis=1)
                blk = blk + jnp.where(lane >= sh, rolled, 0.0)
                sh *= 2
            blk = blk + carry
            c_ref[0, :, ci * LANES:(ci + 1) * LANES] = blk
            carry = blk[:, LANES - 1:LANES]

    rot_tiles = (COL_RV * LANES) // tn

    @pl.when(j == 0)
    def _():
        for r in range(n_sub):
            normalize(r)
            project(r, True)
        forget_cumsum()

    @pl.when((j > 0) & (j < rot_tiles))
    def _():
        for r in range(n_sub):
            project(r, True)

    @pl.when(j >= rot_tiles)
    def _():
        for r in range(n_sub):
            project(r, False)


def _inproj(x2d, g, w_t, w_ff_t, b_f, cos2, sin2, *, tm, tn, n_valid):
    rows = x2d.shape[0]
    sub = min(tm, 512)
    assert (COL_RV * LANES) % tn == 0 and tn % RET_DK == 0
    kern = functools.partial(_inproj_kernel, tm=tm, tn=tn, sub=sub, n_valid=n_valid)
    return pl.pallas_call(
        kern,
        grid=(rows // tm, D_MAIN // tn),
        in_specs=[
            pl.BlockSpec((tm, D_MODEL), lambda i, j: (i, 0)),
            pl.BlockSpec((1, D_MODEL), lambda i, j: (0, 0)),
            pl.BlockSpec((tn, D_MODEL), lambda i, j: (j, 0)),
            pl.BlockSpec((FOX_HEADS, D_MODEL), lambda i, j: (0, 0)),
            pl.BlockSpec((FOX_HEADS, 1), lambda i, j: (0, 0)),
            pl.BlockSpec((tm, RET_DK), lambda i, j: (0, 0)),
            pl.BlockSpec((tm, RET_DK), lambda i, j: (0, 0)),
        ],
        out_specs=[
            pl.BlockSpec((tm, tn), lambda i, j: (i, j)),
            pl.BlockSpec((1, FOX_HEADS, tm), lambda i, j: (i, 0, 0)),
        ],
        out_shape=[
            jax.ShapeDtypeStruct((rows, D_MAIN), _BF16),
            jax.ShapeDtypeStruct((rows // tm, FOX_HEADS, tm), _F32),
        ],
        scratch_shapes=[
            pltpu.VMEM((tm, D_MODEL), _BF16),
            pltpu.VMEM((FOX_HEADS, tm), _F32),
        ],
        compiler_params=pltpu.CompilerParams(
            dimension_semantics=("arbitrary", "arbitrary"),
            vmem_limit_bytes=VMEM_LIMIT),
        name="inproj",
    )(x2d, g, w_t, w_ff_t, b_f, cos2, sin2)


def _retention_kernel(lg_ref, q_ref, k_ref, v_ref, g_ref, km_ref, vm_ref, o_ref, *, n_chunks):
    h = pl.program_id(1)
    lg = lg_ref[h]
    ri = lax.broadcasted_iota(jnp.int32, (CHUNK, CHUNK), 0)
    ci = lax.broadcasted_iota(jnp.int32, (CHUNK, CHUNK), 1)
    diff = (ri - ci).astype(_F32)
    dmask = jnp.where(diff >= 0, jnp.exp(lg * jnp.maximum(diff, 0.0)), 0.0)
    idx = lax.broadcasted_iota(jnp.int32, (CHUNK, 1), 0).astype(_F32)
    xi = jnp.exp(lg * (idx + 1.0))
    zeta = jnp.exp(lg * (CHUNK - 1.0 - idx))
    chunk_decay = jnp.exp(lg * jnp.full((1, 1), float(CHUNK), _F32))

    zeta_m = jnp.exp(lg * (N_META - 1.0 - idx))
    km = km_ref[...].astype(_F32)
    state0 = _dot((km * zeta_m).T.astype(_BF16), vm_ref[...])

    def chunk_products(n):
        rows = slice(n * CHUNK, (n + 1) * CHUNK)
        kb = k_ref[0, rows, :]
        v = v_ref[0, rows, :]
        scores = (_dot_nt(q_ref[0, rows, :], kb) * dmask).astype(_BF16)
        kv = _dot((kb.astype(_F32) * zeta).T.astype(_BF16), v)
        return scores, kv

    def chunk_output(n, scores, state):
        rows = slice(n * CHUNK, (n + 1) * CHUNK)
        out = _dot(scores, v_ref[0, rows, :])
        out = out + _dot(q_ref[0, rows, :], state.astype(_BF16)) * xi
        y = out * lax.rsqrt(jnp.mean(out * out, axis=-1, keepdims=True) + EPS)
        gate = g_ref[0, rows, :].astype(_F32)
        o_ref[0, rows, :] = (y * _silu(gate)).astype(_BF16)

    state = state0
    scores, kv = chunk_products(0)
    for n in range(n_chunks):
        if n + 1 < n_chunks:
            nxt = chunk_products(n + 1)
        chunk_output(n, scores, state)
        state = state * chunk_decay + kv
        if n + 1 < n_chunks:
            scores, kv = nxt


def _retention(lg, z3, zm):
    b, s, _ = z3.shape
    kern = functools.partial(_retention_kernel, n_chunks=s // CHUNK)
    return pl.pallas_call(
        kern,
        grid=(b, RET_HEADS),
        in_specs=[
            pl.BlockSpec(memory_space=pltpu.SMEM),
            pl.BlockSpec((1, s, RET_DK), lambda i, h: (i, 0, COL_RQ + h)),
            pl.BlockSpec((1, s, RET_DK), lambda i, h: (i, 0, COL_RK + h)),
            pl.BlockSpec((1, s, RET_DV), lambda i, h: (i, 0, COL_RV // 2 + h)),
            pl.BlockSpec((1, s, RET_DV), lambda i, h: (i, 0, COL_RG // 2 + h)),
            pl.BlockSpec((CHUNK, RET_DK), lambda i, h: (0, COL_RK + h)),
            pl.BlockSpec((CHUNK, RET_DV), lambda i, h: (0, COL_RV // 2 + h)),
        ],
        out_specs=pl.BlockSpec((1, s, RET_DV), lambda i, h: (i, 0, h)),
        out_shape=jax.ShapeDtypeStruct((b, s, RET_HEADS * RET_DV), _BF16),
        compiler_params=pltpu.CompilerParams(
            dimension_semantics=("arbitrary", "arbitrary"),
            vmem_limit_bytes=VMEM_LIMIT),
        name="retention",
    )(lg, z3, z3, z3, z3, zm, zm)


def _fox_kernel(q_ref, k_ref, v_ref, g_ref, c_ref, km_ref, vm_ref, cm_ref, o_ref,
                vt_scr, vtm_scr, ka_scr, kam_scr, qt_scr, acc_scr, m_scr, s_scr, pm_scr,
                sm_scr, pmm_scr, *, n_tiles, n_pairs, seq):
    half = FOX_HEAD_DIM
    n_heads = 2 * n_pairs
    sub = lax.broadcasted_iota(jnp.int32, (LANES, 1), 0)
    sub8 = lax.broadcasted_iota(jnp.int32, (8, 1), 0)
    lane = lax.broadcasted_iota(jnp.int32, (1, LANES), 1)

    def pieces8(row):
        out = jnp.zeros((8, LANES), _F32)
        rest = row
        for piece in range(FOX_BIAS_PIECES):
            part = rest.astype(_BF16).astype(_F32)
            out = jnp.where(sub8 == piece, part, out)
            rest = rest - part
        return out

    def augmented_keys(kblk, bias_lo, bias_hi):
        gap = jnp.zeros((half - 8, LANES), _F32)
        tile = jnp.concatenate([pieces8(bias_hi), gap, pieces8(bias_lo), gap], axis=0).T
        tile = tile.astype(_BF16)[0:kblk.shape[0], :]
        return jnp.where(lane < half, kblk, tile), jnp.where(lane < half, tile, kblk)

    for p in range(n_pairs):
        for cb in range(seq // LANES):
            ks = slice(cb * LANES, (cb + 1) * LANES)
            ka_scr[2 * p, ks, :], ka_scr[2 * p + 1, ks, :] = augmented_keys(
                k_ref[0, ks, p * LANES:(p + 1) * LANES],
                c_ref[0, p, 0:1, ks] * -LOG2E, c_ref[0, p, 1:2, ks] * -LOG2E)
        cm0, cm1 = cm_ref[p, 0:1, :], cm_ref[p, 1:2, :]
        kam_scr[2 * p], kam_scr[2 * p + 1] = augmented_keys(
            km_ref[0:N_META, p * LANES:(p + 1) * LANES],
            (cm0[:, LANES - 1:LANES] - cm0) * LOG2E, (cm1[:, LANES - 1:LANES] - cm1) * LOG2E)
    for p in range(n_pairs):
        lanes = slice(p * LANES, (p + 1) * LANES)
        for jb in range(n_tiles):
            rows = slice(jb * FOX_TK, (jb + 1) * FOX_TK)
            vt = v_ref[0, rows, lanes].astype(_F32).T.astype(_BF16)
            for hh in range(2):
                h = 2 * p + hh
                vt_scr[h, jb, 0:half, :] = vt[hh * half:(hh + 1) * half, :]
                vt_scr[h, jb, half:FOX_ACC_ROWS, :] = jnp.ones(
                    (FOX_ACC_ROWS - half, FOX_TK), _BF16)
        vtm = vm_ref[:, lanes].astype(_F32).T.astype(_BF16)
        for hh in range(2):
            h = 2 * p + hh
            vtm_scr[h, 0:half, :] = vtm[hh * half:(hh + 1) * half, :]
            vtm_scr[h, half:FOX_ACC_ROWS, :] = jnp.ones((FOX_ACC_ROWS - half, LANES), _BF16)

    ri = lax.broadcasted_iota(jnp.int32, (FOX_TK, FOX_TQ), 0)
    ci = lax.broadcasted_iota(jnp.int32, (FOX_TK, FOX_TQ), 1)
    causal = ri <= ci

    def group_max(st):
        parts = [st[r:r + 8, :] for r in range(0, st.shape[0], 8)]
        while len(parts) > 1:
            nxt = [jnp.maximum(parts[a], parts[a + 1]) for a in range(0, len(parts) - 1, 2)]
            if len(parts) % 2:
                nxt.append(parts[-1])
            parts = nxt
        return parts[0]

    def update(h, tile, first, st, pm, vt):
        buf = tile % 2
        m_blk = jnp.max(pm, axis=0, keepdims=True)
        if first:
            m_new = m_blk
        else:
            m_old = m_scr[buf, h]
            m_new = jnp.maximum(m_old, m_blk)
            alpha = jnp.exp2(m_old - m_new)
        pt = jnp.exp2(st - m_new).astype(_BF16)
        if pt.shape[0] < LANES:
            pt = jnp.concatenate(
                [pt, jnp.zeros((LANES - pt.shape[0], FOX_TQ), _BF16)], axis=0)
        pv = _dot(vt, pt)
        acc_scr[buf, h] = pv if first else acc_scr[buf, h] * alpha + pv
        m_scr[buf, h] = m_new

    def prepare_queries(i):
        rows = slice(i * FOX_TQ, (i + 1) * FOX_TQ)
        for p in range(n_pairs):
            qt = q_ref[0, rows, p * LANES:(p + 1) * LANES].astype(_F32).T
            ones_hi = jnp.where(sub < half + FOX_BIAS_PIECES, 1.0, 0.0)
            ones_lo = jnp.where(sub < FOX_BIAS_PIECES, 1.0, 0.0)
            qt_scr[i % 2, 2 * p] = jnp.where(sub < half, qt, ones_hi).astype(_BF16)
            qt_scr[i % 2, 2 * p + 1] = jnp.where(sub < half, ones_lo, qt).astype(_BF16)

    def scores(item):
        kind, i, j, slot = item
        if kind == "meta":
            for h in range(n_heads):
                st = _dot(kam_scr[h], qt_scr[i % 2, h])
                sm_scr[h] = st
                pmm_scr[h] = group_max(st)
            return
        krows = slice(j * FOX_TK, (j + 1) * FOX_TK)
        for h in range(n_heads):
            st = _dot(ka_scr[h, krows, :], qt_scr[i % 2, h])
            if i == j:
                st = jnp.where(causal, st, NEG_INF)
            s_scr[slot, h] = st
            pm_scr[slot, h] = group_max(st)

    def consume(item):
        kind, i, j, slot = item
        for h in range(n_heads):
            if kind == "meta":
                update(h, i, True, sm_scr[h], pmm_scr[h], vtm_scr[h])
            else:
                update(h, i, False, s_scr[slot, h], pm_scr[slot, h], vt_scr[h, j])

    def finalize(i):
        rows = slice(i * FOX_TQ, (i + 1) * FOX_TQ)
        for p in range(n_pairs):
            lanes = slice(p * LANES, (p + 1) * LANES)
            outs = []
            for hh in range(2):
                acc = acc_scr[i % 2, 2 * p + hh]
                outs.append(acc[0:half, :] / acc[half:half + 1, :])
            y = jnp.concatenate(outs, axis=0).T
            gate = g_ref[0, rows, lanes].astype(_F32)
            o_ref[0, rows, lanes] = (y * _silu(gate)).astype(_BF16)

    items = []
    for i in range(n_tiles):
        items.append(("meta", i, None, None))
        for j in [i] + list(range(i)):
            items.append(("blk", i, j, sum(it[0] == "blk" for it in items) % 2))
    prepare_queries(0)
    scores(items[0])
    for t, item in enumerate(items):
        if t + 1 < len(items):
            nxt = items[t + 1]
            if nxt[0] == "meta":
                prepare_queries(nxt[1])
            scores(nxt)
        consume(item)
        if t + 1 == len(items) or items[t + 1][1] != item[1]:
            finalize(item[1])


def _fox(z3, c4, zm, cm3):
    b, s, _ = z3.shape
    n_pairs = FOX_PAIRS
    n_groups = FOX_HEADS // 2 // n_pairs
    n_tiles = s // FOX_TQ
    w = n_pairs * LANES
    kern = functools.partial(_fox_kernel, n_tiles=n_tiles, n_pairs=n_pairs, seq=s)
    return pl.pallas_call(
        kern,
        grid=(b, n_groups),
        in_specs=[
            pl.BlockSpec((1, s, w), lambda i, p: (i, 0, COL_FQ // n_pairs + p)),
            pl.BlockSpec((1, s, w), lambda i, p: (i, 0, COL_FK // n_pairs + p)),
            pl.BlockSpec((1, s, w), lambda i, p: (i, 0, COL_FV // n_pairs + p)),
            pl.BlockSpec((1, s, w), lambda i, p: (i, 0, COL_FG // n_pairs + p)),
            pl.BlockSpec((1, n_pairs, 2, s), lambda i, p: (i, p, 0, 0)),
            pl.BlockSpec((CHUNK, w), lambda i, p: (0, COL_FK // n_pairs + p)),
            pl.BlockSpec((CHUNK, w), lambda i, p: (0, COL_FV // n_pairs + p)),
            pl.BlockSpec((n_pairs, 2, LANES), lambda i, p: (p, 0, 0)),
        ],
        out_specs=pl.BlockSpec((1, s, w), lambda i, p: (i, 0, p)),
        out_shape=jax.ShapeDtypeStruct((b, s, FOX_HEADS * FOX_HEAD_DIM), _BF16),
        scratch_shapes=[
            pltpu.VMEM((2 * n_pairs, n_tiles, FOX_ACC_ROWS, FOX_TK), _BF16),
            pltpu.VMEM((2 * n_pairs, FOX_ACC_ROWS, LANES), _BF16),
            pltpu.VMEM((2 * n_pairs, s, LANES), _BF16),
            pltpu.VMEM((2 * n_pairs, N_META, LANES), _BF16),
            pltpu.VMEM((2, 2 * n_pairs, LANES, FOX_TQ), _BF16),
            pltpu.VMEM((2, 2 * n_pairs, FOX_ACC_ROWS, FOX_TQ), _F32),
            pltpu.VMEM((2, 2 * n_pairs, 1, FOX_TQ), _F32),
            pltpu.VMEM((2, 2 * n_pairs, FOX_TK, FOX_TQ), _F32),
            pltpu.VMEM((2, 2 * n_pairs, 8, FOX_TQ), _F32),
            pltpu.VMEM((2 * n_pairs, N_META, FOX_TQ), _F32),
            pltpu.VMEM((2 * n_pairs, 8, FOX_TQ), _F32),
        ],
        compiler_params=pltpu.CompilerParams(
            dimension_semantics=("arbitrary", "arbitrary"),
            vmem_limit_bytes=VMEM_LIMIT),
        name="fox",
    )(z3, z3, z3, z3, c4, zm, zm, cm3)


def _outproj_kernel(yr_ref, yf_ref, x_ref, w_ref, g_ref, o_ref):
    half = w_ref.shape[0] // 2
    hres = x_ref[...] + _dot(yr_ref[...], w_ref[:half, :]) + _dot(yf_ref[...], w_ref[half:, :])
    ms = jnp.mean(hres * hres, axis=-1, keepdims=True)
    o_ref[...] = hres * lax.rsqrt(ms + EPS) * g_ref[...]


def _outproj(yr, yf, x2d, w_out, g, *, tm):
    rows = x2d.shape[0]
    return pl.pallas_call(
        _outproj_kernel,
        grid=(rows // tm,),
        in_specs=[
            pl.BlockSpec((tm, yr.shape[1]), lambda i: (i, 0)),
            pl.BlockSpec((tm, yf.shape[1]), lambda i: (i, 0)),
            pl.BlockSpec((tm, D_MODEL), lambda i: (i, 0)),
            pl.BlockSpec(w_out.shape, lambda i: (0, 0)),
            pl.BlockSpec((1, D_MODEL), lambda i: (0, 0)),
        ],
        out_specs=pl.BlockSpec((tm, D_MODEL), lambda i: (i, 0)),
        out_shape=jax.ShapeDtypeStruct((rows, D_MODEL), _F32),
        compiler_params=pltpu.CompilerParams(
            dimension_semantics=("arbitrary",),
            vmem_limit_bytes=VMEM_LIMIT),
        name="outproj",
    )(yr, yf, x2d, w_out, g)


def _rope_tables(pos):
    inv = ROPE_BASE ** (-jnp.arange(0, RET_DK, 2, dtype=_F32) / RET_DK)
    ang = pos[:, None] * inv[None, :]
    cos, sin = jnp.cos(ang), jnp.sin(ang)
    return jnp.concatenate([cos, cos], axis=-1), jnp.concatenate([-sin, sin], axis=-1)


def kernel(x, meta_tokens, norm_g, w_in, b_f, w_out, final_g):
    b, s, d = x.shape
    assert norm_g.shape[0] == 1 and d == D_MODEL and s % FOX_TQ == 0
    x2d = x.reshape(b * s, d)
    col_scale = jnp.ones((D_MAIN, 1), _F32)
    col_scale = col_scale.at[COL_RK * LANES:COL_RV * LANES].set(RET_DK ** -0.5)
    col_scale = col_scale.at[COL_FQ * LANES:COL_FK * LANES].set(FOX_HEAD_DIM ** -0.5 * LOG2E)
    w_all_t = jnp.swapaxes(w_in, 1, 2)[0]
    w_t = (w_all_t[:D_MAIN, :] * col_scale).astype(_BF16)
    w_ff_t = w_all_t[D_MAIN:, :]
    g = norm_g[0].reshape(1, d)
    bf = b_f[0].reshape(FOX_HEADS, 1)
    meta_pad = jnp.pad(meta_tokens.astype(_F32), ((0, CHUNK - N_META), (0, 0)))
    cos2, sin2 = _rope_tables(jnp.arange(s, dtype=_F32) + float(N_META))
    cosm, sinm = _rope_tables(jnp.arange(CHUNK, dtype=_F32))

    z, c = _inproj(x2d, g, w_t, w_ff_t, bf, cos2, sin2, tm=s, tn=1024, n_valid=s)
    zm, cm = _inproj(meta_pad, g, w_t, w_ff_t, bf, cosm, sinm,
                     tm=CHUNK, tn=512, n_valid=N_META)

    z3 = z.reshape(b, s, D_MAIN)
    lg = jnp.log1p(-jnp.exp2(-5.0 - jnp.arange(RET_HEADS, dtype=_F32)))
    y_r = _retention(lg, z3, zm)

    c4 = c.reshape(b, FOX_HEADS // 2, 2, s)
    cm3 = cm.reshape(FOX_HEADS // 2, 2, CHUNK)
    y_f = _fox(z3, c4, zm, cm3)

    out = _outproj(y_r.reshape(b * s, -1), y_f.reshape(b * s, -1), x2d,
                   w_out[0].astype(_BF16), final_g.reshape(1, d), tm=1024)
    return out.reshape(b, s, d)
```

```python
import functools

import jax
import jax.numpy as jnp
from jax import lax
from jax.experimental import pallas as pl
from jax.experimental.pallas import tpu as pltpu

D_MODEL = 1024
N_META = 16
CHUNK = 128
RET_HEADS = 4
RET_DK = 128
RET_DV = 256
FOX_HEADS = 16
FOX_HEAD_DIM = 64
ROPE_BASE = 10000.0
EPS = 1e-6
NEG_INF = -1e30
LOG2E = 1.4426950408889634

LANES = 128
D_MAIN = 7168
COL_RQ, COL_RK, COL_RV, COL_RG = 0, 4, 8, 16
COL_FQ, COL_FK, COL_FV, COL_FG = 24, 32, 40, 48
FOX_TQ = 256
FOX_TK = 256
FOX_PAIRS = 2
FOX_ACC_ROWS = FOX_HEAD_DIM + 16
FOX_BIAS_PIECES = 3
VMEM_LIMIT = 56 * 1024 * 1024

_F32 = jnp.float32
_BF16 = jnp.bfloat16


def _dot(a, b):
    return jnp.dot(a, b, preferred_element_type=_F32)


def _dot_nt(a, b):
    return lax.dot_general(a, b, (((1,), (1,)), ((), ())), preferred_element_type=_F32)


def _silu(g):
    h = 0.5 * g
    return h + h * jnp.tanh(h)


def _rotary(x, cos2, sin2):
    return x * cos2 + pltpu.roll(x, RET_DK // 2, axis=1) * sin2


def _inproj_kernel(x_ref, g_ref, w_ref, wff_ref, bf_ref, cos_ref, sin_ref,
                   z_ref, c_ref, u_scr, lf_scr, *, tm, tn, sub, n_valid):
    j = pl.program_id(1)
    n_sub = tm // sub

    def normalize(r):
        rows = pl.ds(r * sub, sub)
        xf = x_ref[rows, :]
        ms = jnp.mean(xf * xf, axis=-1, keepdims=True)
        u = (xf * lax.rsqrt(ms + EPS) * g_ref[...]).astype(_BF16)
        u_scr[rows, :] = u
        lf_scr[:, r * sub:(r + 1) * sub] = _dot_nt(wff_ref[...].astype(_BF16), u)

    def project(r, rot_cols):
        rows = pl.ds(r * sub, sub)
        acc = _dot_nt(u_scr[rows, :], w_ref[...])
        if rot_cols:
            cos2, sin2 = cos_ref[rows, :], sin_ref[rows, :]
            slabs = [_rotary(acc[:, a:a + RET_DK], cos2, sin2) for a in range(0, rot_cols, RET_DK)]
            acc = jnp.concatenate(slabs + [acc[:, rot_cols:]], axis=1) if rot_cols < tn else (
                jnp.concatenate(slabs, axis=1))
        z_ref[rows, :] = acc.astype(_BF16)

    def forget_cumsum():
        lane = lax.broadcasted_iota(jnp.int32, (FOX_HEADS, LANES), 1)
        carry = jnp.zeros((FOX_HEADS, 1), _F32)
        bf = bf_ref[...]
        for ci in range(tm // LANES):
            v = lf_scr[:, ci * LANES:(ci + 1) * LANES] + bf
            blk = jnp.minimum(v, 0.0) - jnp.log1p(jnp.exp(-jnp.abs(v)))
            if (ci + 1) * LANES > n_valid:
                blk = jnp.where(lane + ci * LANES < n_valid, blk, 0.0)
            sh = 1
            while sh < LANES:
                rolled = pltpu.roll(blk, sh, axis=1)
                blk = blk + jnp.where(lane >= sh, rolled, 0.0)
                sh *= 2
            blk = blk + carry
            c_ref[0, :, ci * LANES:(ci + 1) * LANES] = blk
            carry = blk[:, LANES - 1:LANES]

    rot_end = COL_RV * LANES
    rot_tiles = -(-rot_end // tn)

    for jj in range(rot_tiles):
        @pl.when(j == jj)
        def _(jj=jj):
            for r in range(n_sub):
                if jj == 0:
                    normalize(r)
                project(r, min(tn, rot_end - jj * tn))
            if jj == 0:
                forget_cumsum()

    @pl.when(j >= rot_tiles)
    def _():
        for r in range(n_sub):
            project(r, 0)


def _inproj(x2d, g, w_t, w_ff_t, b_f, cos2, sin2, *, tm, tn, n_valid):
    rows = x2d.shape[0]
    sub = min(tm, 512)
    assert D_MAIN % tn == 0 and tn % RET_DK == 0
    kern = functools.partial(_inproj_kernel, tm=tm, tn=tn, sub=sub, n_valid=n_valid)
    return pl.pallas_call(
        kern,
        grid=(rows // tm, D_MAIN // tn),
        in_specs=[
            pl.BlockSpec((tm, D_MODEL), lambda i, j: (i, 0)),
            pl.BlockSpec((1, D_MODEL), lambda i, j: (0, 0)),
            pl.BlockSpec((tn, D_MODEL), lambda i, j: (j, 0)),
            pl.BlockSpec((FOX_HEADS, D_MODEL), lambda i, j: (0, 0)),
            pl.BlockSpec((FOX_HEADS, 1), lambda i, j: (0, 0)),
            pl.BlockSpec((tm, RET_DK), lambda i, j: (0, 0)),
            pl.BlockSpec((tm, RET_DK), lambda i, j: (0, 0)),
        ],
        out_specs=[
            pl.BlockSpec((tm, tn), lambda i, j: (i, j)),
            pl.BlockSpec((1, FOX_HEADS, tm), lambda i, j: (i, 0, 0)),
        ],
        out_shape=[
            jax.ShapeDtypeStruct((rows, D_MAIN), _BF16),
            jax.ShapeDtypeStruct((rows // tm, FOX_HEADS, tm), _F32),
        ],
        scratch_shapes=[
            pltpu.VMEM((tm, D_MODEL), _BF16),
            pltpu.VMEM((FOX_HEADS, tm), _F32),
        ],
        compiler_params=pltpu.CompilerParams(
            dimension_semantics=("arbitrary", "arbitrary"),
            vmem_limit_bytes=VMEM_LIMIT),
        name="inproj",
    )(x2d, g, w_t, w_ff_t, b_f, cos2, sin2)


def _retention_kernel(lg_ref, q_ref, k_ref, v_ref, g_ref, km_ref, vm_ref, o_ref, *, n_chunks):
    h = pl.program_id(1)
    lg = lg_ref[h]
    ri = lax.broadcasted_iota(jnp.int32, (CHUNK, CHUNK), 0)
    ci = lax.broadcasted_iota(jnp.int32, (CHUNK, CHUNK), 1)
    diff = (ri - ci).astype(_F32)
    dmask = jnp.where(diff >= 0, jnp.exp(lg * jnp.maximum(diff, 0.0)), 0.0)
    idx = lax.broadcasted_iota(jnp.int32, (CHUNK, 1), 0).astype(_F32)
    xi = jnp.exp(lg * (idx + 1.0))
    zeta = jnp.exp(lg * (CHUNK - 1.0 - idx))
    chunk_decay = jnp.exp(lg * jnp.full((1, 1), float(CHUNK), _F32))

    zeta_m = jnp.exp(lg * (N_META - 1.0 - idx))
    km = km_ref[...].astype(_F32)
    state0 = _dot((km * zeta_m).T.astype(_BF16), vm_ref[...])

    def chunk_products(n):
        rows = slice(n * CHUNK, (n + 1) * CHUNK)
        kb = k_ref[0, rows, :]
        v = v_ref[0, rows, :]
        scores = (_dot_nt(q_ref[0, rows, :], kb) * dmask).astype(_BF16)
        kv = _dot((kb.astype(_F32) * zeta).T.astype(_BF16), v)
        return scores, kv

    def chunk_output(n, scores, state):
        rows = slice(n * CHUNK, (n + 1) * CHUNK)
        out = _dot(scores, v_ref[0, rows, :])
        out = out + _dot(q_ref[0, rows, :], state.astype(_BF16)) * xi
        y = out * lax.rsqrt(jnp.mean(out * out, axis=-1, keepdims=True) + EPS)
        gate = g_ref[0, rows, :].astype(_F32)
        o_ref[0, rows, :] = (y * _silu(gate)).astype(_BF16)

    state = state0
    scores, kv = chunk_products(0)
    for n in range(n_chunks):
        if n + 1 < n_chunks:
            nxt = chunk_products(n + 1)
        chunk_output(n, scores, state)
        state = state * chunk_decay + kv
        if n + 1 < n_chunks:
            scores, kv = nxt


def _retention(lg, z3, zm):
    b, s, _ = z3.shape
    kern = functools.partial(_retention_kernel, n_chunks=s // CHUNK)
    return pl.pallas_call(
        kern,
        grid=(b, RET_HEADS),
        in_specs=[
            pl.BlockSpec(memory_space=pltpu.SMEM),
            pl.BlockSpec((1, s, RET_DK), lambda i, h: (i, 0, COL_RQ + h)),
            pl.BlockSpec((1, s, RET_DK), lambda i, h: (i, 0, COL_RK + h)),
            pl.BlockSpec((1, s, RET_DV), lambda i, h: (i, 0, COL_RV // 2 + h)),
            pl.BlockSpec((1, s, RET_DV), lambda i, h: (i, 0, COL_RG // 2 + h)),
            pl.BlockSpec((CHUNK, RET_DK), lambda i, h: (0, COL_RK + h)),
            pl.BlockSpec((CHUNK, RET_DV), lambda i, h: (0, COL_RV // 2 + h)),
        ],
        out_specs=pl.BlockSpec((1, s, RET_DV), lambda i, h: (i, 0, h)),
        out_shape=jax.ShapeDtypeStruct((b, s, RET_HEADS * RET_DV), _BF16),
        compiler_params=pltpu.CompilerParams(
            dimension_semantics=("arbitrary", "arbitrary"),
            vmem_limit_bytes=VMEM_LIMIT),
        name="retention",
    )(lg, z3, z3, z3, z3, zm, zm)


def _fox_kernel(q_ref, k_ref, v_ref, g_ref, c_ref, km_ref, vm_ref, cm_ref, o_ref,
                vt_scr, vtm_scr, ka_scr, kam_scr, qt_scr, acc_scr, m_scr, s_scr, pm_scr,
                sm_scr, pmm_scr, *, n_tiles, n_pairs, seq):
    half = FOX_HEAD_DIM
    n_heads = 2 * n_pairs
    sub = lax.broadcasted_iota(jnp.int32, (LANES, 1), 0)
    sub8 = lax.broadcasted_iota(jnp.int32, (8, 1), 0)
    lane = lax.broadcasted_iota(jnp.int32, (1, LANES), 1)

    def pieces8(row):
        out = jnp.zeros((8, LANES), _F32)
        rest = row
        for piece in range(FOX_BIAS_PIECES):
            part = rest.astype(_BF16).astype(_F32)
            out = jnp.where(sub8 == piece, part, out)
            rest = rest - part
        return out

    def augmented_keys(kblk, bias_lo, bias_hi):
        gap = jnp.zeros((half - 8, LANES), _F32)
        tile = jnp.concatenate([pieces8(bias_hi), gap, pieces8(bias_lo), gap], axis=0).T
        tile = tile.astype(_BF16)[0:kblk.shape[0], :]
        return jnp.where(lane < half, kblk, tile), jnp.where(lane < half, tile, kblk)

    def build_meta_operands():
        for p in range(n_pairs):
            lanes = slice(p * LANES, (p + 1) * LANES)
            cm0, cm1 = cm_ref[p, 0:1, :], cm_ref[p, 1:2, :]
            kam_scr[2 * p], kam_scr[2 * p + 1] = augmented_keys(
                km_ref[0:N_META, lanes],
                (cm0[:, LANES - 1:LANES] - cm0) * LOG2E, (cm1[:, LANES - 1:LANES] - cm1) * LOG2E)
            vtm = vm_ref[:, lanes].astype(_F32).T.astype(_BF16)
            for hh in range(2):
                h = 2 * p + hh
                vtm_scr[h, 0:half, :] = vtm[hh * half:(hh + 1) * half, :]
                vtm_scr[h, half:FOX_ACC_ROWS, :] = jnp.ones((FOX_ACC_ROWS - half, LANES), _BF16)

    def build_block_operands(jb):
        rows = slice(jb * FOX_TK, (jb + 1) * FOX_TK)
        for p in range(n_pairs):
            lanes = slice(p * LANES, (p + 1) * LANES)
            for cb in range(jb * FOX_TK // LANES, (jb + 1) * FOX_TK // LANES):
                ks = slice(cb * LANES, (cb + 1) * LANES)
                ka_scr[2 * p, ks, :], ka_scr[2 * p + 1, ks, :] = augmented_keys(
                    k_ref[0, ks, lanes],
                    c_ref[0, p, 0:1, ks] * -LOG2E, c_ref[0, p, 1:2, ks] * -LOG2E)
            vt = v_ref[0, rows, lanes].astype(_F32).T.astype(_BF16)
            for hh in range(2):
                h = 2 * p + hh
                vt_scr[h, jb, 0:half, :] = vt[hh * half:(hh + 1) * half, :]
                vt_scr[h, jb, half:FOX_ACC_ROWS, :] = jnp.ones(
                    (FOX_ACC_ROWS - half, FOX_TK), _BF16)

    ri = lax.broadcasted_iota(jnp.int32, (FOX_TK, FOX_TQ), 0)
    ci = lax.broadcasted_iota(jnp.int32, (FOX_TK, FOX_TQ), 1)
    causal = ri <= ci

    def group_max(st):
        parts = [st[r:r + 8, :] for r in range(0, st.shape[0], 8)]
        while len(parts) > 1:
            nxt = [jnp.maximum(parts[a], parts[a + 1]) for a in range(0, len(parts) - 1, 2)]
            if len(parts) % 2:
                nxt.append(parts[-1])
            parts = nxt
        return parts[0]

    def update(h, tile, first, st, pm, vt):
        buf = tile % 2
        m_blk = jnp.max(pm, axis=0, keepdims=True)
        if first:
            m_new = m_blk
        else:
            m_old = m_scr[buf, h]
            m_new = jnp.maximum(m_old, m_blk)
            alpha = jnp.exp2(m_old - m_new)
        pt = jnp.exp2(st - m_new).astype(_BF16)
        if pt.shape[0] < LANES:
            pt = jnp.concatenate(
                [pt, jnp.zeros((LANES - pt.shape[0], FOX_TQ), _BF16)], axis=0)
        pv = _dot(vt, pt)
        acc_scr[buf, h] = pv if first else acc_scr[buf, h] * alpha + pv
        m_scr[buf, h] = m_new

    def prepare_queries(i):
        rows = slice(i * FOX_TQ, (i + 1) * FOX_TQ)
        for p in range(n_pairs):
            qt = q_ref[0, rows, p * LANES:(p + 1) * LANES].astype(_F32).T
            ones_hi = jnp.where(sub < half + FOX_BIAS_PIECES, 1.0, 0.0)
            ones_lo = jnp.where(sub < FOX_BIAS_PIECES, 1.0, 0.0)
            qt_scr[i % 2, 2 * p] = jnp.where(sub < half, qt, ones_hi).astype(_BF16)
            qt_scr[i % 2, 2 * p + 1] = jnp.where(sub < half, ones_lo, qt).astype(_BF16)

    def scores(item):
        kind, i, j, slot = item
        if kind == "meta":
            for h in range(n_heads):
                st = _dot(kam_scr[h], qt_scr[i % 2, h])
                sm_scr[h] = st
                pmm_scr[h] = group_max(st)
            return
        krows = slice(j * FOX_TK, (j + 1) * FOX_TK)
        for h in range(n_heads):
            st = _dot(ka_scr[h, krows, :], qt_scr[i % 2, h])
            if i == j:
                st = jnp.where(causal, st, NEG_INF)
            s_scr[slot, h] = st
            pm_scr[slot, h] = group_max(st)

    def consume(item):
        kind, i, j, slot = item
        for h in range(n_heads):
            if kind == "meta":
                update(h, i, True, sm_scr[h], pmm_scr[h], vtm_scr[h])
            else:
                update(h, i, False, s_scr[slot, h], pm_scr[slot, h], vt_scr[h, j])

    def finalize(i):
        rows = slice(i * FOX_TQ, (i + 1) * FOX_TQ)
        for p in range(n_pairs):
            lanes = slice(p * LANES, (p + 1) * LANES)
            outs = []
            for hh in range(2):
                acc = acc_scr[i % 2, 2 * p + hh]
                outs.append(acc[0:half, :] / acc[half:half + 1, :])
            y = jnp.concatenate(outs, axis=0).T
            gate = g_ref[0, rows, lanes].astype(_F32)
            o_ref[0, rows, lanes] = (y * _silu(gate)).astype(_BF16)

    items = []
    for i in range(n_tiles):
        items.append(("meta", i, None, None))
        for j in [i] + list(range(i)):
            items.append(("blk", i, j, sum(it[0] == "blk" for it in items) % 2))
    build_meta_operands()
    build_block_operands(0)
    prepare_queries(0)
    scores(items[0])
    for t, item in enumerate(items):
        if t + 1 < len(items):
            nxt = items[t + 1]
            if nxt[0] == "meta":
                build_block_operands(nxt[1])
                prepare_queries(nxt[1])
            scores(nxt)
        consume(item)
        if t + 1 == len(items) or items[t + 1][1] != item[1]:
            finalize(item[1])


def _fox(z3, c4, zm, cm3):
    b, s, _ = z3.shape
    n_pairs = FOX_PAIRS
    n_groups = FOX_HEADS // 2 // n_pairs
    n_tiles = s // FOX_TQ
    w = n_pairs * LANES
    kern = functools.partial(_fox_kernel, n_tiles=n_tiles, n_pairs=n_pairs, seq=s)
    return pl.pallas_call(
        kern,
        grid=(b, n_groups),
        in_specs=[
            pl.BlockSpec((1, s, w), lambda i, p: (i, 0, COL_FQ // n_pairs + p)),
            pl.BlockSpec((1, s, w), lambda i, p: (i, 0, COL_FK // n_pairs + p)),
            pl.BlockSpec((1, s, w), lambda i, p: (i, 0, COL_FV // n_pairs + p)),
            pl.BlockSpec((1, s, w), lambda i, p: (i, 0, COL_FG // n_pairs + p)),
            pl.BlockSpec((1, n_pairs, 2, s), lambda i, p: (i, p, 0, 0)),
            pl.BlockSpec((CHUNK, w), lambda i, p: (0, COL_FK // n_pairs + p)),
            pl.BlockSpec((CHUNK, w), lambda i, p: (0, COL_FV // n_pairs + p)),
            pl.BlockSpec((n_pairs, 2, LANES), lambda i, p: (p, 0, 0)),
        ],
        out_specs=pl.BlockSpec((1, s, w), lambda i, p: (i, 0, p)),
        out_shape=jax.ShapeDtypeStruct((b, s, FOX_HEADS * FOX_HEAD_DIM), _BF16),
        scratch_shapes=[
            pltpu.VMEM((2 * n_pairs, n_tiles, FOX_ACC_ROWS, FOX_TK), _BF16),
            pltpu.VMEM((2 * n_pairs, FOX_ACC_ROWS, LANES), _BF16),
            pltpu.VMEM((2 * n_pairs, s, LANES), _BF16),
            pltpu.VMEM((2 * n_pairs, N_META, LANES), _BF16),
            pltpu.VMEM((2, 2 * n_pairs, LANES, FOX_TQ), _BF16),
            pltpu.VMEM((2, 2 * n_pairs, FOX_ACC_ROWS, FOX_TQ), _F32),
            pltpu.VMEM((2, 2 * n_pairs, 1, FOX_TQ), _F32),
            pltpu.VMEM((2, 2 * n_pairs, FOX_TK, FOX_TQ), _F32),
            pltpu.VMEM((2, 2 * n_pairs, 8, FOX_TQ), _F32),
            pltpu.VMEM((2 * n_pairs, N_META, FOX_TQ), _F32),
            pltpu.VMEM((2 * n_pairs, 8, FOX_TQ), _F32),
        ],
        compiler_params=pltpu.CompilerParams(
            dimension_semantics=("arbitrary", "arbitrary"),
            vmem_limit_bytes=VMEM_LIMIT),
        name="fox",
    )(z3, z3, z3, z3, c4, zm, zm, cm3)


def _outproj_kernel(yr_ref, yf_ref, x_ref, w_ref, g_ref, o_ref):
    half = w_ref.shape[0] // 2
    hres = x_ref[...] + _dot(yr_ref[...], w_ref[:half, :]) + _dot(yf_ref[...], w_ref[half:, :])
    ms = jnp.mean(hres * hres, axis=-1, keepdims=True)
    o_ref[...] = hres * lax.rsqrt(ms + EPS) * g_ref[...]


def _outproj(yr, yf, x2d, w_out, g, *, tm):
    rows = x2d.shape[0]
    return pl.pallas_call(
        _outproj_kernel,
        grid=(rows // tm,),
        in_specs=[
            pl.BlockSpec((tm, yr.shape[1]), lambda i: (i, 0)),
            pl.BlockSpec((tm, yf.shape[1]), lambda i: (i, 0)),
            pl.BlockSpec((tm, D_MODEL), lambda i: (i, 0)),
            pl.BlockSpec(w_out.shape, lambda i: (0, 0)),
            pl.BlockSpec((1, D_MODEL), lambda i: (0, 0)),
        ],
        out_specs=pl.BlockSpec((tm, D_MODEL), lambda i: (i, 0)),
        out_shape=jax.ShapeDtypeStruct((rows, D_MODEL), _F32),
        compiler_params=pltpu.CompilerParams(
            dimension_semantics=("arbitrary",),
            vmem_limit_bytes=VMEM_LIMIT),
        name="outproj",
    )(yr, yf, x2d, w_out, g)


def _rope_tables(pos):
    inv = ROPE_BASE ** (-jnp.arange(0, RET_DK, 2, dtype=_F32) / RET_DK)
    ang = pos[:, None] * inv[None, :]
    cos, sin = jnp.cos(ang), jnp.sin(ang)
    return jnp.concatenate([cos, cos], axis=-1), jnp.concatenate([-sin, sin], axis=-1)


def kernel(x, meta_tokens, norm_g, w_in, b_f, w_out, final_g):
    b, s, d = x.shape
    assert norm_g.shape[0] == 1 and d == D_MODEL and s % FOX_TQ == 0
    x2d = x.reshape(b * s, d)
    col_scale = jnp.ones((D_MAIN, 1), _F32)
    col_scale = col_scale.at[COL_RK * LANES:COL_RV * LANES].set(RET_DK ** -0.5)
    col_scale = col_scale.at[COL_FQ * LANES:COL_FK * LANES].set(FOX_HEAD_DIM ** -0.5 * LOG2E)
    w_all_t = jnp.swapaxes(w_in, 1, 2)[0]
    w_t = (w_all_t[:D_MAIN, :] * col_scale).astype(_BF16)
    w_ff_t = w_all_t[D_MAIN:, :]
    g = norm_g[0].reshape(1, d)
    bf = b_f[0].reshape(FOX_HEADS, 1)
    meta_pad = jnp.pad(meta_tokens.astype(_F32), ((0, CHUNK - N_META), (0, 0)))
    cos2, sin2 = _rope_tables(jnp.arange(s, dtype=_F32) + float(N_META))
    cosm, sinm = _rope_tables(jnp.arange(CHUNK, dtype=_F32))

    z, c = _inproj(x2d, g, w_t, w_ff_t, bf, cos2, sin2, tm=s, tn=1792, n_valid=s)
    zm, cm = _inproj(meta_pad, g, w_t, w_ff_t, bf, cosm, sinm,
                     tm=CHUNK, tn=512, n_valid=N_META)

    z3 = z.reshape(b, s, D_MAIN)
    lg = jnp.log1p(-jnp.exp2(-5.0 - jnp.arange(RET_HEADS, dtype=_F32)))
    y_r = _retention(lg, z3, zm)

    c4 = c.reshape(b, FOX_HEADS // 2, 2, s)
    cm3 = cm.reshape(FOX_HEADS // 2, 2, CHUNK)
    y_f = _fox(z3, c4, zm, cm3)

    out = _outproj(y_r.reshape(b * s, -1), y_f.reshape(b * s, -1), x2d,
                   w_out[0].astype(_BF16), final_g.reshape(1, d), tm=1024)
    return out.reshape(b, s, d)
```

```python
import functools

import jax
import jax.numpy as jnp
from jax import lax
from jax.experimental import pallas as pl
from jax.experimental.pallas import tpu as pltpu

D_MODEL = 1024
N_META = 16
CHUNK = 128
RET_HEADS = 4
RET_DK = 128
RET_DV = 256
FOX_HEADS = 16
FOX_HEAD_DIM = 64
ROPE_BASE = 10000.0
EPS = 1e-6
NEG_INF = -1e30
LOG2E = 1.4426950408889634

LANES = 128
D_MAIN = 7168
COL_RQ, COL_RK, COL_RV, COL_RG = 0, 4, 8, 16
COL_FQ, COL_FK, COL_FV, COL_FG = 24, 32, 40, 48
FOX_TQ = 256
FOX_TK = 256
FOX_PAIRS = 2
FOX_ACC_ROWS = FOX_HEAD_DIM + 16
FOX_BIAS_PIECES = 3
VMEM_LIMIT = 56 * 1024 * 1024

_F32 = jnp.float32
_BF16 = jnp.bfloat16


def _dot(a, b):
    return jnp.dot(a, b, preferred_element_type=_F32)


def _dot_nt(a, b):
    return lax.dot_general(a, b, (((1,), (1,)), ((), ())), preferred_element_type=_F32)


def _silu(g):
    h = 0.5 * g
    return h + h * jnp.tanh(h)


def _rotary(x, cos2, sin2):
    return x * cos2 + pltpu.roll(x, RET_DK // 2, axis=1) * sin2


def _inproj_kernel(x_ref, g_ref, w_ref, wff_ref, bf_ref, cos_ref, sin_ref,
                   z_ref, c_ref, u_scr, lf_scr, *, tm, tn, sub, n_valid):
    j = pl.program_id(1)
    n_sub = tm // sub

    def normalize(r):
        rows = pl.ds(r * sub, sub)
        xf = x_ref[rows, :]
        ms = jnp.mean(xf * xf, axis=-1, keepdims=True)
        u = (xf * lax.rsqrt(ms + EPS) * g_ref[...]).astype(_BF16)
        u_scr[rows, :] = u
        lf_scr[:, r * sub:(r + 1) * sub] = _dot_nt(wff_ref[...].astype(_BF16), u)

    def project(r, rot_cols):
        rows = pl.ds(r * sub, sub)
        acc = _dot_nt(u_scr[rows, :], w_ref[...])
        if rot_cols:
            cos2, sin2 = cos_ref[rows, :], sin_ref[rows, :]
            slabs = [_rotary(acc[:, a:a + RET_DK], cos2, sin2) for a in range(0, rot_cols, RET_DK)]
            acc = jnp.concatenate(slabs + [acc[:, rot_cols:]], axis=1) if rot_cols < tn else (
                jnp.concatenate(slabs, axis=1))
        z_ref[rows, :] = acc.astype(_BF16)

    def forget_cumsum():
        lane = lax.broadcasted_iota(jnp.int32, (FOX_HEADS, LANES), 1)
        carry = jnp.zeros((FOX_HEADS, 1), _F32)
        bf = bf_ref[...]
        for ci in range(tm // LANES):
            v = lf_scr[:, ci * LANES:(ci + 1) * LANES] + bf
            blk = jnp.minimum(v, 0.0) - jnp.log1p(jnp.exp(-jnp.abs(v)))
            if (ci + 1) * LANES > n_valid:
                blk = jnp.where(lane + ci * LANES < n_valid, blk, 0.0)
            sh = 1
            while sh < LANES:
                rolled = pltpu.roll(blk, sh, axis=1)
                blk = blk + jnp.where(lane >= sh, rolled, 0.0)
                sh *= 2
            blk = blk + carry
            c_ref[0, :, ci * LANES:(ci + 1) * LANES] = blk
            carry = blk[:, LANES - 1:LANES]

    rot_end = COL_RV * LANES
    rot_tiles = -(-rot_end // tn)

    for jj in range(rot_tiles):
        @pl.when(j == jj)
        def _(jj=jj):
            for r in range(n_sub):
                if jj == 0:
                    normalize(r)
                project(r, min(tn, rot_end - jj * tn))
            if jj == 0:
                forget_cumsum()

    @pl.when(j >= rot_tiles)
    def _():
        for r in range(n_sub):
            project(r, 0)


def _inproj(x2d, g, w_t, w_ff_t, b_f, cos2, sin2, *, tm, tn, n_valid):
    rows = x2d.shape[0]
    sub = min(tm, 512)
    assert D_MAIN % tn == 0 and tn % RET_DK == 0
    kern = functools.partial(_inproj_kernel, tm=tm, tn=tn, sub=sub, n_valid=n_valid)
    return pl.pallas_call(
        kern,
        grid=(rows // tm, D_MAIN // tn),
        in_specs=[
            pl.BlockSpec((tm, D_MODEL), lambda i, j: (i, 0)),
            pl.BlockSpec((1, D_MODEL), lambda i, j: (0, 0)),
            pl.BlockSpec((tn, D_MODEL), lambda i, j: (j, 0)),
            pl.BlockSpec((FOX_HEADS, D_MODEL), lambda i, j: (0, 0)),
            pl.BlockSpec((FOX_HEADS, 1), lambda i, j: (0, 0)),
            pl.BlockSpec((tm, RET_DK), lambda i, j: (0, 0)),
            pl.BlockSpec((tm, RET_DK), lambda i, j: (0, 0)),
        ],
        out_specs=[
            pl.BlockSpec((tm, tn), lambda i, j: (i, j)),
            pl.BlockSpec((1, FOX_HEADS, tm), lambda i, j: (i, 0, 0)),
        ],
        out_shape=[
            jax.ShapeDtypeStruct((rows, D_MAIN), _BF16),
            jax.ShapeDtypeStruct((rows // tm, FOX_HEADS, tm), _F32),
        ],
        scratch_shapes=[
            pltpu.VMEM((tm, D_MODEL), _BF16),
            pltpu.VMEM((FOX_HEADS, tm), _F32),
        ],
        compiler_params=pltpu.CompilerParams(
            dimension_semantics=("arbitrary", "arbitrary"),
            vmem_limit_bytes=VMEM_LIMIT),
        name="inproj",
    )(x2d, g, w_t, w_ff_t, b_f, cos2, sin2)


def _retention_kernel(lg_ref, q_ref, k_ref, v_ref, g_ref, km_ref, vm_ref, o_ref, *, n_chunks):
    h = pl.program_id(1)
    lg = lg_ref[h]
    ri = lax.broadcasted_iota(jnp.int32, (CHUNK, CHUNK), 0)
    ci = lax.broadcasted_iota(jnp.int32, (CHUNK, CHUNK), 1)
    diff = (ri - ci).astype(_F32)
    dmask = jnp.where(diff >= 0, jnp.exp(lg * jnp.maximum(diff, 0.0)), 0.0)
    idx = lax.broadcasted_iota(jnp.int32, (CHUNK, 1), 0).astype(_F32)
    xi = jnp.exp(lg * (idx + 1.0))
    zeta = jnp.exp(lg * (CHUNK - 1.0 - idx))
    chunk_decay = jnp.exp(lg * jnp.full((1, 1), float(CHUNK), _F32))

    zeta_m = jnp.exp(lg * (N_META - 1.0 - idx))
    km = km_ref[...].astype(_F32)
    state0 = _dot((km * zeta_m).T.astype(_BF16), vm_ref[...])

    def chunk_products(n):
        rows = slice(n * CHUNK, (n + 1) * CHUNK)
        kb = k_ref[0, rows, :]
        v = v_ref[0, rows, :]
        scores = (_dot_nt(q_ref[0, rows, :], kb) * dmask).astype(_BF16)
        kv = _dot((kb.astype(_F32) * zeta).T.astype(_BF16), v)
        return scores, kv

    def chunk_output(n, scores, state):
        rows = slice(n * CHUNK, (n + 1) * CHUNK)
        out = _dot(scores, v_ref[0, rows, :])
        out = out + _dot(q_ref[0, rows, :], state.astype(_BF16)) * xi
        y = out * lax.rsqrt(jnp.mean(out * out, axis=-1, keepdims=True) + EPS)
        gate = g_ref[0, rows, :].astype(_F32)
        o_ref[0, rows, :] = (y * _silu(gate)).astype(_BF16)

    state = state0
    scores, kv = chunk_products(0)
    for n in range(n_chunks):
        if n + 1 < n_chunks:
            nxt = chunk_products(n + 1)
        chunk_output(n, scores, state)
        state = state * chunk_decay + kv
        if n + 1 < n_chunks:
            scores, kv = nxt


def _retention(lg, z3, zm):
    b, s, _ = z3.shape
    kern = functools.partial(_retention_kernel, n_chunks=s // CHUNK)
    return pl.pallas_call(
        kern,
        grid=(b, RET_HEADS),
        in_specs=[
            pl.BlockSpec(memory_space=pltpu.SMEM),
            pl.BlockSpec((1, s, RET_DK), lambda i, h: (i, 0, COL_RQ + h)),
            pl.BlockSpec((1, s, RET_DK), lambda i, h: (i, 0, COL_RK + h)),
            pl.BlockSpec((1, s, RET_DV), lambda i, h: (i, 0, COL_RV // 2 + h)),
            pl.BlockSpec((1, s, RET_DV), lambda i, h: (i, 0, COL_RG // 2 + h)),
            pl.BlockSpec((CHUNK, RET_DK), lambda i, h: (0, COL_RK + h)),
            pl.BlockSpec((CHUNK, RET_DV), lambda i, h: (0, COL_RV // 2 + h)),
        ],
        out_specs=pl.BlockSpec((1, s, RET_DV), lambda i, h: (i, 0, h)),
        out_shape=jax.ShapeDtypeStruct((b, s, RET_HEADS * RET_DV), _BF16),
        compiler_params=pltpu.CompilerParams(
            dimension_semantics=("arbitrary", "arbitrary"),
            vmem_limit_bytes=VMEM_LIMIT),
        name="retention",
    )(lg, z3, z3, z3, z3, zm, zm)


def _fox_kernel(q_ref, k_ref, v_ref, g_ref, c_ref, km_ref, vm_ref, cm_ref, o_ref,
                vt_scr, vtm_scr, ka_scr, kam_scr, qt_scr, acc_scr, m_scr, s_scr, pm_scr,
                sm_scr, pmm_scr, *, n_tiles, n_pairs, seq):
    half = FOX_HEAD_DIM
    n_heads = 2 * n_pairs
    sub = lax.broadcasted_iota(jnp.int32, (LANES, 1), 0)
    sub8 = lax.broadcasted_iota(jnp.int32, (8, 1), 0)
    lane = lax.broadcasted_iota(jnp.int32, (1, LANES), 1)

    def pieces8(row):
        out = jnp.zeros((8, LANES), _F32)
        rest = row
        for piece in range(FOX_BIAS_PIECES):
            part = rest.astype(_BF16).astype(_F32)
            out = jnp.where(sub8 == piece, part, out)
            rest = rest - part
        return out

    def augmented_keys(kblk, bias_lo, bias_hi):
        gap = jnp.zeros((half - 8, LANES), _F32)
        tile = jnp.concatenate([pieces8(bias_hi), gap, pieces8(bias_lo), gap], axis=0).T
        tile = tile.astype(_BF16)[0:kblk.shape[0], :]
        return jnp.where(lane < half, kblk, tile), jnp.where(lane < half, tile, kblk)

    def build_meta_operands():
        for p in range(n_pairs):
            lanes = slice(p * LANES, (p + 1) * LANES)
            cm0, cm1 = cm_ref[p, 0:1, :], cm_ref[p, 1:2, :]
            kam_scr[2 * p], kam_scr[2 * p + 1] = augmented_keys(
                km_ref[0:N_META, lanes],
                (cm0[:, LANES - 1:LANES] - cm0) * LOG2E, (cm1[:, LANES - 1:LANES] - cm1) * LOG2E)
            vtm = vm_ref[:, lanes].T
            for hh in range(2):
                h = 2 * p + hh
                vtm_scr[h, 0:half, :] = vtm[hh * half:(hh + 1) * half, :]
                vtm_scr[h, half:FOX_ACC_ROWS, :] = jnp.ones((FOX_ACC_ROWS - half, LANES), _BF16)

    def build_block_operands(jb):
        rows = slice(jb * FOX_TK, (jb + 1) * FOX_TK)
        for p in range(n_pairs):
            lanes = slice(p * LANES, (p + 1) * LANES)
            for cb in range(jb * FOX_TK // LANES, (jb + 1) * FOX_TK // LANES):
                ks = slice(cb * LANES, (cb + 1) * LANES)
                ka_scr[2 * p, ks, :], ka_scr[2 * p + 1, ks, :] = augmented_keys(
                    k_ref[0, ks, lanes],
                    c_ref[0, p, 0:1, ks] * -LOG2E, c_ref[0, p, 1:2, ks] * -LOG2E)
            vt = v_ref[0, rows, lanes].T
            for hh in range(2):
                h = 2 * p + hh
                vt_scr[h, jb, 0:half, :] = vt[hh * half:(hh + 1) * half, :]
                vt_scr[h, jb, half:FOX_ACC_ROWS, :] = jnp.ones(
                    (FOX_ACC_ROWS - half, FOX_TK), _BF16)

    ri = lax.broadcasted_iota(jnp.int32, (FOX_TK, FOX_TQ), 0)
    ci = lax.broadcasted_iota(jnp.int32, (FOX_TK, FOX_TQ), 1)
    causal = ri <= ci

    def group_max(st):
        parts = [st[r:r + 8, :] for r in range(0, st.shape[0], 8)]
        while len(parts) > 1:
            nxt = [jnp.maximum(parts[a], parts[a + 1]) for a in range(0, len(parts) - 1, 2)]
            if len(parts) % 2:
                nxt.append(parts[-1])
            parts = nxt
        return parts[0]

    def update(h, tile, first, st, pm, vt):
        buf = tile % 2
        m_blk = jnp.max(pm, axis=0, keepdims=True)
        if first:
            m_new = m_blk
        else:
            m_old = m_scr[buf, h]
            m_new = jnp.maximum(m_old, m_blk)
            alpha = jnp.exp2(m_old - m_new)
        pt = jnp.exp2(st - m_new).astype(_BF16)
        if pt.shape[0] < LANES:
            pt = jnp.concatenate(
                [pt, jnp.zeros((LANES - pt.shape[0], FOX_TQ), _BF16)], axis=0)
        pv = _dot(vt, pt)
        acc_scr[buf, h] = pv if first else acc_scr[buf, h] * alpha + pv
        m_scr[buf, h] = m_new

    def prepare_queries(i):
        rows = slice(i * FOX_TQ, (i + 1) * FOX_TQ)
        for p in range(n_pairs):
            qt = q_ref[0, rows, p * LANES:(p + 1) * LANES].astype(_F32).T
            ones_hi = jnp.where(sub < half + FOX_BIAS_PIECES, 1.0, 0.0)
            ones_lo = jnp.where(sub < FOX_BIAS_PIECES, 1.0, 0.0)
            qt_scr[i % 2, 2 * p] = jnp.where(sub < half, qt, ones_hi).astype(_BF16)
            qt_scr[i % 2, 2 * p + 1] = jnp.where(sub < half, ones_lo, qt).astype(_BF16)

    def scores(item, heads):
        kind, i, j, slot = item
        if kind == "meta":
            for h in heads:
                st = _dot(kam_scr[h], qt_scr[i % 2, h])
                sm_scr[h] = st
                pmm_scr[h] = group_max(st)
            return
        krows = slice(j * FOX_TK, (j + 1) * FOX_TK)
        for h in heads:
            st = _dot(ka_scr[h, krows, :], qt_scr[i % 2, h])
            if i == j:
                st = jnp.where(causal, st, NEG_INF)
            s_scr[slot, h] = st
            pm_scr[slot, h] = group_max(st)

    def consume(item, heads):
        kind, i, j, slot = item
        for h in heads:
            if kind == "meta":
                update(h, i, True, sm_scr[h], pmm_scr[h], vtm_scr[h])
            else:
                update(h, i, False, s_scr[slot, h], pm_scr[slot, h], vt_scr[h, j])

    def finalize(i):
        rows = slice(i * FOX_TQ, (i + 1) * FOX_TQ)
        for p in range(n_pairs):
            lanes = slice(p * LANES, (p + 1) * LANES)
            outs = []
            for hh in range(2):
                acc = acc_scr[i % 2, 2 * p + hh]
                outs.append(acc[0:half, :] / acc[half:half + 1, :])
            y = jnp.concatenate(outs, axis=0).T
            gate = g_ref[0, rows, lanes].astype(_F32)
            o_ref[0, rows, lanes] = (y * _silu(gate)).astype(_BF16)

    items = []
    for i in range(n_tiles):
        items.append(("meta", i, None, None))
        for j in [i] + list(range(i)):
            items.append(("blk", i, j, sum(it[0] == "blk" for it in items) % 2))
    build_meta_operands()
    build_block_operands(0)
    prepare_queries(0)
    scores(items[0], range(n_heads))
    for t, item in enumerate(items):
        if t + 1 < len(items):
            nxt = items[t + 1]
            if nxt[0] == "meta":
                build_block_operands(nxt[1])
                prepare_queries(nxt[1])
            scores(nxt, range(n_heads))
        consume(item, range(n_heads))
        if t + 1 == len(items) or items[t + 1][1] != item[1]:
            finalize(item[1])


def _fox(z3, c4, zm, cm3):
    b, s, _ = z3.shape
    n_pairs = FOX_PAIRS
    n_groups = FOX_HEADS // 2 // n_pairs
    n_tiles = s // FOX_TQ
    w = n_pairs * LANES
    kern = functools.partial(_fox_kernel, n_tiles=n_tiles, n_pairs=n_pairs, seq=s)
    return pl.pallas_call(
        kern,
        grid=(b, n_groups),
        in_specs=[
            pl.BlockSpec((1, s, w), lambda i, p: (i, 0, COL_FQ // n_pairs + p)),
            pl.BlockSpec((1, s, w), lambda i, p: (i, 0, COL_FK // n_pairs + p)),
            pl.BlockSpec((1, s, w), lambda i, p: (i, 0, COL_FV // n_pairs + p)),
            pl.BlockSpec((1, s, w), lambda i, p: (i, 0, COL_FG // n_pairs + p)),
            pl.BlockSpec((1, n_pairs, 2, s), lambda i, p: (i, p, 0, 0)),
            pl.BlockSpec((CHUNK, w), lambda i, p: (0, COL_FK // n_pairs + p)),
            pl.BlockSpec((CHUNK, w), lambda i, p: (0, COL_FV // n_pairs + p)),
            pl.BlockSpec((n_pairs, 2, LANES), lambda i, p: (p, 0, 0)),
        ],
        out_specs=pl.BlockSpec((1, s, w), lambda i, p: (i, 0, p)),
        out_shape=jax.ShapeDtypeStruct((b, s, FOX_HEADS * FOX_HEAD_DIM), _BF16),
        scratch_shapes=[
            pltpu.VMEM((2 * n_pairs, n_tiles, FOX_ACC_ROWS, FOX_TK), _BF16),
            pltpu.VMEM((2 * n_pairs, FOX_ACC_ROWS, LANES), _BF16),
            pltpu.VMEM((2 * n_pairs, s, LANES), _BF16),
            pltpu.VMEM((2 * n_pairs, N_META, LANES), _BF16),
            pltpu.VMEM((2, 2 * n_pairs, LANES, FOX_TQ), _BF16),
            pltpu.VMEM((2, 2 * n_pairs, FOX_ACC_ROWS, FOX_TQ), _F32),
            pltpu.VMEM((2, 2 * n_pairs, 1, FOX_TQ), _F32),
            pltpu.VMEM((2, 2 * n_pairs, FOX_TK, FOX_TQ), _F32),
            pltpu.VMEM((2, 2 * n_pairs, 8, FOX_TQ), _F32),
            pltpu.VMEM((2 * n_pairs, N_META, FOX_TQ), _F32),
            pltpu.VMEM((2 * n_pairs, 8, FOX_TQ), _F32),
        ],
        compiler_params=pltpu.CompilerParams(
            dimension_semantics=("arbitrary", "arbitrary"),
            vmem_limit_bytes=VMEM_LIMIT),
        name="fox",
    )(z3, z3, z3, z3, c4, zm, zm, cm3)


def _outproj_kernel(yr_ref, yf_ref, x_ref, w_ref, g_ref, o_ref):
    half = w_ref.shape[0] // 2
    hres = x_ref[...] + _dot(yr_ref[...], w_ref[:half, :]) + _dot(yf_ref[...], w_ref[half:, :])
    ms = jnp.mean(hres * hres, axis=-1, keepdims=True)
    o_ref[...] = hres * lax.rsqrt(ms + EPS) * g_ref[...]


def _outproj(yr, yf, x2d, w_out, g, *, tm):
    rows = x2d.shape[0]
    return pl.pallas_call(
        _outproj_kernel,
        grid=(rows // tm,),
        in_specs=[
            pl.BlockSpec((tm, yr.shape[1]), lambda i: (i, 0)),
            pl.BlockSpec((tm, yf.shape[1]), lambda i: (i, 0)),
            pl.BlockSpec((tm, D_MODEL), lambda i: (i, 0)),
            pl.BlockSpec(w_out.shape, lambda i: (0, 0)),
            pl.BlockSpec((1, D_MODEL), lambda i: (0, 0)),
        ],
        out_specs=pl.BlockSpec((tm, D_MODEL), lambda i: (i, 0)),
        out_shape=jax.ShapeDtypeStruct((rows, D_MODEL), _F32),
        compiler_params=pltpu.CompilerParams(
            dimension_semantics=("arbitrary",),
            vmem_limit_bytes=VMEM_LIMIT),
        name="outproj",
    )(yr, yf, x2d, w_out, g)


def _rope_tables(pos):
    inv = ROPE_BASE ** (-jnp.arange(0, RET_DK, 2, dtype=_F32) / RET_DK)
    ang = pos[:, None] * inv[None, :]
    cos, sin = jnp.cos(ang), jnp.sin(ang)
    return jnp.concatenate([cos, cos], axis=-1), jnp.concatenate([-sin, sin], axis=-1)


def kernel(x, meta_tokens, norm_g, w_in, b_f, w_out, final_g):
    b, s, d = x.shape
    assert norm_g.shape[0] == 1 and d == D_MODEL and s % FOX_TQ == 0
    x2d = x.reshape(b * s, d)
    col_scale = jnp.ones((D_MAIN, 1), _F32)
    col_scale = col_scale.at[COL_RK * LANES:COL_RV * LANES].set(RET_DK ** -0.5)
    col_scale = col_scale.at[COL_FQ * LANES:COL_FK * LANES].set(FOX_HEAD_DIM ** -0.5 * LOG2E)
    w_all_t = jnp.swapaxes(w_in, 1, 2)[0]
    w_t = (w_all_t[:D_MAIN, :] * col_scale).astype(_BF16)
    w_ff_t = w_all_t[D_MAIN:, :]
    g = norm_g[0].reshape(1, d)
    bf = b_f[0].reshape(FOX_HEADS, 1)
    meta_pad = jnp.pad(meta_tokens.astype(_F32), ((0, CHUNK - N_META), (0, 0)))
    cos2, sin2 = _rope_tables(jnp.arange(s, dtype=_F32) + float(N_META))
    cosm, sinm = _rope_tables(jnp.arange(CHUNK, dtype=_F32))

    z, c = _inproj(x2d, g, w_t, w_ff_t, bf, cos2, sin2, tm=s, tn=1792, n_valid=s)
    zm, cm = _inproj(meta_pad, g, w_t, w_ff_t, bf, cosm, sinm,
                     tm=CHUNK, tn=3584, n_valid=N_META)

    z3 = z.reshape(b, s, D_MAIN)
    lg = jnp.log1p(-jnp.exp2(-5.0 - jnp.arange(RET_HEADS, dtype=_F32)))
    y_r = _retention(lg, z3, zm)

    c4 = c.reshape(b, FOX_HEADS // 2, 2, s)
    cm3 = cm.reshape(FOX_HEADS // 2, 2, CHUNK)
    y_f = _fox(z3, c4, zm, cm3)

    out = _outproj(y_r.reshape(b * s, -1), y_f.reshape(b * s, -1), x2d,
                   w_out[0].astype(_BF16), final_g.reshape(1, d), tm=1024)
    return out.reshape(b, s, d)
```

```python
import functools

import jax
import jax.numpy as jnp
from jax import lax
from jax.experimental import pallas as pl
from jax.experimental.pallas import tpu as pltpu

D_MODEL = 1024
N_META = 16
CHUNK = 128
RET_HEADS = 4
RET_DK = 128
RET_DV = 256
FOX_HEADS = 16
FOX_HEAD_DIM = 64
ROPE_BASE = 10000.0
EPS = 1e-6
NEG_INF = -1e30
LOG2E = 1.4426950408889634

LANES = 128
D_MAIN = 7168
COL_RQ, COL_RK, COL_RV, COL_RG = 0, 4, 8, 16
COL_FQ, COL_FK, COL_FV, COL_FG = 24, 32, 40, 48
FOX_TQ = 256
FOX_TK = 256
FOX_PAIRS = 2
FOX_ACC_ROWS = FOX_HEAD_DIM + 16
FOX_BIAS_PIECES = 3
VMEM_LIMIT = 56 * 1024 * 1024

_F32 = jnp.float32
_BF16 = jnp.bfloat16


def _dot(a, b):
    return jnp.dot(a, b, preferred_element_type=_F32)


def _dot_nt(a, b):
    return lax.dot_general(a, b, (((1,), (1,)), ((), ())), preferred_element_type=_F32)


def _silu(g):
    h = 0.5 * g
    return h + h * jnp.tanh(h)


def _rotary(x, cos2, sin2):
    return x * cos2 + pltpu.roll(x, RET_DK // 2, axis=1) * sin2


def _inproj_kernel(x_ref, g_ref, w_ref, wff_ref, bf_ref, cos_ref, sin_ref,
                   z_ref, c_ref, u_scr, lf_scr, *, tm, tn, sub, n_valid):
    j = pl.program_id(1)
    n_sub = tm // sub

    def normalize(r):
        rows = pl.ds(r * sub, sub)
        xf = x_ref[rows, :]
        ms = jnp.mean(xf * xf, axis=-1, keepdims=True)
        u = (xf * lax.rsqrt(ms + EPS) * g_ref[...]).astype(_BF16)
        u_scr[rows, :] = u
        lf_scr[:, r * sub:(r + 1) * sub] = _dot_nt(wff_ref[...].astype(_BF16), u)

    def project(r, rot_cols):
        rows = pl.ds(r * sub, sub)
        acc = _dot_nt(u_scr[rows, :], w_ref[...])
        if rot_cols:
            cos2, sin2 = cos_ref[rows, :], sin_ref[rows, :]
            slabs = [_rotary(acc[:, a:a + RET_DK], cos2, sin2) for a in range(0, rot_cols, RET_DK)]
            acc = jnp.concatenate(slabs + [acc[:, rot_cols:]], axis=1) if rot_cols < tn else (
                jnp.concatenate(slabs, axis=1))
        z_ref[rows, :] = acc.astype(_BF16)

    def forget_cumsum():
        lane = lax.broadcasted_iota(jnp.int32, (FOX_HEADS, LANES), 1)
        carry = jnp.zeros((FOX_HEADS, 1), _F32)
        bf = bf_ref[...]
        for ci in range(tm // LANES):
            v = lf_scr[:, ci * LANES:(ci + 1) * LANES] + bf
            blk = jnp.minimum(v, 0.0) - jnp.log1p(jnp.exp(-jnp.abs(v)))
            if (ci + 1) * LANES > n_valid:
                blk = jnp.where(lane + ci * LANES < n_valid, blk, 0.0)
            sh = 1
            while sh < LANES:
                rolled = pltpu.roll(blk, sh, axis=1)
                blk = blk + jnp.where(lane >= sh, rolled, 0.0)
                sh *= 2
            blk = blk + carry
            c_ref[0, :, ci * LANES:(ci + 1) * LANES] = blk
            carry = blk[:, LANES - 1:LANES]

    rot_end = COL_RV * LANES
    rot_tiles = -(-rot_end // tn)

    for jj in range(rot_tiles):
        @pl.when(j == jj)
        def _(jj=jj):
            for r in range(n_sub):
                if jj == 0:
                    normalize(r)
                project(r, min(tn, rot_end - jj * tn))
            if jj == 0:
                forget_cumsum()

    @pl.when(j >= rot_tiles)
    def _():
        for r in range(n_sub):
            project(r, 0)


def _inproj(x2d, g, w_t, w_ff_t, b_f, cos2, sin2, *, tm, tn, n_valid):
    rows = x2d.shape[0]
    sub = min(tm, 512)
    assert D_MAIN % tn == 0 and tn % RET_DK == 0
    kern = functools.partial(_inproj_kernel, tm=tm, tn=tn, sub=sub, n_valid=n_valid)
    return pl.pallas_call(
        kern,
        grid=(rows // tm, D_MAIN // tn),
        in_specs=[
            pl.BlockSpec((tm, D_MODEL), lambda i, j: (i, 0)),
            pl.BlockSpec((1, D_MODEL), lambda i, j: (0, 0)),
            pl.BlockSpec((tn, D_MODEL), lambda i, j: (j, 0)),
            pl.BlockSpec((FOX_HEADS, D_MODEL), lambda i, j: (0, 0)),
            pl.BlockSpec((FOX_HEADS, 1), lambda i, j: (0, 0)),
            pl.BlockSpec((tm, RET_DK), lambda i, j: (0, 0)),
            pl.BlockSpec((tm, RET_DK), lambda i, j: (0, 0)),
        ],
        out_specs=[
            pl.BlockSpec((tm, tn), lambda i, j: (i, j)),
            pl.BlockSpec((1, FOX_HEADS, tm), lambda i, j: (i, 0, 0)),
        ],
        out_shape=[
            jax.ShapeDtypeStruct((rows, D_MAIN), _BF16),
            jax.ShapeDtypeStruct((rows // tm, FOX_HEADS, tm), _F32),
        ],
        scratch_shapes=[
            pltpu.VMEM((tm, D_MODEL), _BF16),
            pltpu.VMEM((FOX_HEADS, tm), _F32),
        ],
        compiler_params=pltpu.CompilerParams(
            dimension_semantics=("arbitrary", "arbitrary"),
            vmem_limit_bytes=VMEM_LIMIT),
        name="inproj",
    )(x2d, g, w_t, w_ff_t, b_f, cos2, sin2)


def _retention_kernel(lg_ref, q_ref, k_ref, v_ref, g_ref, km_ref, vm_ref, o_ref, *, n_chunks):
    h = pl.program_id(1)
    lg = lg_ref[h]
    ri = lax.broadcasted_iota(jnp.int32, (CHUNK, CHUNK), 0)
    ci = lax.broadcasted_iota(jnp.int32, (CHUNK, CHUNK), 1)
    diff = (ri - ci).astype(_F32)
    dmask = jnp.where(diff >= 0, jnp.exp(lg * jnp.maximum(diff, 0.0)), 0.0)
    idx = lax.broadcasted_iota(jnp.int32, (CHUNK, 1), 0).astype(_F32)
    xi = jnp.exp(lg * (idx + 1.0))
    zeta = jnp.exp(lg * (CHUNK - 1.0 - idx))
    chunk_decay = jnp.exp(lg * jnp.full((1, 1), float(CHUNK), _F32))

    zeta_m = jnp.exp(lg * (N_META - 1.0 - idx))
    km = km_ref[...].astype(_F32)
    state0 = _dot((km * zeta_m).T.astype(_BF16), vm_ref[...])

    def chunk_products(n):
        rows = slice(n * CHUNK, (n + 1) * CHUNK)
        kb = k_ref[0, rows, :]
        v = v_ref[0, rows, :]
        scores = (_dot_nt(q_ref[0, rows, :], kb) * dmask).astype(_BF16)
        kv = _dot((kb.astype(_F32) * zeta).T.astype(_BF16), v)
        return scores, kv

    def chunk_output(n, scores, state):
        rows = slice(n * CHUNK, (n + 1) * CHUNK)
        out = _dot(scores, v_ref[0, rows, :])
        out = out + _dot(q_ref[0, rows, :], state.astype(_BF16)) * xi
        y = out * lax.rsqrt(jnp.mean(out * out, axis=-1, keepdims=True) + EPS)
        gate = g_ref[0, rows, :].astype(_F32)
        o_ref[0, rows, :] = (y * _silu(gate)).astype(_BF16)

    state = state0
    scores, kv = chunk_products(0)
    for n in range(n_chunks):
        if n + 1 < n_chunks:
            nxt = chunk_products(n + 1)
        chunk_output(n, scores, state)
        state = state * chunk_decay + kv
        if n + 1 < n_chunks:
            scores, kv = nxt


def _retention(lg, z3, zm):
    b, s, _ = z3.shape
    kern = functools.partial(_retention_kernel, n_chunks=s // CHUNK)
    return pl.pallas_call(
        kern,
        grid=(b, RET_HEADS),
        in_specs=[
            pl.BlockSpec(memory_space=pltpu.SMEM),
            pl.BlockSpec((1, s, RET_DK), lambda i, h: (i, 0, COL_RQ + h)),
            pl.BlockSpec((1, s, RET_DK), lambda i, h: (i, 0, COL_RK + h)),
            pl.BlockSpec((1, s, RET_DV), lambda i, h: (i, 0, COL_RV // 2 + h)),
            pl.BlockSpec((1, s, RET_DV), lambda i, h: (i, 0, COL_RG // 2 + h)),
            pl.BlockSpec((CHUNK, RET_DK), lambda i, h: (0, COL_RK + h)),
            pl.BlockSpec((CHUNK, RET_DV), lambda i, h: (0, COL_RV // 2 + h)),
        ],
        out_specs=pl.BlockSpec((1, s, RET_DV), lambda i, h: (i, 0, h)),
        out_shape=jax.ShapeDtypeStruct((b, s, RET_HEADS * RET_DV), _BF16),
        compiler_params=pltpu.CompilerParams(
            dimension_semantics=("arbitrary", "arbitrary"),
            vmem_limit_bytes=VMEM_LIMIT),
        name="retention",
    )(lg, z3, z3, z3, z3, zm, zm)


def _fox_kernel(q_ref, k_ref, v_ref, g_ref, c_ref, km_ref, vm_ref, cm_ref, o_ref,
                vt_scr, vtm_scr, ka_scr, kam_scr, qt_scr, acc_scr, m_scr, s_scr, pm_scr,
                sm_scr, pmm_scr, *, n_tiles, n_pairs, seq):
    half = FOX_HEAD_DIM
    n_heads = 2 * n_pairs
    sub = lax.broadcasted_iota(jnp.int32, (LANES, 1), 0)
    sub8 = lax.broadcasted_iota(jnp.int32, (8, 1), 0)
    lane = lax.broadcasted_iota(jnp.int32, (1, LANES), 1)

    def pieces8(row):
        out = jnp.zeros((8, LANES), _F32)
        rest = row
        for piece in range(FOX_BIAS_PIECES):
            part = rest.astype(_BF16).astype(_F32)
            out = jnp.where(sub8 == piece, part, out)
            rest = rest - part
        return out

    def augmented_keys(kblk, bias_lo, bias_hi):
        gap = jnp.zeros((half - 8, LANES), _F32)
        tile = jnp.concatenate([pieces8(bias_hi), gap, pieces8(bias_lo), gap], axis=0).T
        tile = tile.astype(_BF16)[0:kblk.shape[0], :]
        return jnp.where(lane < half, kblk, tile), jnp.where(lane < half, tile, kblk)

    def build_meta_operands():
        for p in range(n_pairs):
            lanes = slice(p * LANES, (p + 1) * LANES)
            cm0, cm1 = cm_ref[p, 0:1, :], cm_ref[p, 1:2, :]
            kam_scr[2 * p], kam_scr[2 * p + 1] = augmented_keys(
                km_ref[0:N_META, lanes],
                (cm0[:, LANES - 1:LANES] - cm0) * LOG2E, (cm1[:, LANES - 1:LANES] - cm1) * LOG2E)
            vtm = vm_ref[:, lanes].T
            for hh in range(2):
                h = 2 * p + hh
                vtm_scr[h, 0:half, :] = vtm[hh * half:(hh + 1) * half, :]
                vtm_scr[h, half:FOX_ACC_ROWS, :] = jnp.ones((FOX_ACC_ROWS - half, LANES), _BF16)

    def build_block_operands(jb):
        rows = slice(jb * FOX_TK, (jb + 1) * FOX_TK)
        for p in range(n_pairs):
            lanes = slice(p * LANES, (p + 1) * LANES)
            for cb in range(jb * FOX_TK // LANES, (jb + 1) * FOX_TK // LANES):
                ks = slice(cb * LANES, (cb + 1) * LANES)
                ka_scr[2 * p, ks, :], ka_scr[2 * p + 1, ks, :] = augmented_keys(
                    k_ref[0, ks, lanes],
                    c_ref[0, p, 0:1, ks] * -LOG2E, c_ref[0, p, 1:2, ks] * -LOG2E)
            vt = v_ref[0, rows, lanes].T
            for hh in range(2):
                h = 2 * p + hh
                vt_scr[h, jb, 0:half, :] = vt[hh * half:(hh + 1) * half, :]
                vt_scr[h, jb, half:FOX_ACC_ROWS, :] = jnp.ones(
                    (FOX_ACC_ROWS - half, FOX_TK), _BF16)

    ri = lax.broadcasted_iota(jnp.int32, (FOX_TK, FOX_TQ), 0)
    ci = lax.broadcasted_iota(jnp.int32, (FOX_TK, FOX_TQ), 1)
    causal = ri <= ci

    def group_max(st):
        parts = [st[r:r + 8, :] for r in range(0, st.shape[0], 8)]
        while len(parts) > 1:
            nxt = [jnp.maximum(parts[a], parts[a + 1]) for a in range(0, len(parts) - 1, 2)]
            if len(parts) % 2:
                nxt.append(parts[-1])
            parts = nxt
        return parts[0]

    def update(h, tile, first, st, pm, vt):
        buf = tile % 2
        m_blk = jnp.max(pm, axis=0, keepdims=True)
        if first:
            m_new = m_blk
        else:
            m_old = m_scr[buf, h]
            m_new = jnp.maximum(m_old, m_blk)
            alpha = jnp.exp2(m_old - m_new)
        pt = jnp.exp2((st - m_new).astype(_BF16))
        if pt.shape[0] < LANES:
            pt = jnp.concatenate(
                [pt, jnp.zeros((LANES - pt.shape[0], FOX_TQ), _BF16)], axis=0)
        pv = _dot(vt, pt)
        acc_scr[buf, h] = pv if first else acc_scr[buf, h] * alpha + pv
        m_scr[buf, h] = m_new

    def prepare_queries(i):
        rows = slice(i * FOX_TQ, (i + 1) * FOX_TQ)
        for p in range(n_pairs):
            qt = q_ref[0, rows, p * LANES:(p + 1) * LANES].astype(_F32).T
            ones_hi = jnp.where(sub < half + FOX_BIAS_PIECES, 1.0, 0.0)
            ones_lo = jnp.where(sub < FOX_BIAS_PIECES, 1.0, 0.0)
            qt_scr[i % 2, 2 * p] = jnp.where(sub < half, qt, ones_hi).astype(_BF16)
            qt_scr[i % 2, 2 * p + 1] = jnp.where(sub < half, ones_lo, qt).astype(_BF16)

    def scores(item, heads):
        kind, i, j, slot = item
        if kind == "meta":
            for h in heads:
                st = _dot(kam_scr[h], qt_scr[i % 2, h])
                sm_scr[h] = st
                pmm_scr[h] = group_max(st)
            return
        krows = slice(j * FOX_TK, (j + 1) * FOX_TK)
        for h in heads:
            st = _dot(ka_scr[h, krows, :], qt_scr[i % 2, h])
            if i == j:
                st = jnp.where(causal, st, NEG_INF)
            s_scr[slot, h] = st
            pm_scr[slot, h] = group_max(st)

    def consume(item, heads):
        kind, i, j, slot = item
        for h in heads:
            if kind == "meta":
                update(h, i, True, sm_scr[h], pmm_scr[h], vtm_scr[h])
            else:
                update(h, i, False, s_scr[slot, h], pm_scr[slot, h], vt_scr[h, j])

    def finalize(i):
        rows = slice(i * FOX_TQ, (i + 1) * FOX_TQ)
        for p in range(n_pairs):
            lanes = slice(p * LANES, (p + 1) * LANES)
            outs = []
            for hh in range(2):
                acc = acc_scr[i % 2, 2 * p + hh]
                outs.append(acc[0:half, :] / acc[half:half + 1, :])
            y = jnp.concatenate(outs, axis=0).T
            gate = g_ref[0, rows, lanes].astype(_F32)
            o_ref[0, rows, lanes] = (y * _silu(gate)).astype(_BF16)

    items = []
    for i in range(n_tiles):
        items.append(("meta", i, None, None))
        for j in [i] + list(range(i)):
            items.append(("blk", i, j, sum(it[0] == "blk" for it in items) % 2))
    build_meta_operands()
    build_block_operands(0)
    prepare_queries(0)
    scores(items[0], range(n_heads))
    for t, item in enumerate(items):
        if t + 1 < len(items):
            nxt = items[t + 1]
            if nxt[0] == "meta":
                build_block_operands(nxt[1])
                prepare_queries(nxt[1])
            scores(nxt, range(n_heads))
        consume(item, range(n_heads))
        if t + 1 == len(items) or items[t + 1][1] != item[1]:
            finalize(item[1])


def _fox(z3, c4, zm, cm3):
    b, s, _ = z3.shape
    n_pairs = FOX_PAIRS
    n_groups = FOX_HEADS // 2 // n_pairs
    n_tiles = s // FOX_TQ
    w = n_pairs * LANES
    kern = functools.partial(_fox_kernel, n_tiles=n_tiles, n_pairs=n_pairs, seq=s)
    return pl.pallas_call(
        kern,
        grid=(b, n_groups),
        in_specs=[
            pl.BlockSpec((1, s, w), lambda i, p: (i, 0, COL_FQ // n_pairs + p)),
            pl.BlockSpec((1, s, w), lambda i, p: (i, 0, COL_FK // n_pairs + p)),
            pl.BlockSpec((1, s, w), lambda i, p: (i, 0, COL_FV // n_pairs + p)),
            pl.BlockSpec((1, s, w), lambda i, p: (i, 0, COL_FG // n_pairs + p)),
            pl.BlockSpec((1, n_pairs, 2, s), lambda i, p: (i, p, 0, 0)),
            pl.BlockSpec((CHUNK, w), lambda i, p: (0, COL_FK // n_pairs + p)),
            pl.BlockSpec((CHUNK, w), lambda i, p: (0, COL_FV // n_pairs + p)),
            pl.BlockSpec((n_pairs, 2, LANES), lambda i, p: (p, 0, 0)),
        ],
        out_specs=pl.BlockSpec((1, s, w), lambda i, p: (i, 0, p)),
        out_shape=jax.ShapeDtypeStruct((b, s, FOX_HEADS * FOX_HEAD_DIM), _BF16),
        scratch_shapes=[
            pltpu.VMEM((2 * n_pairs, n_tiles, FOX_ACC_ROWS, FOX_TK), _BF16),
            pltpu.VMEM((2 * n_pairs, FOX_ACC_ROWS, LANES), _BF16),
            pltpu.VMEM((2 * n_pairs, s, LANES), _BF16),
            pltpu.VMEM((2 * n_pairs, N_META, LANES), _BF16),
            pltpu.VMEM((2, 2 * n_pairs, LANES, FOX_TQ), _BF16),
            pltpu.VMEM((2, 2 * n_pairs, FOX_ACC_ROWS, FOX_TQ), _F32),
            pltpu.VMEM((2, 2 * n_pairs, 1, FOX_TQ), _F32),
            pltpu.VMEM((2, 2 * n_pairs, FOX_TK, FOX_TQ), _F32),
            pltpu.VMEM((2, 2 * n_pairs, 8, FOX_TQ), _F32),
            pltpu.VMEM((2 * n_pairs, N_META, FOX_TQ), _F32),
            pltpu.VMEM((2 * n_pairs, 8, FOX_TQ), _F32),
        ],
        compiler_params=pltpu.CompilerParams(
            dimension_semantics=("arbitrary", "arbitrary"),
            vmem_limit_bytes=VMEM_LIMIT),
        name="fox",
    )(z3, z3, z3, z3, c4, zm, zm, cm3)


def _outproj_kernel(yr_ref, yf_ref, x_ref, w_ref, g_ref, o_ref):
    half = w_ref.shape[0] // 2
    hres = x_ref[...] + _dot(yr_ref[...], w_ref[:half, :]) + _dot(yf_ref[...], w_ref[half:, :])
    ms = jnp.mean(hres * hres, axis=-1, keepdims=True)
    o_ref[...] = hres * lax.rsqrt(ms + EPS) * g_ref[...]


def _outproj(yr, yf, x2d, w_out, g, *, tm):
    rows = x2d.shape[0]
    return pl.pallas_call(
        _outproj_kernel,
        grid=(rows // tm,),
        in_specs=[
            pl.BlockSpec((tm, yr.shape[1]), lambda i: (i, 0)),
            pl.BlockSpec((tm, yf.shape[1]), lambda i: (i, 0)),
            pl.BlockSpec((tm, D_MODEL), lambda i: (i, 0)),
            pl.BlockSpec(w_out.shape, lambda i: (0, 0)),
            pl.BlockSpec((1, D_MODEL), lambda i: (0, 0)),
        ],
        out_specs=pl.BlockSpec((tm, D_MODEL), lambda i: (i, 0)),
        out_shape=jax.ShapeDtypeStruct((rows, D_MODEL), _F32),
        compiler_params=pltpu.CompilerParams(
            dimension_semantics=("arbitrary",),
            vmem_limit_bytes=VMEM_LIMIT),
        name="outproj",
    )(yr, yf, x2d, w_out, g)


def _rope_tables(pos):
    inv = ROPE_BASE ** (-jnp.arange(0, RET_DK, 2, dtype=_F32) / RET_DK)
    ang = pos[:, None] * inv[None, :]
    cos, sin = jnp.cos(ang), jnp.sin(ang)
    return jnp.concatenate([cos, cos], axis=-1), jnp.concatenate([-sin, sin], axis=-1)


def kernel(x, meta_tokens, norm_g, w_in, b_f, w_out, final_g):
    b, s, d = x.shape
    assert norm_g.shape[0] == 1 and d == D_MODEL and s % FOX_TQ == 0
    x2d = x.reshape(b * s, d)
    col_scale = jnp.ones((D_MAIN, 1), _F32)
    col_scale = col_scale.at[COL_RK * LANES:COL_RV * LANES].set(RET_DK ** -0.5)
    col_scale = col_scale.at[COL_FQ * LANES:COL_FK * LANES].set(FOX_HEAD_DIM ** -0.5 * LOG2E)
    w_all_t = jnp.swapaxes(w_in, 1, 2)[0]
    w_t = (w_all_t[:D_MAIN, :] * col_scale).astype(_BF16)
    w_ff_t = w_all_t[D_MAIN:, :]
    g = norm_g[0].reshape(1, d)
    bf = b_f[0].reshape(FOX_HEADS, 1)
    meta_pad = jnp.pad(meta_tokens.astype(_F32), ((0, CHUNK - N_META), (0, 0)))
    cos2, sin2 = _rope_tables(jnp.arange(s, dtype=_F32) + float(N_META))
    cosm, sinm = _rope_tables(jnp.arange(CHUNK, dtype=_F32))

    z, c = _inproj(x2d, g, w_t, w_ff_t, bf, cos2, sin2, tm=s, tn=1792, n_valid=s)
    zm, cm = _inproj(meta_pad, g, w_t, w_ff_t, bf, cosm, sinm,
                     tm=CHUNK, tn=3584, n_valid=N_META)

    z3 = z.reshape(b, s, D_MAIN)
    lg = jnp.log1p(-jnp.exp2(-5.0 - jnp.arange(RET_HEADS, dtype=_F32)))
    y_r = _retention(lg, z3, zm)

    c4 = c.reshape(b, FOX_HEADS // 2, 2, s)
    cm3 = cm.reshape(FOX_HEADS // 2, 2, CHUNK)
    y_f = _fox(z3, c4, zm, cm3)

    out = _outproj(y_r.reshape(b * s, -1), y_f.reshape(b * s, -1), x2d,
                   w_out[0].astype(_BF16), final_g.reshape(1, d), tm=1024)
    return out.reshape(b, s, d)
```

```python
import functools

import jax
import jax.numpy as jnp
from jax import lax
from jax.experimental import pallas as pl
from jax.experimental.pallas import tpu as pltpu

D_MODEL = 1024
N_META = 16
CHUNK = 128
RET_HEADS = 4
RET_DK = 128
RET_DV = 256
FOX_HEADS = 16
FOX_HEAD_DIM = 64
ROPE_BASE = 10000.0
EPS = 1e-6
NEG_INF = -1e30
LOG2E = 1.4426950408889634

LANES = 128
D_MAIN = 7168
COL_RQ, COL_RK, COL_RV, COL_RG = 0, 4, 8, 16
COL_FQ, COL_FK, COL_FV, COL_FG = 24, 32, 40, 48
FOX_TQ = 256
FOX_TK = 256
FOX_PAIRS = 2
FOX_ACC_ROWS = FOX_HEAD_DIM + 16
FOX_BIAS_PIECES = 3
VMEM_LIMIT = 56 * 1024 * 1024

_F32 = jnp.float32
_BF16 = jnp.bfloat16


def _dot(a, b):
    return jnp.dot(a, b, preferred_element_type=_F32)


def _dot_nt(a, b):
    return lax.dot_general(a, b, (((1,), (1,)), ((), ())), preferred_element_type=_F32)


def _silu(g):
    h = 0.5 * g
    return h + h * jnp.tanh(h)


def _rotary(x, cos2, sin2):
    return x * cos2 + pltpu.roll(x, RET_DK // 2, axis=1) * sin2


def _inproj_kernel(x_ref, g_ref, w_ref, wff_ref, bf_ref, cos_ref, sin_ref,
                   z_ref, c_ref, u_scr, lf_scr, *, tm, tn, sub, n_valid):
    j = pl.program_id(1)
    n_sub = tm // sub

    def normalize(r):
        rows = pl.ds(r * sub, sub)
        xf = x_ref[rows, :]
        ms = jnp.mean(xf * xf, axis=-1, keepdims=True)
        u = (xf * lax.rsqrt(ms + EPS) * g_ref[...]).astype(_BF16)
        u_scr[rows, :] = u
        lf_scr[:, r * sub:(r + 1) * sub] = _dot_nt(wff_ref[...].astype(_BF16), u)

    def project(r, rot_cols):
        rows = pl.ds(r * sub, sub)
        acc = _dot_nt(u_scr[rows, :], w_ref[...])
        if rot_cols:
            cos2, sin2 = cos_ref[rows, :], sin_ref[rows, :]
            slabs = [_rotary(acc[:, a:a + RET_DK], cos2, sin2) for a in range(0, rot_cols, RET_DK)]
            acc = jnp.concatenate(slabs + [acc[:, rot_cols:]], axis=1) if rot_cols < tn else (
                jnp.concatenate(slabs, axis=1))
        z_ref[rows, :] = acc.astype(_BF16)

    def forget_cumsum():
        lane = lax.broadcasted_iota(jnp.int32, (FOX_HEADS, LANES), 1)
        carry = jnp.zeros((FOX_HEADS, 1), _F32)
        bf = bf_ref[...]
        for ci in range(tm // LANES):
            v = lf_scr[:, ci * LANES:(ci + 1) * LANES] + bf
            blk = jnp.minimum(v, 0.0) - jnp.log1p(jnp.exp(-jnp.abs(v)))
            if (ci + 1) * LANES > n_valid:
                blk = jnp.where(lane + ci * LANES < n_valid, blk, 0.0)
            sh = 1
            while sh < LANES:
                rolled = pltpu.roll(blk, sh, axis=1)
                blk = blk + jnp.where(lane >= sh, rolled, 0.0)
                sh *= 2
            blk = blk + carry
            c_ref[0, :, ci * LANES:(ci + 1) * LANES] = blk
            carry = blk[:, LANES - 1:LANES]

    rot_end = COL_RV * LANES
    rot_tiles = -(-rot_end // tn)

    for jj in range(rot_tiles):
        @pl.when(j == jj)
        def _(jj=jj):
            for r in range(n_sub):
                if jj == 0:
                    normalize(r)
                project(r, min(tn, rot_end - jj * tn))
            if jj == 0:
                forget_cumsum()

    @pl.when(j >= rot_tiles)
    def _():
        for r in range(n_sub):
            project(r, 0)


def _inproj(x2d, g, w_t, w_ff_t, b_f, cos2, sin2, *, tm, tn, n_valid):
    rows = x2d.shape[0]
    sub = min(tm, 512)
    assert D_MAIN % tn == 0 and tn % RET_DK == 0
    kern = functools.partial(_inproj_kernel, tm=tm, tn=tn, sub=sub, n_valid=n_valid)
    return pl.pallas_call(
        kern,
        grid=(rows // tm, D_MAIN // tn),
        in_specs=[
            pl.BlockSpec((tm, D_MODEL), lambda i, j: (i, 0)),
            pl.BlockSpec((1, D_MODEL), lambda i, j: (0, 0)),
            pl.BlockSpec((tn, D_MODEL), lambda i, j: (j, 0)),
            pl.BlockSpec((FOX_HEADS, D_MODEL), lambda i, j: (0, 0)),
            pl.BlockSpec((FOX_HEADS, 1), lambda i, j: (0, 0)),
            pl.BlockSpec((tm, RET_DK), lambda i, j: (0, 0)),
            pl.BlockSpec((tm, RET_DK), lambda i, j: (0, 0)),
        ],
        out_specs=[
            pl.BlockSpec((tm, tn), lambda i, j: (i, j)),
            pl.BlockSpec((1, FOX_HEADS, tm), lambda i, j: (i, 0, 0)),
        ],
        out_shape=[
            jax.ShapeDtypeStruct((rows, D_MAIN), _BF16),
            jax.ShapeDtypeStruct((rows // tm, FOX_HEADS, tm), _F32),
        ],
        scratch_shapes=[
            pltpu.VMEM((tm, D_MODEL), _BF16),
            pltpu.VMEM((FOX_HEADS, tm), _F32),
        ],
        compiler_params=pltpu.CompilerParams(
            dimension_semantics=("arbitrary", "arbitrary"),
            vmem_limit_bytes=VMEM_LIMIT),
        name="inproj",
    )(x2d, g, w_t, w_ff_t, b_f, cos2, sin2)


def _retention_kernel(lg_ref, q_ref, k_ref, v_ref, km_ref, vm_ref, o_ref, *, n_chunks):
    h = pl.program_id(1)
    lg = lg_ref[h]
    ri = lax.broadcasted_iota(jnp.int32, (CHUNK, CHUNK), 0)
    ci = lax.broadcasted_iota(jnp.int32, (CHUNK, CHUNK), 1)
    diff = (ri - ci).astype(_F32)
    dmask = jnp.where(diff >= 0, jnp.exp(lg * jnp.maximum(diff, 0.0)), 0.0)
    idx = lax.broadcasted_iota(jnp.int32, (CHUNK, 1), 0).astype(_F32)
    xi = jnp.exp(lg * (idx + 1.0))
    zeta = jnp.exp(lg * (CHUNK - 1.0 - idx))
    chunk_decay = jnp.exp(lg * jnp.full((1, 1), float(CHUNK), _F32))

    zeta_m = jnp.exp(lg * (N_META - 1.0 - idx))
    km = km_ref[...].astype(_F32)
    state0 = _dot((km * zeta_m).T.astype(_BF16), vm_ref[...])

    def chunk_products(n):
        rows = slice(n * CHUNK, (n + 1) * CHUNK)
        kb = k_ref[0, rows, :]
        v = v_ref[0, rows, :]
        scores = (_dot_nt(q_ref[0, rows, :], kb) * dmask).astype(_BF16)
        kv = _dot((kb.astype(_F32) * zeta).T.astype(_BF16), v)
        return scores, kv

    def chunk_output(n, scores, state):
        rows = slice(n * CHUNK, (n + 1) * CHUNK)
        out = _dot(scores, v_ref[0, rows, :])
        out = out + _dot(q_ref[0, rows, :], state.astype(_BF16)) * xi
        y = out * lax.rsqrt(jnp.mean(out * out, axis=-1, keepdims=True) + EPS)
        o_ref[0, rows, :] = y.astype(_BF16)

    state = state0
    scores, kv = chunk_products(0)
    for n in range(n_chunks):
        if n + 1 < n_chunks:
            nxt = chunk_products(n + 1)
        chunk_output(n, scores, state)
        state = state * chunk_decay + kv
        if n + 1 < n_chunks:
            scores, kv = nxt


def _retention(lg, z3, zm):
    b, s, _ = z3.shape
    kern = functools.partial(_retention_kernel, n_chunks=s // CHUNK)
    return pl.pallas_call(
        kern,
        grid=(b, RET_HEADS),
        in_specs=[
            pl.BlockSpec(memory_space=pltpu.SMEM),
            pl.BlockSpec((1, s, RET_DK), lambda i, h: (i, 0, COL_RQ + h)),
            pl.BlockSpec((1, s, RET_DK), lambda i, h: (i, 0, COL_RK + h)),
            pl.BlockSpec((1, s, RET_DV), lambda i, h: (i, 0, COL_RV // 2 + h)),
            pl.BlockSpec((CHUNK, RET_DK), lambda i, h: (0, COL_RK + h)),
            pl.BlockSpec((CHUNK, RET_DV), lambda i, h: (0, COL_RV // 2 + h)),
        ],
        out_specs=pl.BlockSpec((1, s, RET_DV), lambda i, h: (i, 0, h)),
        out_shape=jax.ShapeDtypeStruct((b, s, RET_HEADS * RET_DV), _BF16),
        compiler_params=pltpu.CompilerParams(
            dimension_semantics=("arbitrary", "arbitrary"),
            vmem_limit_bytes=VMEM_LIMIT),
        name="retention",
    )(lg, z3, z3, z3, zm, zm)


def _fox_kernel(q_ref, k_ref, v_ref, c_ref, km_ref, vm_ref, cm_ref, o_ref,
                vt_scr, vtm_scr, ka_scr, kam_scr, qt_scr, acc_scr, m_scr, s_scr, pm_scr,
                sm_scr, pmm_scr, *, n_tiles, n_pairs, seq):
    half = FOX_HEAD_DIM
    n_heads = 2 * n_pairs
    sub = lax.broadcasted_iota(jnp.int32, (LANES, 1), 0)
    sub8 = lax.broadcasted_iota(jnp.int32, (8, 1), 0)
    lane = lax.broadcasted_iota(jnp.int32, (1, LANES), 1)

    def pieces8(row):
        out = jnp.zeros((8, LANES), _F32)
        rest = row
        for piece in range(FOX_BIAS_PIECES):
            part = rest.astype(_BF16).astype(_F32)
            out = jnp.where(sub8 == piece, part, out)
            rest = rest - part
        return out

    def augmented_keys(kblk, bias_lo, bias_hi):
        gap = jnp.zeros((half - 8, LANES), _F32)
        tile = jnp.concatenate([pieces8(bias_hi), gap, pieces8(bias_lo), gap], axis=0).T
        tile = tile.astype(_BF16)[0:kblk.shape[0], :]
        return jnp.where(lane < half, kblk, tile), jnp.where(lane < half, tile, kblk)

    def build_meta_operands():
        for p in range(n_pairs):
            lanes = slice(p * LANES, (p + 1) * LANES)
            cm0, cm1 = cm_ref[p, 0:1, :], cm_ref[p, 1:2, :]
            kam_scr[2 * p], kam_scr[2 * p + 1] = augmented_keys(
                km_ref[0:N_META, lanes],
                (cm0[:, LANES - 1:LANES] - cm0) * LOG2E, (cm1[:, LANES - 1:LANES] - cm1) * LOG2E)
            vtm = vm_ref[:, lanes].T
            for hh in range(2):
                h = 2 * p + hh
                vtm_scr[h, 0:half, :] = vtm[hh * half:(hh + 1) * half, :]
                vtm_scr[h, half:FOX_ACC_ROWS, :] = jnp.ones((FOX_ACC_ROWS - half, LANES), _BF16)

    def build_block_operands(jb):
        rows = slice(jb * FOX_TK, (jb + 1) * FOX_TK)
        for p in range(n_pairs):
            lanes = slice(p * LANES, (p + 1) * LANES)
            for cb in range(jb * FOX_TK // LANES, (jb + 1) * FOX_TK // LANES):
                ks = slice(cb * LANES, (cb + 1) * LANES)
                ka_scr[2 * p, ks, :], ka_scr[2 * p + 1, ks, :] = augmented_keys(
                    k_ref[0, ks, lanes],
                    c_ref[0, p, 0:1, ks] * -LOG2E, c_ref[0, p, 1:2, ks] * -LOG2E)
            vt = v_ref[0, rows, lanes].T
            for hh in range(2):
                h = 2 * p + hh
                vt_scr[h, jb, 0:half, :] = vt[hh * half:(hh + 1) * half, :]
                vt_scr[h, jb, half:FOX_ACC_ROWS, :] = jnp.ones(
                    (FOX_ACC_ROWS - half, FOX_TK), _BF16)

    ri = lax.broadcasted_iota(jnp.int32, (FOX_TK, FOX_TQ), 0)
    ci = lax.broadcasted_iota(jnp.int32, (FOX_TK, FOX_TQ), 1)
    causal = ri <= ci

    def group_max(st):
        parts = [st[r:r + 8, :] for r in range(0, st.shape[0], 8)]
        while len(parts) > 1:
            nxt = [jnp.maximum(parts[a], parts[a + 1]) for a in range(0, len(parts) - 1, 2)]
            if len(parts) % 2:
                nxt.append(parts[-1])
            parts = nxt
        return parts[0]

    def update(h, tile, first, st, pm, vt):
        buf = tile % 2
        m_blk = jnp.max(pm, axis=0, keepdims=True)
        if first:
            m_new = m_blk
        else:
            m_old = m_scr[buf, h]
            m_new = jnp.maximum(m_old, m_blk)
            alpha = jnp.exp2(m_old - m_new)
        pt = jnp.exp2(st - m_new).astype(_BF16)
        if pt.shape[0] < LANES:
            pt = jnp.concatenate(
                [pt, jnp.zeros((LANES - pt.shape[0], FOX_TQ), _BF16)], axis=0)
        pv = _dot(vt, pt)
        acc_scr[buf, h] = pv if first else acc_scr[buf, h] * alpha + pv
        m_scr[buf, h] = m_new

    def prepare_queries(i):
        rows = slice(i * FOX_TQ, (i + 1) * FOX_TQ)
        for p in range(n_pairs):
            qt = q_ref[0, rows, p * LANES:(p + 1) * LANES].astype(_F32).T
            ones_hi = jnp.where(sub < half + FOX_BIAS_PIECES, 1.0, 0.0)
            ones_lo = jnp.where(sub < FOX_BIAS_PIECES, 1.0, 0.0)
            qt_scr[i % 2, 2 * p] = jnp.where(sub < half, qt, ones_hi).astype(_BF16)
            qt_scr[i % 2, 2 * p + 1] = jnp.where(sub < half, ones_lo, qt).astype(_BF16)

    def scores(item, heads):
        kind, i, j, slot = item
        if kind == "meta":
            for h in heads:
                st = _dot(kam_scr[h], qt_scr[i % 2, h])
                sm_scr[h] = st
                pmm_scr[h] = group_max(st)
            return
        krows = slice(j * FOX_TK, (j + 1) * FOX_TK)
        for h in heads:
            st = _dot(ka_scr[h, krows, :], qt_scr[i % 2, h])
            if i == j:
                st = jnp.where(causal, st, NEG_INF)
            s_scr[slot, h] = st
            pm_scr[slot, h] = group_max(st)

    def consume(item, heads):
        kind, i, j, slot = item
        for h in heads:
            if kind == "meta":
                update(h, i, True, sm_scr[h], pmm_scr[h], vtm_scr[h])
            else:
                update(h, i, False, s_scr[slot, h], pm_scr[slot, h], vt_scr[h, j])

    def finalize(i):
        rows = slice(i * FOX_TQ, (i + 1) * FOX_TQ)
        for p in range(n_pairs):
            lanes = slice(p * LANES, (p + 1) * LANES)
            outs = []
            for hh in range(2):
                acc = acc_scr[i % 2, 2 * p + hh]
                outs.append(acc[0:half, :] / acc[half:half + 1, :])
            o_ref[0, rows, lanes] = jnp.concatenate(outs, axis=0).T.astype(_BF16)

    items = []
    for i in range(n_tiles):
        items.append(("meta", i, None, None))
        for j in [i] + list(range(i)):
            items.append(("blk", i, j, sum(it[0] == "blk" for it in items) % 2))
    build_meta_operands()
    build_block_operands(0)
    prepare_queries(0)
    scores(items[0], range(n_heads))
    for t, item in enumerate(items):
        if t + 1 < len(items):
            nxt = items[t + 1]
            if nxt[0] == "meta":
                build_block_operands(nxt[1])
                prepare_queries(nxt[1])
            scores(nxt, range(n_heads))
        consume(item, range(n_heads))
        if t + 1 == len(items) or items[t + 1][1] != item[1]:
            finalize(item[1])


def _fox(z3, c4, zm, cm3):
    b, s, _ = z3.shape
    n_pairs = FOX_PAIRS
    n_groups = FOX_HEADS // 2 // n_pairs
    n_tiles = s // FOX_TQ
    w = n_pairs * LANES
    kern = functools.partial(_fox_kernel, n_tiles=n_tiles, n_pairs=n_pairs, seq=s)
    return pl.pallas_call(
        kern,
        grid=(b, n_groups),
        in_specs=[
            pl.BlockSpec((1, s, w), lambda i, p: (i, 0, COL_FQ // n_pairs + p)),
            pl.BlockSpec((1, s, w), lambda i, p: (i, 0, COL_FK // n_pairs + p)),
            pl.BlockSpec((1, s, w), lambda i, p: (i, 0, COL_FV // n_pairs + p)),
            pl.BlockSpec((1, n_pairs, 2, s), lambda i, p: (i, p, 0, 0)),
            pl.BlockSpec((CHUNK, w), lambda i, p: (0, COL_FK // n_pairs + p)),
            pl.BlockSpec((CHUNK, w), lambda i, p: (0, COL_FV // n_pairs + p)),
            pl.BlockSpec((n_pairs, 2, LANES), lambda i, p: (p, 0, 0)),
        ],
        out_specs=pl.BlockSpec((1, s, w), lambda i, p: (i, 0, p)),
        out_shape=jax.ShapeDtypeStruct((b, s, FOX_HEADS * FOX_HEAD_DIM), _BF16),
        scratch_shapes=[
            pltpu.VMEM((2 * n_pairs, n_tiles, FOX_ACC_ROWS, FOX_TK), _BF16),
            pltpu.VMEM((2 * n_pairs, FOX_ACC_ROWS, LANES), _BF16),
            pltpu.VMEM((2 * n_pairs, s, LANES), _BF16),
            pltpu.VMEM((2 * n_pairs, N_META, LANES), _BF16),
            pltpu.VMEM((2, 2 * n_pairs, LANES, FOX_TQ), _BF16),
            pltpu.VMEM((2, 2 * n_pairs, FOX_ACC_ROWS, FOX_TQ), _F32),
            pltpu.VMEM((2, 2 * n_pairs, 1, FOX_TQ), _F32),
            pltpu.VMEM((2, 2 * n_pairs, FOX_TK, FOX_TQ), _F32),
            pltpu.VMEM((2, 2 * n_pairs, 8, FOX_TQ), _F32),
            pltpu.VMEM((2 * n_pairs, N_META, FOX_TQ), _F32),
            pltpu.VMEM((2 * n_pairs, 8, FOX_TQ), _F32),
        ],
        compiler_params=pltpu.CompilerParams(
            dimension_semantics=("arbitrary", "arbitrary"),
            vmem_limit_bytes=VMEM_LIMIT),
        name="fox",
    )(z3, z3, z3, c4, zm, zm, cm3)


def _outproj_kernel(yr_ref, yf_ref, gr_ref, gf_ref, x_ref, w_ref, g_ref, o_ref, *, tm, sub):
    half = w_ref.shape[0] // 2

    def gated(y_ref, gate_ref, rows):
        gate = _silu(gate_ref[rows, :].astype(_F32))
        return (y_ref[rows, :].astype(_F32) * gate).astype(_BF16)

    for r in range(tm // sub):
        rows = pl.ds(r * sub, sub)
        hres = (x_ref[rows, :] + _dot(gated(yr_ref, gr_ref, rows), w_ref[:half, :])
                + _dot(gated(yf_ref, gf_ref, rows), w_ref[half:, :]))
        ms = jnp.mean(hres * hres, axis=-1, keepdims=True)
        o_ref[rows, :] = hres * lax.rsqrt(ms + EPS) * g_ref[...]


def _outproj(yr, yf, z2d, x2d, w_out, g, *, tm):
    rows = x2d.shape[0]
    wr, wf = yr.shape[1], yf.shape[1]
    assert (COL_RG * LANES) % wr == 0 and (COL_FG * LANES) % wf == 0
    kern = functools.partial(_outproj_kernel, tm=tm, sub=256)
    return pl.pallas_call(
        kern,
        grid=(rows // tm,),
        in_specs=[
            pl.BlockSpec((tm, wr), lambda i: (i, 0)),
            pl.BlockSpec((tm, wf), lambda i: (i, 0)),
            pl.BlockSpec((tm, wr), lambda i: (i, COL_RG * LANES // wr)),
            pl.BlockSpec((tm, wf), lambda i: (i, COL_FG * LANES // wf)),
            pl.BlockSpec((tm, D_MODEL), lambda i: (i, 0)),
            pl.BlockSpec(w_out.shape, lambda i: (0, 0)),
            pl.BlockSpec((1, D_MODEL), lambda i: (0, 0)),
        ],
        out_specs=pl.BlockSpec((tm, D_MODEL), lambda i: (i, 0)),
        out_shape=jax.ShapeDtypeStruct((rows, D_MODEL), _F32),
        compiler_params=pltpu.CompilerParams(
            dimension_semantics=("arbitrary",),
            vmem_limit_bytes=VMEM_LIMIT),
        name="outproj",
    )(yr, yf, z2d, z2d, x2d, w_out, g)


def _rope_tables(pos):
    inv = ROPE_BASE ** (-jnp.arange(0, RET_DK, 2, dtype=_F32) / RET_DK)
    ang = pos[:, None] * inv[None, :]
    cos, sin = jnp.cos(ang), jnp.sin(ang)
    return jnp.concatenate([cos, cos], axis=-1), jnp.concatenate([-sin, sin], axis=-1)


def kernel(x, meta_tokens, norm_g, w_in, b_f, w_out, final_g):
    b, s, d = x.shape
    assert norm_g.shape[0] == 1 and d == D_MODEL and s % FOX_TQ == 0
    x2d = x.reshape(b * s, d)
    col_scale = jnp.ones((D_MAIN, 1), _F32)
    col_scale = col_scale.at[COL_RK * LANES:COL_RV * LANES].set(RET_DK ** -0.5)
    col_scale = col_scale.at[COL_FQ * LANES:COL_FK * LANES].set(FOX_HEAD_DIM ** -0.5 * LOG2E)
    w_all_t = jnp.swapaxes(w_in, 1, 2)[0]
    w_t = (w_all_t[:D_MAIN, :] * col_scale).astype(_BF16)
    w_ff_t = w_all_t[D_MAIN:, :]
    g = norm_g[0].reshape(1, d)
    bf = b_f[0].reshape(FOX_HEADS, 1)
    meta_pad = jnp.pad(meta_tokens.astype(_F32), ((0, CHUNK - N_META), (0, 0)))
    cos2, sin2 = _rope_tables(jnp.arange(s, dtype=_F32) + float(N_META))
    cosm, sinm = _rope_tables(jnp.arange(CHUNK, dtype=_F32))

    z, c = _inproj(x2d, g, w_t, w_ff_t, bf, cos2, sin2, tm=s, tn=1792, n_valid=s)
    zm, cm = _inproj(meta_pad, g, w_t, w_ff_t, bf, cosm, sinm,
                     tm=CHUNK, tn=3584, n_valid=N_META)

    z3 = z.reshape(b, s, D_MAIN)
    lg = jnp.log1p(-jnp.exp2(-5.0 - jnp.arange(RET_HEADS, dtype=_F32)))
    y_r = _retention(lg, z3, zm)

    c4 = c.reshape(b, FOX_HEADS // 2, 2, s)
    cm3 = cm.reshape(FOX_HEADS // 2, 2, CHUNK)
    y_f = _fox(z3, c4, zm, cm3)

    out = _outproj(y_r.reshape(b * s, -1), y_f.reshape(b * s, -1), z, x2d,
                   w_out[0].astype(_BF16), final_g.reshape(1, d), tm=1024)
    return out.reshape(b, s, d)
```

```python
import functools

import jax
import jax.numpy as jnp
from jax import lax
from jax.experimental import pallas as pl
from jax.experimental.pallas import tpu as pltpu

D_MODEL = 1024
N_META = 16
CHUNK = 128
RET_HEADS = 4
RET_DK = 128
RET_DV = 256
FOX_HEADS = 16
FOX_HEAD_DIM = 64
ROPE_BASE = 10000.0
EPS = 1e-6
NEG_INF = -1e30
LOG2E = 1.4426950408889634

LANES = 128
D_MAIN = 7168
COL_RQ, COL_RK, COL_RV, COL_RG = 0, 4, 8, 16
COL_FQ, COL_FK, COL_FV, COL_FG = 24, 32, 40, 48
FOX_TQ = 256
FOX_TK = 256
FOX_PAIRS = 2
FOX_ACC_ROWS = FOX_HEAD_DIM + 16
FOX_BIAS_PIECES = 3
VMEM_LIMIT = 56 * 1024 * 1024

_F32 = jnp.float32
_BF16 = jnp.bfloat16


def _dot(a, b):
    return jnp.dot(a, b, preferred_element_type=_F32)


def _dot_nt(a, b):
    return lax.dot_general(a, b, (((1,), (1,)), ((), ())), preferred_element_type=_F32)


def _silu(g):
    h = 0.5 * g
    return h + h * jnp.tanh(h)


def _rotary(x, cos2, sin2):
    return x * cos2 + pltpu.roll(x, RET_DK // 2, axis=1) * sin2


def _inproj_kernel(x_ref, g_ref, w_ref, wff_ref, bf_ref, cos_ref, sin_ref,
                   z_ref, c_ref, u_scr, lf_scr, *, tm, tn, sub, n_valid):
    j = pl.program_id(1)
    n_sub = tm // sub

    def normalize(r):
        rows = pl.ds(r * sub, sub)
        xf = x_ref[rows, :]
        ms = jnp.mean(xf * xf, axis=-1, keepdims=True)
        u = (xf * lax.rsqrt(ms + EPS) * g_ref[...]).astype(_BF16)
        u_scr[rows, :] = u
        lf_scr[:, r * sub:(r + 1) * sub] = _dot_nt(wff_ref[...].astype(_BF16), u)

    def project(r, rot_cols):
        rows = pl.ds(r * sub, sub)
        acc = _dot_nt(u_scr[rows, :], w_ref[...])
        if rot_cols:
            cos2, sin2 = cos_ref[rows, :], sin_ref[rows, :]
            slabs = [_rotary(acc[:, a:a + RET_DK], cos2, sin2) for a in range(0, rot_cols, RET_DK)]
            acc = jnp.concatenate(slabs + [acc[:, rot_cols:]], axis=1) if rot_cols < tn else (
                jnp.concatenate(slabs, axis=1))
        z_ref[rows, :] = acc.astype(_BF16)

    def forget_cumsum():
        lane = lax.broadcasted_iota(jnp.int32, (FOX_HEADS, LANES), 1)
        carry = jnp.zeros((FOX_HEADS, 1), _F32)
        bf = bf_ref[...]
        for ci in range(tm // LANES):
            v = lf_scr[:, ci * LANES:(ci + 1) * LANES] + bf
            blk = jnp.minimum(v, 0.0) - jnp.log1p(jnp.exp(-jnp.abs(v)))
            if (ci + 1) * LANES > n_valid:
                blk = jnp.where(lane + ci * LANES < n_valid, blk, 0.0)
            sh = 1
            while sh < LANES:
                rolled = pltpu.roll(blk, sh, axis=1)
                blk = blk + jnp.where(lane >= sh, rolled, 0.0)
                sh *= 2
            blk = blk + carry
            c_ref[0, :, ci * LANES:(ci + 1) * LANES] = blk
            carry = blk[:, LANES - 1:LANES]

    rot_end = COL_RV * LANES
    rot_tiles = -(-rot_end // tn)

    for jj in range(rot_tiles):
        @pl.when(j == jj)
        def _(jj=jj):
            for r in range(n_sub):
                if jj == 0:
                    normalize(r)
                project(r, min(tn, rot_end - jj * tn))
            if jj == 0:
                forget_cumsum()

    @pl.when(j >= rot_tiles)
    def _():
        for r in range(n_sub):
            project(r, 0)


def _inproj(x2d, g, w_t, w_ff_t, b_f, cos2, sin2, *, tm, tn, n_valid):
    rows = x2d.shape[0]
    sub = min(tm, 512)
    assert D_MAIN % tn == 0 and tn % RET_DK == 0
    kern = functools.partial(_inproj_kernel, tm=tm, tn=tn, sub=sub, n_valid=n_valid)
    return pl.pallas_call(
        kern,
        grid=(rows // tm, D_MAIN // tn),
        in_specs=[
            pl.BlockSpec((tm, D_MODEL), lambda i, j: (i, 0)),
            pl.BlockSpec((1, D_MODEL), lambda i, j: (0, 0)),
            pl.BlockSpec((tn, D_MODEL), lambda i, j: (j, 0)),
            pl.BlockSpec((FOX_HEADS, D_MODEL), lambda i, j: (0, 0)),
            pl.BlockSpec((FOX_HEADS, 1), lambda i, j: (0, 0)),
            pl.BlockSpec((tm, RET_DK), lambda i, j: (0, 0)),
            pl.BlockSpec((tm, RET_DK), lambda i, j: (0, 0)),
        ],
        out_specs=[
            pl.BlockSpec((tm, tn), lambda i, j: (i, j)),
            pl.BlockSpec((1, FOX_HEADS, tm), lambda i, j: (i, 0, 0)),
        ],
        out_shape=[
            jax.ShapeDtypeStruct((rows, D_MAIN), _BF16),
            jax.ShapeDtypeStruct((rows // tm, FOX_HEADS, tm), _F32),
        ],
        scratch_shapes=[
            pltpu.VMEM((tm, D_MODEL), _BF16),
            pltpu.VMEM((FOX_HEADS, tm), _F32),
        ],
        compiler_params=pltpu.CompilerParams(
            dimension_semantics=("arbitrary", "arbitrary"),
            vmem_limit_bytes=VMEM_LIMIT),
        name="inproj",
    )(x2d, g, w_t, w_ff_t, b_f, cos2, sin2)


def _retention_kernel(lg_ref, q_ref, k_ref, v_ref, g_ref, km_ref, vm_ref, o_ref, *, n_chunks):
    h = pl.program_id(1)
    lg = lg_ref[h]
    ri = lax.broadcasted_iota(jnp.int32, (CHUNK, CHUNK), 0)
    ci = lax.broadcasted_iota(jnp.int32, (CHUNK, CHUNK), 1)
    diff = (ri - ci).astype(_F32)
    dmask = jnp.where(diff >= 0, jnp.exp(lg * jnp.maximum(diff, 0.0)), 0.0)
    idx = lax.broadcasted_iota(jnp.int32, (CHUNK, 1), 0).astype(_F32)
    xi = jnp.exp(lg * (idx + 1.0))
    zeta = jnp.exp(lg * (CHUNK - 1.0 - idx))
    chunk_decay = jnp.exp(lg * jnp.full((1, 1), float(CHUNK), _F32))

    zeta_m = jnp.exp(lg * (N_META - 1.0 - idx))
    km = km_ref[...].astype(_F32)
    state0 = _dot((km * zeta_m).T.astype(_BF16), vm_ref[...])

    def chunk_products(n):
        rows = slice(n * CHUNK, (n + 1) * CHUNK)
        kb = k_ref[0, rows, :]
        v = v_ref[0, rows, :]
        scores = (_dot_nt(q_ref[0, rows, :], kb) * dmask).astype(_BF16)
        kv = _dot((kb.astype(_F32) * zeta).T.astype(_BF16), v)
        return scores, kv

    def chunk_output(n, scores, state):
        rows = slice(n * CHUNK, (n + 1) * CHUNK)
        q_decayed = (q_ref[0, rows, :].astype(_F32) * xi).astype(_BF16)
        lhs = jnp.concatenate([scores, q_decayed], axis=1)
        rhs = jnp.concatenate([v_ref[0, rows, :], state.astype(_BF16)], axis=0)
        out = _dot(lhs, rhs)
        y = out * lax.rsqrt(jnp.mean(out * out, axis=-1, keepdims=True) + EPS)
        gate = g_ref[0, rows, :].astype(_F32)
        o_ref[0, rows, :] = (y * _silu(gate)).astype(_BF16)

    state = state0
    scores, kv = chunk_products(0)
    for n in range(n_chunks):
        if n + 1 < n_chunks:
            nxt = chunk_products(n + 1)
        chunk_output(n, scores, state)
        state = state * chunk_decay + kv
        if n + 1 < n_chunks:
            scores, kv = nxt


def _retention(lg, z3, zm):
    b, s, _ = z3.shape
    kern = functools.partial(_retention_kernel, n_chunks=s // CHUNK)
    return pl.pallas_call(
        kern,
        grid=(b, RET_HEADS),
        in_specs=[
            pl.BlockSpec(memory_space=pltpu.SMEM),
            pl.BlockSpec((1, s, RET_DK), lambda i, h: (i, 0, COL_RQ + h)),
            pl.BlockSpec((1, s, RET_DK), lambda i, h: (i, 0, COL_RK + h)),
            pl.BlockSpec((1, s, RET_DV), lambda i, h: (i, 0, COL_RV // 2 + h)),
            pl.BlockSpec((1, s, RET_DV), lambda i, h: (i, 0, COL_RG // 2 + h)),
            pl.BlockSpec((CHUNK, RET_DK), lambda i, h: (0, COL_RK + h)),
            pl.BlockSpec((CHUNK, RET_DV), lambda i, h: (0, COL_RV // 2 + h)),
        ],
        out_specs=pl.BlockSpec((1, s, RET_DV), lambda i, h: (i, 0, h)),
        out_shape=jax.ShapeDtypeStruct((b, s, RET_HEADS * RET_DV), _BF16),
        compiler_params=pltpu.CompilerParams(
            dimension_semantics=("arbitrary", "arbitrary"),
            vmem_limit_bytes=VMEM_LIMIT),
        name="retention",
    )(lg, z3, z3, z3, z3, zm, zm)


def _fox_kernel(q_ref, k_ref, v_ref, g_ref, c_ref, km_ref, vm_ref, cm_ref, o_ref,
                vt_scr, vtm_scr, ka_scr, kam_scr, qt_scr, acc_scr, m_scr, s_scr, pm_scr,
                sm_scr, pmm_scr, *, n_tiles, n_pairs, seq):
    half = FOX_HEAD_DIM
    n_heads = 2 * n_pairs
    sub = lax.broadcasted_iota(jnp.int32, (LANES, 1), 0)
    sub8 = lax.broadcasted_iota(jnp.int32, (8, 1), 0)
    lane = lax.broadcasted_iota(jnp.int32, (1, LANES), 1)

    def pieces8(row):
        out = jnp.zeros((8, LANES), _F32)
        rest = row
        for piece in range(FOX_BIAS_PIECES):
            part = rest.astype(_BF16).astype(_F32)
            out = jnp.where(sub8 == piece, part, out)
            rest = rest - part
        return out

    def augmented_keys(kblk, bias_lo, bias_hi):
        gap = jnp.zeros((half - 8, LANES), _F32)
        tile = jnp.concatenate([pieces8(bias_hi), gap, pieces8(bias_lo), gap], axis=0).T
        tile = tile.astype(_BF16)[0:kblk.shape[0], :]
        return jnp.where(lane < half, kblk, tile), jnp.where(lane < half, tile, kblk)

    def build_meta_operands():
        for p in range(n_pairs):
            lanes = slice(p * LANES, (p + 1) * LANES)
            cm0, cm1 = cm_ref[p, 0:1, :], cm_ref[p, 1:2, :]
            kam_scr[2 * p], kam_scr[2 * p + 1] = augmented_keys(
                km_ref[0:N_META, lanes],
                (cm0[:, LANES - 1:LANES] - cm0) * LOG2E, (cm1[:, LANES - 1:LANES] - cm1) * LOG2E)
            vtm = vm_ref[:, lanes].T
            for hh in range(2):
                h = 2 * p + hh
                vtm_scr[h, 0:half, :] = vtm[hh * half:(hh + 1) * half, :]
                vtm_scr[h, half:FOX_ACC_ROWS, :] = jnp.ones((FOX_ACC_ROWS - half, LANES), _BF16)

    def build_block_operands(jb):
        rows = slice(jb * FOX_TK, (jb + 1) * FOX_TK)
        for p in range(n_pairs):
            lanes = slice(p * LANES, (p + 1) * LANES)
            for cb in range(jb * FOX_TK // LANES, (jb + 1) * FOX_TK // LANES):
                ks = slice(cb * LANES, (cb + 1) * LANES)
                ka_scr[2 * p, ks, :], ka_scr[2 * p + 1, ks, :] = augmented_keys(
                    k_ref[0, ks, lanes],
                    c_ref[0, p, 0:1, ks] * -LOG2E, c_ref[0, p, 1:2, ks] * -LOG2E)
            vt = v_ref[0, rows, lanes].T
            for hh in range(2):
                h = 2 * p + hh
                vt_scr[h, jb, 0:half, :] = vt[hh * half:(hh + 1) * half, :]
                vt_scr[h, jb, half:FOX_ACC_ROWS, :] = jnp.ones(
                    (FOX_ACC_ROWS - half, FOX_TK), _BF16)

    ri = lax.broadcasted_iota(jnp.int32, (FOX_TK, FOX_TQ), 0)
    ci = lax.broadcasted_iota(jnp.int32, (FOX_TK, FOX_TQ), 1)
    causal = ri <= ci

    def group_max(st):
        parts = [st[r:r + 8, :] for r in range(0, st.shape[0], 8)]
        while len(parts) > 1:
            nxt = [jnp.maximum(parts[a], parts[a + 1]) for a in range(0, len(parts) - 1, 2)]
            if len(parts) % 2:
                nxt.append(parts[-1])
            parts = nxt
        return parts[0]

    def update(h, tile, first, st, pm, vt):
        buf = tile % 2
        m_blk = jnp.max(pm, axis=0, keepdims=True)
        if first:
            m_new = m_blk
        else:
            m_old = m_scr[buf, h]
            m_new = jnp.maximum(m_old, m_blk)
            alpha = jnp.exp2(m_old - m_new)
        pt = jnp.exp2(st - m_new).astype(_BF16)
        if pt.shape[0] < LANES:
            pt = jnp.concatenate(
                [pt, jnp.zeros((LANES - pt.shape[0], FOX_TQ), _BF16)], axis=0)
        pv = _dot(vt, pt)
        acc_scr[buf, h] = pv if first else acc_scr[buf, h] * alpha + pv
        m_scr[buf, h] = m_new

    def prepare_queries(i):
        rows = slice(i * FOX_TQ, (i + 1) * FOX_TQ)
        for p in range(n_pairs):
            qt = q_ref[0, rows, p * LANES:(p + 1) * LANES].astype(_F32).T
            ones_hi = jnp.where(sub < half + FOX_BIAS_PIECES, 1.0, 0.0)
            ones_lo = jnp.where(sub < FOX_BIAS_PIECES, 1.0, 0.0)
            qt_scr[i % 2, 2 * p] = jnp.where(sub < half, qt, ones_hi).astype(_BF16)
            qt_scr[i % 2, 2 * p + 1] = jnp.where(sub < half, ones_lo, qt).astype(_BF16)

    def scores(item, heads):
        kind, i, j, slot = item
        if kind == "meta":
            for h in heads:
                st = _dot(kam_scr[h], qt_scr[i % 2, h])
                sm_scr[h] = st
                pmm_scr[h] = group_max(st)
            return
        krows = slice(j * FOX_TK, (j + 1) * FOX_TK)
        for h in heads:
            st = _dot(ka_scr[h, krows, :], qt_scr[i % 2, h])
            if i == j:
                st = jnp.where(causal, st, NEG_INF)
            s_scr[slot, h] = st
            pm_scr[slot, h] = group_max(st)

    def consume(item, heads):
        kind, i, j, slot = item
        for h in heads:
            if kind == "meta":
                update(h, i, True, sm_scr[h], pmm_scr[h], vtm_scr[h])
            else:
                update(h, i, False, s_scr[slot, h], pm_scr[slot, h], vt_scr[h, j])

    def finalize(i):
        rows = slice(i * FOX_TQ, (i + 1) * FOX_TQ)
        for p in range(n_pairs):
            lanes = slice(p * LANES, (p + 1) * LANES)
            outs = []
            for hh in range(2):
                acc = acc_scr[i % 2, 2 * p + hh]
                outs.append(acc[0:half, :] / acc[half:half + 1, :])
            y = jnp.concatenate(outs, axis=0).T
            gate = g_ref[0, rows, lanes].astype(_F32)
            o_ref[0, rows, lanes] = (y * _silu(gate)).astype(_BF16)

    items = []
    for i in range(n_tiles):
        items.append(("meta", i, None, None))
        for j in [i] + list(range(i)):
            items.append(("blk", i, j, sum(it[0] == "blk" for it in items) % 2))
    build_meta_operands()
    build_block_operands(0)
    prepare_queries(0)
    scores(items[0], range(n_heads))
    for t, item in enumerate(items):
        if t + 1 < len(items):
            nxt = items[t + 1]
            if nxt[0] == "meta":
                build_block_operands(nxt[1])
                prepare_queries(nxt[1])
            scores(nxt, range(n_heads))
        consume(item, range(n_heads))
        if t + 1 == len(items) or items[t + 1][1] != item[1]:
            finalize(item[1])


def _fox(z3, c4, zm, cm3):
    b, s, _ = z3.shape
    n_pairs = FOX_PAIRS
    n_groups = FOX_HEADS // 2 // n_pairs
    n_tiles = s // FOX_TQ
    w = n_pairs * LANES
    kern = functools.partial(_fox_kernel, n_tiles=n_tiles, n_pairs=n_pairs, seq=s)
    return pl.pallas_call(
        kern,
        grid=(b, n_groups),
        in_specs=[
            pl.BlockSpec((1, s, w), lambda i, p: (i, 0, COL_FQ // n_pairs + p)),
            pl.BlockSpec((1, s, w), lambda i, p: (i, 0, COL_FK // n_pairs + p)),
            pl.BlockSpec((1, s, w), lambda i, p: (i, 0, COL_FV // n_pairs + p)),
            pl.BlockSpec((1, s, w), lambda i, p: (i, 0, COL_FG // n_pairs + p)),
            pl.BlockSpec((1, n_pairs, 2, s), lambda i, p: (i, p, 0, 0)),
            pl.BlockSpec((CHUNK, w), lambda i, p: (0, COL_FK // n_pairs + p)),
            pl.BlockSpec((CHUNK, w), lambda i, p: (0, COL_FV // n_pairs + p)),
            pl.BlockSpec((n_pairs, 2, LANES), lambda i, p: (p, 0, 0)),
        ],
        out_specs=pl.BlockSpec((1, s, w), lambda i, p: (i, 0, p)),
        out_shape=jax.ShapeDtypeStruct((b, s, FOX_HEADS * FOX_HEAD_DIM), _BF16),
        scratch_shapes=[
            pltpu.VMEM((2 * n_pairs, n_tiles, FOX_ACC_ROWS, FOX_TK), _BF16),
            pltpu.VMEM((2 * n_pairs, FOX_ACC_ROWS, LANES), _BF16),
            pltpu.VMEM((2 * n_pairs, s, LANES), _BF16),
            pltpu.VMEM((2 * n_pairs, N_META, LANES), _BF16),
            pltpu.VMEM((2, 2 * n_pairs, LANES, FOX_TQ), _BF16),
            pltpu.VMEM((2, 2 * n_pairs, FOX_ACC_ROWS, FOX_TQ), _F32),
            pltpu.VMEM((2, 2 * n_pairs, 1, FOX_TQ), _F32),
            pltpu.VMEM((2, 2 * n_pairs, FOX_TK, FOX_TQ), _F32),
            pltpu.VMEM((2, 2 * n_pairs, 8, FOX_TQ), _F32),
            pltpu.VMEM((2 * n_pairs, N_META, FOX_TQ), _F32),
            pltpu.VMEM((2 * n_pairs, 8, FOX_TQ), _F32),
        ],
        compiler_params=pltpu.CompilerParams(
            dimension_semantics=("arbitrary", "arbitrary"),
            vmem_limit_bytes=VMEM_LIMIT),
        name="fox",
    )(z3, z3, z3, z3, c4, zm, zm, cm3)


def _outproj_kernel(yr_ref, yf_ref, x_ref, w_ref, g_ref, o_ref):
    half = w_ref.shape[0] // 2
    hres = x_ref[...] + _dot(yr_ref[...], w_ref[:half, :]) + _dot(yf_ref[...], w_ref[half:, :])
    ms = jnp.mean(hres * hres, axis=-1, keepdims=True)
    o_ref[...] = hres * lax.rsqrt(ms + EPS) * g_ref[...]


def _outproj(yr, yf, x2d, w_out, g, *, tm):
    rows = x2d.shape[0]
    return pl.pallas_call(
        _outproj_kernel,
        grid=(rows // tm,),
        in_specs=[
            pl.BlockSpec((tm, yr.shape[1]), lambda i: (i, 0)),
            pl.BlockSpec((tm, yf.shape[1]), lambda i: (i, 0)),
            pl.BlockSpec((tm, D_MODEL), lambda i: (i, 0)),
            pl.BlockSpec(w_out.shape, lambda i: (0, 0)),
            pl.BlockSpec((1, D_MODEL), lambda i: (0, 0)),
        ],
        out_specs=pl.BlockSpec((tm, D_MODEL), lambda i: (i, 0)),
        out_shape=jax.ShapeDtypeStruct((rows, D_MODEL), _F32),
        compiler_params=pltpu.CompilerParams(
            dimension_semantics=("arbitrary",),
            vmem_limit_bytes=VMEM_LIMIT),
        name="outproj",
    )(yr, yf, x2d, w_out, g)


def _rope_tables(pos):
    inv = ROPE_BASE ** (-jnp.arange(0, RET_DK, 2, dtype=_F32) / RET_DK)
    ang = pos[:, None] * inv[None, :]
    cos, sin = jnp.cos(ang), jnp.sin(ang)
    return jnp.concatenate([cos, cos], axis=-1), jnp.concatenate([-sin, sin], axis=-1)


def kernel(x, meta_tokens, norm_g, w_in, b_f, w_out, final_g):
    b, s, d = x.shape
    assert norm_g.shape[0] == 1 and d == D_MODEL and s % FOX_TQ == 0
    x2d = x.reshape(b * s, d)
    col_scale = jnp.ones((D_MAIN, 1), _F32)
    col_scale = col_scale.at[COL_RK * LANES:COL_RV * LANES].set(RET_DK ** -0.5)
    col_scale = col_scale.at[COL_FQ * LANES:COL_FK * LANES].set(FOX_HEAD_DIM ** -0.5 * LOG2E)
    w_all_t = jnp.swapaxes(w_in, 1, 2)[0]
    w_t = (w_all_t[:D_MAIN, :] * col_scale).astype(_BF16)
    w_ff_t = w_all_t[D_MAIN:, :]
    g = norm_g[0].reshape(1, d)
    bf = b_f[0].reshape(FOX_HEADS, 1)
    meta_pad = jnp.pad(meta_tokens.astype(_F32), ((0, CHUNK - N_META), (0, 0)))
    cos2, sin2 = _rope_tables(jnp.arange(s, dtype=_F32) + float(N_META))
    cosm, sinm = _rope_tables(jnp.arange(CHUNK, dtype=_F32))

    z, c = _inproj(x2d, g, w_t, w_ff_t, bf, cos2, sin2, tm=s, tn=1792, n_valid=s)
    zm, cm = _inproj(meta_pad, g, w_t, w_ff_t, bf, cosm, sinm,
                     tm=CHUNK, tn=3584, n_valid=N_META)

    z3 = z.reshape(b, s, D_MAIN)
    lg = jnp.log1p(-jnp.exp2(-5.0 - jnp.arange(RET_HEADS, dtype=_F32)))
    y_r = _retention(lg, z3, zm)

    c4 = c.reshape(b, FOX_HEADS // 2, 2, s)
    cm3 = cm.reshape(FOX_HEADS // 2, 2, CHUNK)
    y_f = _fox(z3, c4, zm, cm3)

    out = _outproj(y_r.reshape(b * s, -1), y_f.reshape(b * s, -1), x2d,
                   w_out[0].astype(_BF16), final_g.reshape(1, d), tm=1024)
    return out.reshape(b, s, d)
```

```python
import functools

import jax
import jax.numpy as jnp
from jax import lax
from jax.experimental import pallas as pl
from jax.experimental.pallas import tpu as pltpu

D_MODEL = 1024
N_META = 16
CHUNK = 128
RET_HEADS = 4
RET_DK = 128
RET_DV = 256
FOX_HEADS = 16
FOX_HEAD_DIM = 64
ROPE_BASE = 10000.0
EPS = 1e-6
NEG_INF = -1e30
LOG2E = 1.4426950408889634

LANES = 128
SUBLANES = 8
BF16_SUBLANES = 16
D_MAIN = 7168
COL_RQ, COL_RK, COL_RV, COL_RG = 0, 4, 8, 16
COL_FQ, COL_FK, COL_FV, COL_FG = 24, 32, 40, 48
FOX_TQ = 256
FOX_TK = 256
FOX_PAIRS = 2
FOX_ACC_ROWS = FOX_HEAD_DIM + BF16_SUBLANES
FOX_BIAS_PIECES = 3
PROJ_SUB = 512
OUT_SUB = 256
VMEM_LIMIT = 56 * 1024 * 1024

_F32 = jnp.float32
_BF16 = jnp.bfloat16


def _dot(a, b):
    return jnp.dot(a, b, preferred_element_type=_F32)


def _dot_nt(a, b):
    return lax.dot_general(a, b, (((1,), (1,)), ((), ())), preferred_element_type=_F32)


def _silu(g):
    h = 0.5 * g
    return h + h * jnp.tanh(h)


def _rotary(x, cos2, sin2):
    return x * cos2 + pltpu.roll(x, RET_DK // 2, axis=1) * sin2


def _inproj_kernel(x_ref, g_ref, w_ref, wff_ref, bf_ref, cos_ref, sin_ref,
                   z_ref, c_ref, u_scr, lf_scr, *, tm, tn, sub, n_valid):
    j = pl.program_id(1)
    n_sub = tm // sub

    def normalize(r):
        rows = pl.ds(r * sub, sub)
        xf = x_ref[rows, :]
        ms = jnp.mean(xf * xf, axis=-1, keepdims=True)
        u = (xf * lax.rsqrt(ms + EPS) * g_ref[...]).astype(_BF16)
        u_scr[rows, :] = u
        lf_scr[:, r * sub:(r + 1) * sub] = _dot_nt(wff_ref[...].astype(_BF16), u)

    def project(r, rot_cols):
        rows = pl.ds(r * sub, sub)
        acc = _dot_nt(u_scr[rows, :], w_ref[...])
        if rot_cols:
            cos2, sin2 = cos_ref[rows, :], sin_ref[rows, :]
            slabs = [_rotary(acc[:, a:a + RET_DK], cos2, sin2) for a in range(0, rot_cols, RET_DK)]
            acc = jnp.concatenate(slabs + [acc[:, rot_cols:]], axis=1) if rot_cols < tn else (
                jnp.concatenate(slabs, axis=1))
        z_ref[rows, :] = acc.astype(_BF16)

    def forget_cumsum():
        lane = lax.broadcasted_iota(jnp.int32, (FOX_HEADS, LANES), 1)
        carry = jnp.zeros((FOX_HEADS, 1), _F32)
        bf = bf_ref[...]
        for ci in range(tm // LANES):
            v = lf_scr[:, ci * LANES:(ci + 1) * LANES] + bf
            blk = jnp.minimum(v, 0.0) - jnp.log1p(jnp.exp(-jnp.abs(v)))
            if (ci + 1) * LANES > n_valid:
                blk = jnp.where(lane + ci * LANES < n_valid, blk, 0.0)
            sh = 1
            while sh < LANES:
                rolled = pltpu.roll(blk, sh, axis=1)
                blk = blk + jnp.where(lane >= sh, rolled, 0.0)
                sh *= 2
            blk = blk + carry
            c_ref[0, :, ci * LANES:(ci + 1) * LANES] = blk
            carry = blk[:, LANES - 1:LANES]

    rot_end = COL_RV * LANES
    rot_tiles = -(-rot_end // tn)

    for jj in range(rot_tiles):
        @pl.when(j == jj)
        def _(jj=jj):
            for r in range(n_sub):
                if jj == 0:
                    normalize(r)
                project(r, min(tn, rot_end - jj * tn))
            if jj == 0:
                forget_cumsum()

    @pl.when(j >= rot_tiles)
    def _():
        for r in range(n_sub):
            project(r, 0)


def _inproj(x2d, g, w_t, w_ff_t, b_f, cos2, sin2, *, tm, tn, n_valid):
    rows = x2d.shape[0]
    sub = min(tm, PROJ_SUB)
    assert D_MAIN % tn == 0 and tn % RET_DK == 0
    kern = functools.partial(_inproj_kernel, tm=tm, tn=tn, sub=sub, n_valid=n_valid)
    return pl.pallas_call(
        kern,
        grid=(rows // tm, D_MAIN // tn),
        in_specs=[
            pl.BlockSpec((tm, D_MODEL), lambda i, j: (i, 0)),
            pl.BlockSpec((1, D_MODEL), lambda i, j: (0, 0)),
            pl.BlockSpec((tn, D_MODEL), lambda i, j: (j, 0)),
            pl.BlockSpec((FOX_HEADS, D_MODEL), lambda i, j: (0, 0)),
            pl.BlockSpec((FOX_HEADS, 1), lambda i, j: (0, 0)),
            pl.BlockSpec((tm, RET_DK), lambda i, j: (0, 0)),
            pl.BlockSpec((tm, RET_DK), lambda i, j: (0, 0)),
        ],
        out_specs=[
            pl.BlockSpec((tm, tn), lambda i, j: (i, j)),
            pl.BlockSpec((1, FOX_HEADS, tm), lambda i, j: (i, 0, 0)),
        ],
        out_shape=[
            jax.ShapeDtypeStruct((rows, D_MAIN), _BF16),
            jax.ShapeDtypeStruct((rows // tm, FOX_HEADS, tm), _F32),
        ],
        scratch_shapes=[
            pltpu.VMEM((tm, D_MODEL), _BF16),
            pltpu.VMEM((FOX_HEADS, tm), _F32),
        ],
        compiler_params=pltpu.CompilerParams(
            dimension_semantics=("arbitrary", "arbitrary"),
            vmem_limit_bytes=VMEM_LIMIT),
        name="inproj",
    )(x2d, g, w_t, w_ff_t, b_f, cos2, sin2)


def _retention_kernel(lg_ref, q_ref, k_ref, v_ref, g_ref, km_ref, vm_ref, o_ref, *, n_chunks):
    h = pl.program_id(1)
    lg = lg_ref[h]
    ri = lax.broadcasted_iota(jnp.int32, (CHUNK, CHUNK), 0)
    ci = lax.broadcasted_iota(jnp.int32, (CHUNK, CHUNK), 1)
    diff = (ri - ci).astype(_F32)
    dmask = jnp.where(diff >= 0, jnp.exp(lg * jnp.maximum(diff, 0.0)), 0.0)
    idx = lax.broadcasted_iota(jnp.int32, (CHUNK, 1), 0).astype(_F32)
    xi = jnp.exp(lg * (idx + 1.0))
    zeta = jnp.exp(lg * (CHUNK - 1.0 - idx))
    chunk_decay = jnp.exp(lg * jnp.full((1, 1), float(CHUNK), _F32))

    zeta_m = jnp.exp(lg * (N_META - 1.0 - idx))
    km = km_ref[...].astype(_F32)
    state0 = _dot((km * zeta_m).T.astype(_BF16), vm_ref[...])

    def chunk_products(n):
        rows = slice(n * CHUNK, (n + 1) * CHUNK)
        kb = k_ref[0, rows, :]
        v = v_ref[0, rows, :]
        scores = (_dot_nt(q_ref[0, rows, :], kb) * dmask).astype(_BF16)
        kv = _dot((kb.astype(_F32) * zeta).T.astype(_BF16), v)
        return scores, kv

    def chunk_output(n, scores, state):
        rows = slice(n * CHUNK, (n + 1) * CHUNK)
        out = _dot(scores, v_ref[0, rows, :])
        out = out + _dot(q_ref[0, rows, :], state.astype(_BF16)) * xi
        y = out * lax.rsqrt(jnp.mean(out * out, axis=-1, keepdims=True) + EPS)
        gate = g_ref[0, rows, :].astype(_F32)
        o_ref[0, rows, :] = (y * _silu(gate)).astype(_BF16)

    state = state0
    scores, kv = chunk_products(0)
    for n in range(n_chunks):
        if n + 1 < n_chunks:
            nxt = chunk_products(n + 1)
        chunk_output(n, scores, state)
        state = state * chunk_decay + kv
        if n + 1 < n_chunks:
            scores, kv = nxt


def _retention(lg, z3, zm):
    b, s, _ = z3.shape
    kern = functools.partial(_retention_kernel, n_chunks=s // CHUNK)
    return pl.pallas_call(
        kern,
        grid=(b, RET_HEADS),
        in_specs=[
            pl.BlockSpec(memory_space=pltpu.SMEM),
            pl.BlockSpec((1, s, RET_DK), lambda i, h: (i, 0, COL_RQ + h)),
            pl.BlockSpec((1, s, RET_DK), lambda i, h: (i, 0, COL_RK + h)),
            pl.BlockSpec((1, s, RET_DV), lambda i, h: (i, 0, COL_RV // 2 + h)),
            pl.BlockSpec((1, s, RET_DV), lambda i, h: (i, 0, COL_RG // 2 + h)),
            pl.BlockSpec((CHUNK, RET_DK), lambda i, h: (0, COL_RK + h)),
            pl.BlockSpec((CHUNK, RET_DV), lambda i, h: (0, COL_RV // 2 + h)),
        ],
        out_specs=pl.BlockSpec((1, s, RET_DV), lambda i, h: (i, 0, h)),
        out_shape=jax.ShapeDtypeStruct((b, s, RET_HEADS * RET_DV), _BF16),
        compiler_params=pltpu.CompilerParams(
            dimension_semantics=("arbitrary", "arbitrary"),
            vmem_limit_bytes=VMEM_LIMIT),
        name="retention",
    )(lg, z3, z3, z3, z3, zm, zm)


def _fox_kernel(q_ref, k_ref, v_ref, g_ref, c_ref, km_ref, vm_ref, cm_ref, o_ref,
                vt_scr, vtm_scr, ka_scr, kam_scr, qt_scr, acc_scr, m_scr, s_scr, pm_scr,
                sm_scr, pmm_scr, *, n_tiles, n_pairs, seq):
    half = FOX_HEAD_DIM
    n_heads = 2 * n_pairs
    sub = lax.broadcasted_iota(jnp.int32, (LANES, 1), 0)
    sub8 = lax.broadcasted_iota(jnp.int32, (SUBLANES, 1), 0)
    lane = lax.broadcasted_iota(jnp.int32, (1, LANES), 1)

    def pieces8(row):
        out = jnp.zeros((SUBLANES, LANES), _F32)
        rest = row
        for piece in range(FOX_BIAS_PIECES):
            part = rest.astype(_BF16).astype(_F32)
            out = jnp.where(sub8 == piece, part, out)
            rest = rest - part
        return out

    def augmented_keys(kblk, bias_lo, bias_hi):
        gap = jnp.zeros((half - SUBLANES, LANES), _F32)
        tile = jnp.concatenate([pieces8(bias_hi), gap, pieces8(bias_lo), gap], axis=0).T
        tile = tile.astype(_BF16)[0:kblk.shape[0], :]
        return jnp.where(lane < half, kblk, tile), jnp.where(lane < half, tile, kblk)

    def build_meta_operands():
        for p in range(n_pairs):
            lanes = slice(p * LANES, (p + 1) * LANES)
            cm0, cm1 = cm_ref[p, 0:1, :], cm_ref[p, 1:2, :]
            kam_scr[2 * p], kam_scr[2 * p + 1] = augmented_keys(
                km_ref[0:N_META, lanes],
                (cm0[:, LANES - 1:LANES] - cm0) * LOG2E, (cm1[:, LANES - 1:LANES] - cm1) * LOG2E)
            vtm = vm_ref[:, lanes].T
            for hh in range(2):
                h = 2 * p + hh
                vtm_scr[h, 0:half, :] = vtm[hh * half:(hh + 1) * half, :]
                vtm_scr[h, half:FOX_ACC_ROWS, :] = jnp.ones((FOX_ACC_ROWS - half, LANES), _BF16)

    def build_block_operands(jb):
        rows = slice(jb * FOX_TK, (jb + 1) * FOX_TK)
        for p in range(n_pairs):
            lanes = slice(p * LANES, (p + 1) * LANES)
            for cb in range(jb * FOX_TK // LANES, (jb + 1) * FOX_TK // LANES):
                ks = slice(cb * LANES, (cb + 1) * LANES)
                ka_scr[2 * p, ks, :], ka_scr[2 * p + 1, ks, :] = augmented_keys(
                    k_ref[0, ks, lanes],
                    c_ref[0, p, 0:1, ks] * -LOG2E, c_ref[0, p, 1:2, ks] * -LOG2E)
            vt = v_ref[0, rows, lanes].T
            for hh in range(2):
                h = 2 * p + hh
                vt_scr[h, jb, 0:half, :] = vt[hh * half:(hh + 1) * half, :]
                vt_scr[h, jb, half:FOX_ACC_ROWS, :] = jnp.ones(
                    (FOX_ACC_ROWS - half, FOX_TK), _BF16)

    ri = lax.broadcasted_iota(jnp.int32, (FOX_TK, FOX_TQ), 0)
    ci = lax.broadcasted_iota(jnp.int32, (FOX_TK, FOX_TQ), 1)
    causal = ri <= ci

    def group_max(st):
        parts = [st[r:r + SUBLANES, :] for r in range(0, st.shape[0], SUBLANES)]
        while len(parts) > 1:
            nxt = [jnp.maximum(parts[a], parts[a + 1]) for a in range(0, len(parts) - 1, 2)]
            if len(parts) % 2:
                nxt.append(parts[-1])
            parts = nxt
        return parts[0]

    def update(h, tile, first, st, pm, vt):
        buf = tile % 2
        m_blk = jnp.max(pm, axis=0, keepdims=True)
        if first:
            m_new = m_blk
        else:
            m_old = m_scr[buf, h]
            m_new = jnp.maximum(m_old, m_blk)
            alpha = jnp.exp2(m_old - m_new)
        pt = jnp.exp2(st - m_new).astype(_BF16)
        if pt.shape[0] < LANES:
            pt = jnp.concatenate(
                [pt, jnp.zeros((LANES - pt.shape[0], FOX_TQ), _BF16)], axis=0)
        pv = _dot(vt, pt)
        acc_scr[buf, h] = pv if first else acc_scr[buf, h] * alpha + pv
        m_scr[buf, h] = m_new

    def prepare_queries(i):
        rows = slice(i * FOX_TQ, (i + 1) * FOX_TQ)
        for p in range(n_pairs):
            qt = q_ref[0, rows, p * LANES:(p + 1) * LANES].astype(_F32).T
            ones_hi = jnp.where(sub < half + FOX_BIAS_PIECES, 1.0, 0.0)
            ones_lo = jnp.where(sub < FOX_BIAS_PIECES, 1.0, 0.0)
            qt_scr[i % 2, 2 * p] = jnp.where(sub < half, qt, ones_hi).astype(_BF16)
            qt_scr[i % 2, 2 * p + 1] = jnp.where(sub < half, ones_lo, qt).astype(_BF16)

    def scores(item, heads):
        kind, i, j, slot = item
        if kind == "meta":
            for h in heads:
                st = _dot(kam_scr[h], qt_scr[i % 2, h])
                sm_scr[h] = st
                pmm_scr[h] = group_max(st)
            return
        krows = slice(j * FOX_TK, (j + 1) * FOX_TK)
        for h in heads:
            st = _dot(ka_scr[h, krows, :], qt_scr[i % 2, h])
            if i == j:
                st = jnp.where(causal, st, NEG_INF)
            s_scr[slot, h] = st
            pm_scr[slot, h] = group_max(st)

    def consume(item, heads):
        kind, i, j, slot = item
        for h in heads:
            if kind == "meta":
                update(h, i, True, sm_scr[h], pmm_scr[h], vtm_scr[h])
            else:
                update(h, i, False, s_scr[slot, h], pm_scr[slot, h], vt_scr[h, j])

    def finalize(i):
        rows = slice(i * FOX_TQ, (i + 1) * FOX_TQ)
        for p in range(n_pairs):
            lanes = slice(p * LANES, (p + 1) * LANES)
            outs = []
            for hh in range(2):
                acc = acc_scr[i % 2, 2 * p + hh]
                outs.append(acc[0:half, :] / acc[half:half + 1, :])
            y = jnp.concatenate(outs, axis=0).T
            gate = g_ref[0, rows, lanes].astype(_F32)
            o_ref[0, rows, lanes] = (y * _silu(gate)).astype(_BF16)

    items = []
    for i in range(n_tiles):
        items.append(("meta", i, None, None))
        for j in [i] + list(range(i)):
            items.append(("blk", i, j, sum(it[0] == "blk" for it in items) % 2))
    build_meta_operands()
    build_block_operands(0)
    prepare_queries(0)
    scores(items[0], range(n_heads))
    for t, item in enumerate(items):
        if t + 1 < len(items):
            nxt = items[t + 1]
            if nxt[0] == "meta":
                build_block_operands(nxt[1])
                prepare_queries(nxt[1])
            scores(nxt, range(n_heads))
        consume(item, range(n_heads))
        if t + 1 == len(items) or items[t + 1][1] != item[1]:
            finalize(item[1])


def _fox(z3, c4, zm, cm3):
    b, s, _ = z3.shape
    n_pairs = FOX_PAIRS
    n_groups = FOX_HEADS // 2 // n_pairs
    n_tiles = s // FOX_TQ
    w = n_pairs * LANES
    kern = functools.partial(_fox_kernel, n_tiles=n_tiles, n_pairs=n_pairs, seq=s)
    return pl.pallas_call(
        kern,
        grid=(b, n_groups),
        in_specs=[
            pl.BlockSpec((1, s, w), lambda i, p: (i, 0, COL_FQ // n_pairs + p)),
            pl.BlockSpec((1, s, w), lambda i, p: (i, 0, COL_FK // n_pairs + p)),
            pl.BlockSpec((1, s, w), lambda i, p: (i, 0, COL_FV // n_pairs + p)),
            pl.BlockSpec((1, s, w), lambda i, p: (i, 0, COL_FG // n_pairs + p)),
            pl.BlockSpec((1, n_pairs, 2, s), lambda i, p: (i, p, 0, 0)),
            pl.BlockSpec((CHUNK, w), lambda i, p: (0, COL_FK // n_pairs + p)),
            pl.BlockSpec((CHUNK, w), lambda i, p: (0, COL_FV // n_pairs + p)),
            pl.BlockSpec((n_pairs, 2, LANES), lambda i, p: (p, 0, 0)),
        ],
        out_specs=pl.BlockSpec((1, s, w), lambda i, p: (i, 0, p)),
        out_shape=jax.ShapeDtypeStruct((b, s, FOX_HEADS * FOX_HEAD_DIM), _BF16),
        scratch_shapes=[
            pltpu.VMEM((2 * n_pairs, n_tiles, FOX_ACC_ROWS, FOX_TK), _BF16),
            pltpu.VMEM((2 * n_pairs, FOX_ACC_ROWS, LANES), _BF16),
            pltpu.VMEM((2 * n_pairs, s, LANES), _BF16),
            pltpu.VMEM((2 * n_pairs, N_META, LANES), _BF16),
            pltpu.VMEM((2, 2 * n_pairs, LANES, FOX_TQ), _BF16),
            pltpu.VMEM((2, 2 * n_pairs, FOX_ACC_ROWS, FOX_TQ), _F32),
            pltpu.VMEM((2, 2 * n_pairs, 1, FOX_TQ), _F32),
            pltpu.VMEM((2, 2 * n_pairs, FOX_TK, FOX_TQ), _F32),
            pltpu.VMEM((2, 2 * n_pairs, SUBLANES, FOX_TQ), _F32),
            pltpu.VMEM((2 * n_pairs, N_META, FOX_TQ), _F32),
            pltpu.VMEM((2 * n_pairs, SUBLANES, FOX_TQ), _F32),
        ],
        compiler_params=pltpu.CompilerParams(
            dimension_semantics=("arbitrary", "arbitrary"),
            vmem_limit_bytes=VMEM_LIMIT),
        name="fox",
    )(z3, z3, z3, z3, c4, zm, zm, cm3)


def _outproj_kernel(yr_ref, yf_ref, x_ref, w_ref, g_ref, o_ref):
    half = w_ref.shape[0] // 2
    for r in range(x_ref.shape[0] // OUT_SUB):
        rows = pl.ds(r * OUT_SUB, OUT_SUB)
        hres = (x_ref[rows, :] + _dot(yr_ref[rows, :], w_ref[:half, :])
                + _dot(yf_ref[rows, :], w_ref[half:, :]))
        ms = jnp.mean(hres * hres, axis=-1, keepdims=True)
        o_ref[rows, :] = hres * lax.rsqrt(ms + EPS) * g_ref[...]


def _outproj(yr, yf, x2d, w_out, g, *, tm):
    rows = x2d.shape[0]
    return pl.pallas_call(
        _outproj_kernel,
        grid=(rows // tm,),
        in_specs=[
            pl.BlockSpec((tm, yr.shape[1]), lambda i: (i, 0)),
            pl.BlockSpec((tm, yf.shape[1]), lambda i: (i, 0)),
            pl.BlockSpec((tm, D_MODEL), lambda i: (i, 0)),
            pl.BlockSpec(w_out.shape, lambda i: (0, 0)),
            pl.BlockSpec((1, D_MODEL), lambda i: (0, 0)),
        ],
        out_specs=pl.BlockSpec((tm, D_MODEL), lambda i: (i, 0)),
        out_shape=jax.ShapeDtypeStruct((rows, D_MODEL), _F32),
        compiler_params=pltpu.CompilerParams(
            dimension_semantics=("arbitrary",),
            vmem_limit_bytes=VMEM_LIMIT),
        name="outproj",
    )(yr, yf, x2d, w_out, g)


def _rope_tables(pos):
    inv = ROPE_BASE ** (-jnp.arange(0, RET_DK, 2, dtype=_F32) / RET_DK)
    ang = pos[:, None] * inv[None, :]
    cos, sin = jnp.cos(ang), jnp.sin(ang)
    return jnp.concatenate([cos, cos], axis=-1), jnp.concatenate([-sin, sin], axis=-1)


def kernel(x, meta_tokens, norm_g, w_in, b_f, w_out, final_g):
    b, s, d = x.shape
    assert norm_g.shape[0] == 1 and d == D_MODEL and s % FOX_TQ == 0
    x2d = x.reshape(b * s, d)
    col_scale = jnp.ones((D_MAIN, 1), _F32)
    col_scale = col_scale.at[COL_RK * LANES:COL_RV * LANES].set(RET_DK ** -0.5)
    col_scale = col_scale.at[COL_FQ * LANES:COL_FK * LANES].set(FOX_HEAD_DIM ** -0.5 * LOG2E)
    w_all_t = jnp.swapaxes(w_in, 1, 2)[0]
    w_t = (w_all_t[:D_MAIN, :] * col_scale).astype(_BF16)
    w_ff_t = w_all_t[D_MAIN:, :]
    g = norm_g[0].reshape(1, d)
    bf = b_f[0].reshape(FOX_HEADS, 1)
    meta_pad = jnp.pad(meta_tokens.astype(_F32), ((0, CHUNK - N_META), (0, 0)))
    cos2, sin2 = _rope_tables(jnp.arange(s, dtype=_F32) + float(N_META))
    cosm, sinm = _rope_tables(jnp.arange(CHUNK, dtype=_F32))

    z, c = _inproj(x2d, g, w_t, w_ff_t, bf, cos2, sin2, tm=s, tn=1792, n_valid=s)
    zm, cm = _inproj(meta_pad, g, w_t, w_ff_t, bf, cosm, sinm,
                     tm=CHUNK, tn=3584, n_valid=N_META)

    z3 = z.reshape(b, s, D_MAIN)
    lg = jnp.log1p(-jnp.exp2(-5.0 - jnp.arange(RET_HEADS, dtype=_F32)))
    y_r = _retention(lg, z3, zm)

    c4 = c.reshape(b, FOX_HEADS // 2, 2, s)
    cm3 = cm.reshape(FOX_HEADS // 2, 2, CHUNK)
    y_f = _fox(z3, c4, zm, cm3)

    out = _outproj(y_r.reshape(b * s, -1), y_f.reshape(b * s, -1), x2d,
                   w_out[0].astype(_BF16), final_g.reshape(1, d), tm=1024)
    return out.reshape(b, s, d)
```

```python
import functools

import jax
import jax.numpy as jnp
from jax import lax
from jax.experimental import pallas as pl
from jax.experimental.pallas import tpu as pltpu

D_MODEL = 1024
N_META = 16
CHUNK = 128
RET_HEADS = 4
RET_DK = 128
RET_DV = 256
FOX_HEADS = 16
FOX_HEAD_DIM = 64
ROPE_BASE = 10000.0
EPS = 1e-6
NEG_INF = -1e30
LOG2E = 1.4426950408889634

LANES = 128
SUBLANES = 8
BF16_SUBLANES = 16
D_MAIN = 7168
COL_RQ, COL_RK, COL_RV, COL_RG = 0, 4, 8, 16
COL_FQ, COL_FK, COL_FV, COL_FG = 24, 32, 40, 48
FOX_TQ = 256
FOX_TK = 256
FOX_PAIRS = 2
FOX_ACC_ROWS = FOX_HEAD_DIM + BF16_SUBLANES
FOX_BIAS_PIECES = 3
PROJ_SUB = 512
OUT_SUB = 256
RET_HEADS_PER_STEP = 4
VMEM_LIMIT = 56 * 1024 * 1024

_F32 = jnp.float32
_BF16 = jnp.bfloat16


def _dot(a, b):
    return jnp.dot(a, b, preferred_element_type=_F32)


def _dot_nt(a, b):
    return lax.dot_general(a, b, (((1,), (1,)), ((), ())), preferred_element_type=_F32)


def _silu(g):
    h = 0.5 * g
    return h + h * jnp.tanh(h)


def _rotary(x, cos2, sin2):
    return x * cos2 + pltpu.roll(x, RET_DK // 2, axis=1) * sin2


def _inproj_kernel(x_ref, g_ref, w_ref, wff_ref, bf_ref, cos_ref, sin_ref,
                   z_ref, c_ref, u_scr, lf_scr, *, tm, tn, sub, n_valid):
    j = pl.program_id(1)
    n_sub = tm // sub

    def normalize(r):
        rows = pl.ds(r * sub, sub)
        xf = x_ref[rows, :]
        ms = jnp.mean(xf * xf, axis=-1, keepdims=True)
        u = (xf * lax.rsqrt(ms + EPS) * g_ref[...]).astype(_BF16)
        u_scr[rows, :] = u
        lf_scr[:, r * sub:(r + 1) * sub] = _dot_nt(wff_ref[...].astype(_BF16), u)

    def project(r, rot_cols):
        rows = pl.ds(r * sub, sub)
        acc = _dot_nt(u_scr[rows, :], w_ref[...])
        if rot_cols:
            cos2, sin2 = cos_ref[rows, :], sin_ref[rows, :]
            slabs = [_rotary(acc[:, a:a + RET_DK], cos2, sin2) for a in range(0, rot_cols, RET_DK)]
            acc = jnp.concatenate(slabs + [acc[:, rot_cols:]], axis=1) if rot_cols < tn else (
                jnp.concatenate(slabs, axis=1))
        z_ref[rows, :] = acc.astype(_BF16)

    def forget_cumsum():
        lane = lax.broadcasted_iota(jnp.int32, (FOX_HEADS, LANES), 1)
        carry = jnp.zeros((FOX_HEADS, 1), _F32)
        bf = bf_ref[...]
        for ci in range(tm // LANES):
            v = lf_scr[:, ci * LANES:(ci + 1) * LANES] + bf
            blk = jnp.minimum(v, 0.0) - jnp.log1p(jnp.exp(-jnp.abs(v)))
            if (ci + 1) * LANES > n_valid:
                blk = jnp.where(lane + ci * LANES < n_valid, blk, 0.0)
            sh = 1
            while sh < LANES:
                rolled = pltpu.roll(blk, sh, axis=1)
                blk = blk + jnp.where(lane >= sh, rolled, 0.0)
                sh *= 2
            blk = blk + carry
            c_ref[0, :, ci * LANES:(ci + 1) * LANES] = blk
            carry = blk[:, LANES - 1:LANES]

    rot_end = COL_RV * LANES
    rot_tiles = -(-rot_end // tn)

    for jj in range(rot_tiles):
        @pl.when(j == jj)
        def _(jj=jj):
            for r in range(n_sub):
                if jj == 0:
                    normalize(r)
                project(r, min(tn, rot_end - jj * tn))
            if jj == 0:
                forget_cumsum()

    @pl.when(j >= rot_tiles)
    def _():
        for r in range(n_sub):
            project(r, 0)


def _inproj(x2d, g, w_t, w_ff_t, b_f, cos2, sin2, *, tm, tn, n_valid):
    rows = x2d.shape[0]
    sub = min(tm, PROJ_SUB)
    assert D_MAIN % tn == 0 and tn % RET_DK == 0
    kern = functools.partial(_inproj_kernel, tm=tm, tn=tn, sub=sub, n_valid=n_valid)
    return pl.pallas_call(
        kern,
        grid=(rows // tm, D_MAIN // tn),
        in_specs=[
            pl.BlockSpec((tm, D_MODEL), lambda i, j: (i, 0)),
            pl.BlockSpec((1, D_MODEL), lambda i, j: (0, 0)),
            pl.BlockSpec((tn, D_MODEL), lambda i, j: (j, 0)),
            pl.BlockSpec((FOX_HEADS, D_MODEL), lambda i, j: (0, 0)),
            pl.BlockSpec((FOX_HEADS, 1), lambda i, j: (0, 0)),
            pl.BlockSpec((tm, RET_DK), lambda i, j: (0, 0)),
            pl.BlockSpec((tm, RET_DK), lambda i, j: (0, 0)),
        ],
        out_specs=[
            pl.BlockSpec((tm, tn), lambda i, j: (i, j)),
            pl.BlockSpec((1, FOX_HEADS, tm), lambda i, j: (i, 0, 0)),
        ],
        out_shape=[
            jax.ShapeDtypeStruct((rows, D_MAIN), _BF16),
            jax.ShapeDtypeStruct((rows // tm, FOX_HEADS, tm), _F32),
        ],
        scratch_shapes=[
            pltpu.VMEM((tm, D_MODEL), _BF16),
            pltpu.VMEM((FOX_HEADS, tm), _F32),
        ],
        compiler_params=pltpu.CompilerParams(
            dimension_semantics=("arbitrary", "arbitrary"),
            vmem_limit_bytes=VMEM_LIMIT),
        name="inproj",
    )(x2d, g, w_t, w_ff_t, b_f, cos2, sin2)


def _retention_kernel(lg_ref, q_ref, k_ref, v_ref, g_ref, km_ref, vm_ref, o_ref,
                      *, n_chunks, heads_per_step):
    ri = lax.broadcasted_iota(jnp.int32, (CHUNK, CHUNK), 0)
    ci = lax.broadcasted_iota(jnp.int32, (CHUNK, CHUNK), 1)
    diff = (ri - ci).astype(_F32)
    idx = lax.broadcasted_iota(jnp.int32, (CHUNK, 1), 0).astype(_F32)

    def one_head(hd):
        lg = lg_ref[pl.program_id(1) * heads_per_step + hd]
        kcols = slice(hd * RET_DK, (hd + 1) * RET_DK)
        vcols = slice(hd * RET_DV, (hd + 1) * RET_DV)
        dmask = jnp.where(diff >= 0, jnp.exp(lg * jnp.maximum(diff, 0.0)), 0.0)
        xi = jnp.exp(lg * (idx + 1.0))
        zeta = jnp.exp(lg * (CHUNK - 1.0 - idx))
        chunk_decay = jnp.exp(lg * jnp.full((1, 1), float(CHUNK), _F32))

        zeta_m = jnp.exp(lg * (N_META - 1.0 - idx))
        km = km_ref[:, kcols].astype(_F32)
        state = _dot((km * zeta_m).T.astype(_BF16), vm_ref[:, vcols])

        def chunk_products(n):
            rows = slice(n * CHUNK, (n + 1) * CHUNK)
            kb = k_ref[0, rows, kcols]
            v = v_ref[0, rows, vcols]
            scores = (_dot_nt(q_ref[0, rows, kcols], kb) * dmask).astype(_BF16)
            kv = _dot((kb.astype(_F32) * zeta).T.astype(_BF16), v)
            return scores, kv

        def chunk_output(n, scores, state):
            rows = slice(n * CHUNK, (n + 1) * CHUNK)
            out = _dot(scores, v_ref[0, rows, vcols])
            out = out + _dot(q_ref[0, rows, kcols], state.astype(_BF16)) * xi
            y = out * lax.rsqrt(jnp.mean(out * out, axis=-1, keepdims=True) + EPS)
            gate = g_ref[0, rows, vcols].astype(_F32)
            o_ref[0, rows, vcols] = (y * _silu(gate)).astype(_BF16)

        scores, kv = chunk_products(0)
        for n in range(n_chunks):
            if n + 1 < n_chunks:
                nxt = chunk_products(n + 1)
            chunk_output(n, scores, state)
            state = state * chunk_decay + kv
            if n + 1 < n_chunks:
                scores, kv = nxt

    for hd in range(heads_per_step):
        one_head(hd)


def _retention(lg, z3, zm):
    b, s, _ = z3.shape
    hps = RET_HEADS_PER_STEP
    wk, wv = hps * RET_DK, hps * RET_DV
    kern = functools.partial(_retention_kernel, n_chunks=s // CHUNK, heads_per_step=hps)
    return pl.pallas_call(
        kern,
        grid=(b, RET_HEADS // hps),
        in_specs=[
            pl.BlockSpec(memory_space=pltpu.SMEM),
            pl.BlockSpec((1, s, wk), lambda i, h: (i, 0, COL_RQ * LANES // wk + h)),
            pl.BlockSpec((1, s, wk), lambda i, h: (i, 0, COL_RK * LANES // wk + h)),
            pl.BlockSpec((1, s, wv), lambda i, h: (i, 0, COL_RV * LANES // wv + h)),
            pl.BlockSpec((1, s, wv), lambda i, h: (i, 0, COL_RG * LANES // wv + h)),
            pl.BlockSpec((CHUNK, wk), lambda i, h: (0, COL_RK * LANES // wk + h)),
            pl.BlockSpec((CHUNK, wv), lambda i, h: (0, COL_RV * LANES // wv + h)),
        ],
        out_specs=pl.BlockSpec((1, s, wv), lambda i, h: (i, 0, h)),
        out_shape=jax.ShapeDtypeStruct((b, s, RET_HEADS * RET_DV), _BF16),
        compiler_params=pltpu.CompilerParams(
            dimension_semantics=("arbitrary", "arbitrary"),
            vmem_limit_bytes=VMEM_LIMIT),
        name="retention",
    )(lg, z3, z3, z3, z3, zm, zm)


def _fox_kernel(q_ref, k_ref, v_ref, g_ref, c_ref, km_ref, vm_ref, cm_ref, o_ref,
                vt_scr, vtm_scr, ka_scr, kam_scr, qt_scr, acc_scr, m_scr, s_scr, pm_scr,
                sm_scr, pmm_scr, *, n_tiles, n_pairs, seq):
    half = FOX_HEAD_DIM
    n_heads = 2 * n_pairs
    sub = lax.broadcasted_iota(jnp.int32, (LANES, 1), 0)
    sub8 = lax.broadcasted_iota(jnp.int32, (SUBLANES, 1), 0)
    lane = lax.broadcasted_iota(jnp.int32, (1, LANES), 1)

    def pieces8(row):
        out = jnp.zeros((SUBLANES, LANES), _F32)
        rest = row
        for piece in range(FOX_BIAS_PIECES):
            part = rest.astype(_BF16).astype(_F32)
            out = jnp.where(sub8 == piece, part, out)
            rest = rest - part
        return out

    def augmented_keys(kblk, bias_lo, bias_hi):
        gap = jnp.zeros((half - SUBLANES, LANES), _F32)
        tile = jnp.concatenate([pieces8(bias_hi), gap, pieces8(bias_lo), gap], axis=0).T
        tile = tile.astype(_BF16)[0:kblk.shape[0], :]
        return jnp.where(lane < half, kblk, tile), jnp.where(lane < half, tile, kblk)

    def build_meta_operands():
        for p in range(n_pairs):
            lanes = slice(p * LANES, (p + 1) * LANES)
            cm0, cm1 = cm_ref[p, 0:1, :], cm_ref[p, 1:2, :]
            kam_scr[2 * p], kam_scr[2 * p + 1] = augmented_keys(
                km_ref[0:N_META, lanes],
                (cm0[:, LANES - 1:LANES] - cm0) * LOG2E, (cm1[:, LANES - 1:LANES] - cm1) * LOG2E)
            vtm = vm_ref[:, lanes].T
            for hh in range(2):
                h = 2 * p + hh
                vtm_scr[h, 0:half, :] = vtm[hh * half:(hh + 1) * half, :]
                vtm_scr[h, half:FOX_ACC_ROWS, :] = jnp.ones((FOX_ACC_ROWS - half, LANES), _BF16)

    def build_block_operands(jb):
        rows = slice(jb * FOX_TK, (jb + 1) * FOX_TK)
        for p in range(n_pairs):
            lanes = slice(p * LANES, (p + 1) * LANES)
            for cb in range(jb * FOX_TK // LANES, (jb + 1) * FOX_TK // LANES):
                ks = slice(cb * LANES, (cb + 1) * LANES)
                ka_scr[2 * p, ks, :], ka_scr[2 * p + 1, ks, :] = augmented_keys(
                    k_ref[0, ks, lanes],
                    c_ref[0, p, 0:1, ks] * -LOG2E, c_ref[0, p, 1:2, ks] * -LOG2E)
            vt = v_ref[0, rows, lanes].T
            for hh in range(2):
                h = 2 * p + hh
                vt_scr[h, jb, 0:half, :] = vt[hh * half:(hh + 1) * half, :]
                vt_scr[h, jb, half:FOX_ACC_ROWS, :] = jnp.ones(
                    (FOX_ACC_ROWS - half, FOX_TK), _BF16)

    ri = lax.broadcasted_iota(jnp.int32, (FOX_TK, FOX_TQ), 0)
    ci = lax.broadcasted_iota(jnp.int32, (FOX_TK, FOX_TQ), 1)
    causal = ri <= ci

    def group_max(st):
        parts = [st[r:r + SUBLANES, :] for r in range(0, st.shape[0], SUBLANES)]
        while len(parts) > 1:
            nxt = [jnp.maximum(parts[a], parts[a + 1]) for a in range(0, len(parts) - 1, 2)]
            if len(parts) % 2:
                nxt.append(parts[-1])
            parts = nxt
        return parts[0]

    def update(h, tile, first, st, pm, vt):
        buf = tile % 2
        m_blk = jnp.max(pm, axis=0, keepdims=True)
        if first:
            m_new = m_blk
        else:
            m_old = m_scr[buf, h]
            m_new = jnp.maximum(m_old, m_blk)
            alpha = jnp.exp2(m_old - m_new)
        pt = jnp.exp2(st - m_new).astype(_BF16)
        if pt.shape[0] < LANES:
            pt = jnp.concatenate(
                [pt, jnp.zeros((LANES - pt.shape[0], FOX_TQ), _BF16)], axis=0)
        pv = _dot(vt, pt)
        acc_scr[buf, h] = pv if first else acc_scr[buf, h] * alpha + pv
        m_scr[buf, h] = m_new

    def prepare_queries(i):
        rows = slice(i * FOX_TQ, (i + 1) * FOX_TQ)
        for p in range(n_pairs):
            qt = q_ref[0, rows, p * LANES:(p + 1) * LANES].astype(_F32).T
            ones_hi = jnp.where(sub < half + FOX_BIAS_PIECES, 1.0, 0.0)
            ones_lo = jnp.where(sub < FOX_BIAS_PIECES, 1.0, 0.0)
            qt_scr[i % 2, 2 * p] = jnp.where(sub < half, qt, ones_hi).astype(_BF16)
            qt_scr[i % 2, 2 * p + 1] = jnp.where(sub < half, ones_lo, qt).astype(_BF16)

    def scores(item, heads):
        kind, i, j, slot = item
        if kind == "meta":
            for h in heads:
                st = _dot(kam_scr[h], qt_scr[i % 2, h])
                sm_scr[h] = st
                pmm_scr[h] = group_max(st)
            return
        krows = slice(j * FOX_TK, (j + 1) * FOX_TK)
        for h in heads:
            st = _dot(ka_scr[h, krows, :], qt_scr[i % 2, h])
            if i == j:
                st = jnp.where(causal, st, NEG_INF)
            s_scr[slot, h] = st
            pm_scr[slot, h] = group_max(st)

    def consume(item, heads):
        kind, i, j, slot = item
        for h in heads:
            if kind == "meta":
                update(h, i, True, sm_scr[h], pmm_scr[h], vtm_scr[h])
            else:
                update(h, i, False, s_scr[slot, h], pm_scr[slot, h], vt_scr[h, j])

    def finalize(i):
        rows = slice(i * FOX_TQ, (i + 1) * FOX_TQ)
        for p in range(n_pairs):
            lanes = slice(p * LANES, (p + 1) * LANES)
            outs = []
            for hh in range(2):
                acc = acc_scr[i % 2, 2 * p + hh]
                outs.append(acc[0:half, :] / acc[half:half + 1, :])
            y = jnp.concatenate(outs, axis=0).T
            gate = g_ref[0, rows, lanes].astype(_F32)
            o_ref[0, rows, lanes] = (y * _silu(gate)).astype(_BF16)

    items = []
    for i in range(n_tiles):
        items.append(("meta", i, None, None))
        for j in [i] + list(range(i)):
            items.append(("blk", i, j, sum(it[0] == "blk" for it in items) % 2))
    build_meta_operands()
    build_block_operands(0)
    prepare_queries(0)
    scores(items[0], range(n_heads))
    for t, item in enumerate(items):
        if t + 1 < len(items):
            nxt = items[t + 1]
            if nxt[0] == "meta":
                build_block_operands(nxt[1])
                prepare_queries(nxt[1])
            scores(nxt, range(n_heads))
        consume(item, range(n_heads))
        if t + 1 == len(items) or items[t + 1][1] != item[1]:
            finalize(item[1])


def _fox(z3, c4, zm, cm3):
    b, s, _ = z3.shape
    n_pairs = FOX_PAIRS
    n_groups = FOX_HEADS // 2 // n_pairs
    n_tiles = s // FOX_TQ
    w = n_pairs * LANES
    kern = functools.partial(_fox_kernel, n_tiles=n_tiles, n_pairs=n_pairs, seq=s)
    return pl.pallas_call(
        kern,
        grid=(b, n_groups),
        in_specs=[
            pl.BlockSpec((1, s, w), lambda i, p: (i, 0, COL_FQ // n_pairs + p)),
            pl.BlockSpec((1, s, w), lambda i, p: (i, 0, COL_FK // n_pairs + p)),
            pl.BlockSpec((1, s, w), lambda i, p: (i, 0, COL_FV // n_pairs + p)),
            pl.BlockSpec((1, s, w), lambda i, p: (i, 0, COL_FG // n_pairs + p)),
            pl.BlockSpec((1, n_pairs, 2, s), lambda i, p: (i, p, 0, 0)),
            pl.BlockSpec((CHUNK, w), lambda i, p: (0, COL_FK // n_pairs + p)),
            pl.BlockSpec((CHUNK, w), lambda i, p: (0, COL_FV // n_pairs + p)),
            pl.BlockSpec((n_pairs, 2, LANES), lambda i, p: (p, 0, 0)),
        ],
        out_specs=pl.BlockSpec((1, s, w), lambda i, p: (i, 0, p)),
        out_shape=jax.ShapeDtypeStruct((b, s, FOX_HEADS * FOX_HEAD_DIM), _BF16),
        scratch_shapes=[
            pltpu.VMEM((2 * n_pairs, n_tiles, FOX_ACC_ROWS, FOX_TK), _BF16),
            pltpu.VMEM((2 * n_pairs, FOX_ACC_ROWS, LANES), _BF16),
            pltpu.VMEM((2 * n_pairs, s, LANES), _BF16),
            pltpu.VMEM((2 * n_pairs, N_META, LANES), _BF16),
            pltpu.VMEM((2, 2 * n_pairs, LANES, FOX_TQ), _BF16),
            pltpu.VMEM((2, 2 * n_pairs, FOX_ACC_ROWS, FOX_TQ), _F32),
            pltpu.VMEM((2, 2 * n_pairs, 1, FOX_TQ), _F32),
            pltpu.VMEM((2, 2 * n_pairs, FOX_TK, FOX_TQ), _F32),
            pltpu.VMEM((2, 2 * n_pairs, SUBLANES, FOX_TQ), _F32),
            pltpu.VMEM((2 * n_pairs, N_META, FOX_TQ), _F32),
            pltpu.VMEM((2 * n_pairs, SUBLANES, FOX_TQ), _F32),
        ],
        compiler_params=pltpu.CompilerParams(
            dimension_semantics=("arbitrary", "arbitrary"),
            vmem_limit_bytes=VMEM_LIMIT),
        name="fox",
    )(z3, z3, z3, z3, c4, zm, zm, cm3)


def _outproj_kernel(yr_ref, yf_ref, x_ref, w_ref, g_ref, o_ref):
    half = w_ref.shape[0] // 2
    for r in range(x_ref.shape[0] // OUT_SUB):
        rows = pl.ds(r * OUT_SUB, OUT_SUB)
        hres = (x_ref[rows, :] + _dot(yr_ref[rows, :], w_ref[:half, :])
                + _dot(yf_ref[rows, :], w_ref[half:, :]))
        ms = jnp.mean(hres * hres, axis=-1, keepdims=True)
        o_ref[rows, :] = hres * lax.rsqrt(ms + EPS) * g_ref[...]


def _outproj(yr, yf, x2d, w_out, g, *, tm):
    rows = x2d.shape[0]
    return pl.pallas_call(
        _outproj_kernel,
        grid=(rows // tm,),
        in_specs=[
            pl.BlockSpec((tm, yr.shape[1]), lambda i: (i, 0)),
            pl.BlockSpec((tm, yf.shape[1]), lambda i: (i, 0)),
            pl.BlockSpec((tm, D_MODEL), lambda i: (i, 0)),
            pl.BlockSpec(w_out.shape, lambda i: (0, 0)),
            pl.BlockSpec((1, D_MODEL), lambda i: (0, 0)),
        ],
        out_specs=pl.BlockSpec((tm, D_MODEL), lambda i: (i, 0)),
        out_shape=jax.ShapeDtypeStruct((rows, D_MODEL), _F32),
        compiler_params=pltpu.CompilerParams(
            dimension_semantics=("arbitrary",),
            vmem_limit_bytes=VMEM_LIMIT),
        name="outproj",
    )(yr, yf, x2d, w_out, g)


def _rope_tables(pos):
    inv = ROPE_BASE ** (-jnp.arange(0, RET_DK, 2, dtype=_F32) / RET_DK)
    ang = pos[:, None] * inv[None, :]
    cos, sin = jnp.cos(ang), jnp.sin(ang)
    return jnp.concatenate([cos, cos], axis=-1), jnp.concatenate([-sin, sin], axis=-1)


def kernel(x, meta_tokens, norm_g, w_in, b_f, w_out, final_g):
    b, s, d = x.shape
    assert norm_g.shape[0] == 1 and d == D_MODEL and s % FOX_TQ == 0
    x2d = x.reshape(b * s, d)
    col_scale = jnp.ones((D_MAIN, 1), _F32)
    col_scale = col_scale.at[COL_RK * LANES:COL_RV * LANES].set(RET_DK ** -0.5)
    col_scale = col_scale.at[COL_FQ * LANES:COL_FK * LANES].set(FOX_HEAD_DIM ** -0.5 * LOG2E)
    w_all_t = jnp.swapaxes(w_in, 1, 2)[0]
    w_t = (w_all_t[:D_MAIN, :] * col_scale).astype(_BF16)
    w_ff_t = w_all_t[D_MAIN:, :]
    g = norm_g[0].reshape(1, d)
    bf = b_f[0].reshape(FOX_HEADS, 1)
    meta_pad = jnp.pad(meta_tokens.astype(_F32), ((0, CHUNK - N_META), (0, 0)))
    cos2, sin2 = _rope_tables(jnp.arange(s, dtype=_F32) + float(N_META))
    cosm, sinm = _rope_tables(jnp.arange(CHUNK, dtype=_F32))

    z, c = _inproj(x2d, g, w_t, w_ff_t, bf, cos2, sin2, tm=s, tn=1792, n_valid=s)
    zm, cm = _inproj(meta_pad, g, w_t, w_ff_t, bf, cosm, sinm,
                     tm=CHUNK, tn=3584, n_valid=N_META)

    z3 = z.reshape(b, s, D_MAIN)
    lg = jnp.log1p(-jnp.exp2(-5.0 - jnp.arange(RET_HEADS, dtype=_F32)))
    y_r = _retention(lg, z3, zm)

    c4 = c.reshape(b, FOX_HEADS // 2, 2, s)
    cm3 = cm.reshape(FOX_HEADS // 2, 2, CHUNK)
    y_f = _fox(z3, c4, zm, cm3)

    out = _outproj(y_r.reshape(b * s, -1), y_f.reshape(b * s, -1), x2d,
                   w_out[0].astype(_BF16), final_g.reshape(1, d), tm=1024)
    return out.reshape(b, s, d)
```

```python
import functools

import jax
import jax.numpy as jnp
from jax import lax
from jax.experimental import pallas as pl
from jax.experimental.pallas import tpu as pltpu

D_MODEL = 1024
N_META = 16
CHUNK = 128
RET_HEADS = 4
RET_DK = 128
RET_DV = 256
FOX_HEADS = 16
FOX_HEAD_DIM = 64
ROPE_BASE = 10000.0
EPS = 1e-6
NEG_INF = -1e30
LOG2E = 1.4426950408889634

LANES = 128
SUBLANES = 8
BF16_SUBLANES = 16
D_MAIN = 7168
COL_RQ, COL_RK, COL_RV, COL_RG = 0, 4, 8, 16
COL_FQ, COL_FK, COL_FV, COL_FG = 24, 32, 40, 48
FOX_TQ = 256
FOX_TK = 256
FOX_PAIRS = 2
FOX_ACC_ROWS = FOX_HEAD_DIM + BF16_SUBLANES
FOX_BIAS_PIECES = 3
PROJ_SUB = 512
OUT_SUB = 256
RET_HEADS_PER_STEP = 4
VMEM_LIMIT = 56 * 1024 * 1024

_F32 = jnp.float32
_BF16 = jnp.bfloat16


def _dot(a, b):
    return jnp.dot(a, b, preferred_element_type=_F32)


def _dot_nt(a, b):
    return lax.dot_general(a, b, (((1,), (1,)), ((), ())), preferred_element_type=_F32)


def _silu(g):
    h = 0.5 * g
    return h + h * jnp.tanh(h)


def _rotary(x, cos2, sin2):
    return x * cos2 + pltpu.roll(x, RET_DK // 2, axis=1) * sin2


def _inproj_kernel(x_ref, g_ref, w_ref, cs_ref, wff_ref, bf_ref, cos_ref, sin_ref,
                   z_ref, c_ref, wcast_ref, u_scr, lf_scr, *, tm, tn, sub, n_valid, cast_weights):
    j = pl.program_id(1)
    n_sub = tm // sub

    def weight_tile():
        if not cast_weights:
            wcast_ref[...] = jnp.zeros(wcast_ref.shape, _BF16)
            return w_ref[...]
        wcast_ref[...] = (w_ref[...] * cs_ref[...]).astype(_BF16)
        return wcast_ref[...]

    def normalize(r):
        rows = pl.ds(r * sub, sub)
        xf = x_ref[rows, :]
        ms = jnp.mean(xf * xf, axis=-1, keepdims=True)
        u = (xf * lax.rsqrt(ms + EPS) * g_ref[...]).astype(_BF16)
        u_scr[rows, :] = u
        lf_scr[:, r * sub:(r + 1) * sub] = _dot_nt(wff_ref[...].astype(_BF16), u)

    def project(r, w_tile, rot_cols):
        rows = pl.ds(r * sub, sub)
        acc = _dot_nt(u_scr[rows, :], w_tile)
        if rot_cols:
            cos2, sin2 = cos_ref[rows, :], sin_ref[rows, :]
            slabs = [_rotary(acc[:, a:a + RET_DK], cos2, sin2) for a in range(0, rot_cols, RET_DK)]
            acc = jnp.concatenate(slabs + [acc[:, rot_cols:]], axis=1) if rot_cols < tn else (
                jnp.concatenate(slabs, axis=1))
        z_ref[rows, :] = acc.astype(_BF16)

    def forget_cumsum():
        lane = lax.broadcasted_iota(jnp.int32, (FOX_HEADS, LANES), 1)
        carry = jnp.zeros((FOX_HEADS, 1), _F32)
        bf = bf_ref[...]
        for ci in range(tm // LANES):
            v = lf_scr[:, ci * LANES:(ci + 1) * LANES] + bf
            blk = jnp.minimum(v, 0.0) - jnp.log1p(jnp.exp(-jnp.abs(v)))
            if (ci + 1) * LANES > n_valid:
                blk = jnp.where(lane + ci * LANES < n_valid, blk, 0.0)
            sh = 1
            while sh < LANES:
                rolled = pltpu.roll(blk, sh, axis=1)
                blk = blk + jnp.where(lane >= sh, rolled, 0.0)
                sh *= 2
            blk = blk + carry
            c_ref[0, :, ci * LANES:(ci + 1) * LANES] = blk
            carry = blk[:, LANES - 1:LANES]

    rot_end = COL_RV * LANES
    rot_tiles = -(-rot_end // tn)

    for jj in range(rot_tiles):
        @pl.when(j == jj)
        def _(jj=jj):
            w_tile = weight_tile()
            for r in range(n_sub):
                if jj == 0:
                    normalize(r)
                project(r, w_tile, min(tn, rot_end - jj * tn))
            if jj == 0:
                forget_cumsum()

    @pl.when(j >= rot_tiles)
    def _():
        w_tile = weight_tile()
        for r in range(n_sub):
            project(r, w_tile, 0)


def _inproj(x2d, g, w_t, col_scale, w_ff_t, b_f, cos2, sin2, *, tm, tn, n_valid):
    rows = x2d.shape[0]
    sub = min(tm, PROJ_SUB)
    assert D_MAIN % tn == 0 and tn % RET_DK == 0
    cast = w_t.dtype != _BF16
    wrows = tn if cast else SUBLANES
    kern = functools.partial(_inproj_kernel, tm=tm, tn=tn, sub=sub, n_valid=n_valid,
                             cast_weights=cast)
    return pl.pallas_call(
        kern,
        grid=(rows // tm, D_MAIN // tn),
        in_specs=[
            pl.BlockSpec((tm, D_MODEL), lambda i, j: (i, 0)),
            pl.BlockSpec((1, D_MODEL), lambda i, j: (0, 0)),
            pl.BlockSpec((tn, D_MODEL), lambda i, j: (j, 0)),
            pl.BlockSpec((wrows, 1), lambda i, j: (j if cast else 0, 0)),
            pl.BlockSpec((FOX_HEADS, D_MODEL), lambda i, j: (0, 0)),
            pl.BlockSpec((FOX_HEADS, 1), lambda i, j: (0, 0)),
            pl.BlockSpec((tm, RET_DK), lambda i, j: (0, 0)),
            pl.BlockSpec((tm, RET_DK), lambda i, j: (0, 0)),
        ],
        out_specs=[
            pl.BlockSpec((tm, tn), lambda i, j: (i, j)),
            pl.BlockSpec((1, FOX_HEADS, tm), lambda i, j: (i, 0, 0)),
            pl.BlockSpec((wrows, D_MODEL), lambda i, j: (j if cast else 0, 0)),
        ],
        out_shape=[
            jax.ShapeDtypeStruct((rows, D_MAIN), _BF16),
            jax.ShapeDtypeStruct((rows // tm, FOX_HEADS, tm), _F32),
            jax.ShapeDtypeStruct((D_MAIN if cast else SUBLANES, D_MODEL), _BF16),
        ],
        scratch_shapes=[
            pltpu.VMEM((tm, D_MODEL), _BF16),
            pltpu.VMEM((FOX_HEADS, tm), _F32),
        ],
        compiler_params=pltpu.CompilerParams(
            dimension_semantics=("arbitrary", "arbitrary"),
            vmem_limit_bytes=VMEM_LIMIT),
        name="inproj",
    )(x2d, g, w_t, col_scale, w_ff_t, b_f, cos2, sin2)


def _retention_kernel(lg_ref, q_ref, k_ref, v_ref, g_ref, km_ref, vm_ref, o_ref,
                      *, n_chunks, heads_per_step):
    ri = lax.broadcasted_iota(jnp.int32, (CHUNK, CHUNK), 0)
    ci = lax.broadcasted_iota(jnp.int32, (CHUNK, CHUNK), 1)
    diff = (ri - ci).astype(_F32)
    idx = lax.broadcasted_iota(jnp.int32, (CHUNK, 1), 0).astype(_F32)

    def one_head(hd):
        lg = lg_ref[pl.program_id(1) * heads_per_step + hd]
        kcols = slice(hd * RET_DK, (hd + 1) * RET_DK)
        vcols = slice(hd * RET_DV, (hd + 1) * RET_DV)
        dmask = jnp.where(diff >= 0, jnp.exp(lg * jnp.maximum(diff, 0.0)), 0.0)
        xi = jnp.exp(lg * (idx + 1.0))
        zeta = jnp.exp(lg * (CHUNK - 1.0 - idx))
        chunk_decay = jnp.exp(lg * jnp.full((1, 1), float(CHUNK), _F32))

        zeta_m = jnp.exp(lg * (N_META - 1.0 - idx))
        km = km_ref[:, kcols].astype(_F32)
        state = _dot((km * zeta_m).T.astype(_BF16), vm_ref[:, vcols])

        def chunk_products(n):
            rows = slice(n * CHUNK, (n + 1) * CHUNK)
            kb = k_ref[0, rows, kcols]
            v = v_ref[0, rows, vcols]
            scores = (_dot_nt(q_ref[0, rows, kcols], kb) * dmask).astype(_BF16)
            kv = _dot((kb.astype(_F32) * zeta).T.astype(_BF16), v)
            return scores, kv

        def chunk_output(n, scores, state):
            rows = slice(n * CHUNK, (n + 1) * CHUNK)
            out = _dot(scores, v_ref[0, rows, vcols])
            out = out + _dot(q_ref[0, rows, kcols], state.astype(_BF16)) * xi
            y = out * lax.rsqrt(jnp.mean(out * out, axis=-1, keepdims=True) + EPS)
            gate = g_ref[0, rows, vcols].astype(_F32)
            o_ref[0, rows, vcols] = (y * _silu(gate)).astype(_BF16)

        scores, kv = chunk_products(0)
        for n in range(n_chunks):
            if n + 1 < n_chunks:
                nxt = chunk_products(n + 1)
            chunk_output(n, scores, state)
            state = state * chunk_decay + kv
            if n + 1 < n_chunks:
                scores, kv = nxt

    for hd in range(heads_per_step):
        one_head(hd)


def _retention(lg, z3, zm):
    b, s, _ = z3.shape
    hps = RET_HEADS_PER_STEP
    wk, wv = hps * RET_DK, hps * RET_DV
    kern = functools.partial(_retention_kernel, n_chunks=s // CHUNK, heads_per_step=hps)
    return pl.pallas_call(
        kern,
        grid=(b, RET_HEADS // hps),
        in_specs=[
            pl.BlockSpec(memory_space=pltpu.SMEM),
            pl.BlockSpec((1, s, wk), lambda i, h: (i, 0, COL_RQ * LANES // wk + h)),
            pl.BlockSpec((1, s, wk), lambda i, h: (i, 0, COL_RK * LANES // wk + h)),
            pl.BlockSpec((1, s, wv), lambda i, h: (i, 0, COL_RV * LANES // wv + h)),
            pl.BlockSpec((1, s, wv), lambda i, h: (i, 0, COL_RG * LANES // wv + h)),
            pl.BlockSpec((CHUNK, wk), lambda i, h: (0, COL_RK * LANES // wk + h)),
            pl.BlockSpec((CHUNK, wv), lambda i, h: (0, COL_RV * LANES // wv + h)),
        ],
        out_specs=pl.BlockSpec((1, s, wv), lambda i, h: (i, 0, h)),
        out_shape=jax.ShapeDtypeStruct((b, s, RET_HEADS * RET_DV), _BF16),
        compiler_params=pltpu.CompilerParams(
            dimension_semantics=("arbitrary", "arbitrary"),
            vmem_limit_bytes=VMEM_LIMIT),
        name="retention",
    )(lg, z3, z3, z3, z3, zm, zm)


def _fox_kernel(q_ref, k_ref, v_ref, g_ref, c_ref, km_ref, vm_ref, cm_ref, o_ref,
                vt_scr, vtm_scr, ka_scr, kam_scr, qt_scr, acc_scr, m_scr, s_scr, pm_scr,
                sm_scr, pmm_scr, *, n_tiles, n_pairs, seq):
    half = FOX_HEAD_DIM
    n_heads = 2 * n_pairs
    sub = lax.broadcasted_iota(jnp.int32, (LANES, 1), 0)
    sub8 = lax.broadcasted_iota(jnp.int32, (SUBLANES, 1), 0)
    lane = lax.broadcasted_iota(jnp.int32, (1, LANES), 1)

    def pieces8(row):
        out = jnp.zeros((SUBLANES, LANES), _F32)
        rest = row
        for piece in range(FOX_BIAS_PIECES):
            part = rest.astype(_BF16).astype(_F32)
            out = jnp.where(sub8 == piece, part, out)
            rest = rest - part
        return out

    def augmented_keys(kblk, bias_lo, bias_hi):
        gap = jnp.zeros((half - SUBLANES, LANES), _F32)
        tile = jnp.concatenate([pieces8(bias_hi), gap, pieces8(bias_lo), gap], axis=0).T
        tile = tile.astype(_BF16)[0:kblk.shape[0], :]
        return jnp.where(lane < half, kblk, tile), jnp.where(lane < half, tile, kblk)

    def build_meta_operands():
        for p in range(n_pairs):
            lanes = slice(p * LANES, (p + 1) * LANES)
            cm0, cm1 = cm_ref[p, 0:1, :], cm_ref[p, 1:2, :]
            kam_scr[2 * p], kam_scr[2 * p + 1] = augmented_keys(
                km_ref[0:N_META, lanes],
                (cm0[:, LANES - 1:LANES] - cm0) * LOG2E, (cm1[:, LANES - 1:LANES] - cm1) * LOG2E)
            vtm = vm_ref[:, lanes].T
            for hh in range(2):
                h = 2 * p + hh
                vtm_scr[h, 0:half, :] = vtm[hh * half:(hh + 1) * half, :]
                vtm_scr[h, half:FOX_ACC_ROWS, :] = jnp.ones((FOX_ACC_ROWS - half, LANES), _BF16)

    def build_block_operands(jb):
        rows = slice(jb * FOX_TK, (jb + 1) * FOX_TK)
        for p in range(n_pairs):
            lanes = slice(p * LANES, (p + 1) * LANES)
            for cb in range(jb * FOX_TK // LANES, (jb + 1) * FOX_TK // LANES):
                ks = slice(cb * LANES, (cb + 1) * LANES)
                ka_scr[2 * p, ks, :], ka_scr[2 * p + 1, ks, :] = augmented_keys(
                    k_ref[0, ks, lanes],
                    c_ref[0, p, 0:1, ks] * -LOG2E, c_ref[0, p, 1:2, ks] * -LOG2E)
            vt = v_ref[0, rows, lanes].T
            for hh in range(2):
                h = 2 * p + hh
                vt_scr[h, jb, 0:half, :] = vt[hh * half:(hh + 1) * half, :]
                vt_scr[h, jb, half:FOX_ACC_ROWS, :] = jnp.ones(
                    (FOX_ACC_ROWS - half, FOX_TK), _BF16)

    ri = lax.broadcasted_iota(jnp.int32, (FOX_TK, FOX_TQ), 0)
    ci = lax.broadcasted_iota(jnp.int32, (FOX_TK, FOX_TQ), 1)
    causal = ri <= ci

    def group_max(st):
        parts = [st[r:r + SUBLANES, :] for r in range(0, st.shape[0], SUBLANES)]
        while len(parts) > 1:
            nxt = [jnp.maximum(parts[a], parts[a + 1]) for a in range(0, len(parts) - 1, 2)]
            if len(parts) % 2:
                nxt.append(parts[-1])
            parts = nxt
        return parts[0]

    def update(h, tile, first, st, pm, vt):
        buf = tile % 2
        m_blk = jnp.max(pm, axis=0, keepdims=True)
        if first:
            m_new = m_blk
        else:
            m_old = m_scr[buf, h]
            m_new = jnp.maximum(m_old, m_blk)
            alpha = jnp.exp2(m_old - m_new)
        pt = jnp.exp2(st - m_new).astype(_BF16)
        if pt.shape[0] < LANES:
            pt = jnp.concatenate(
                [pt, jnp.zeros((LANES - pt.shape[0], FOX_TQ), _BF16)], axis=0)
        pv = _dot(vt, pt)
        acc_scr[buf, h] = pv if first else acc_scr[buf, h] * alpha + pv
        m_scr[buf, h] = m_new

    def prepare_queries(i):
        rows = slice(i * FOX_TQ, (i + 1) * FOX_TQ)
        for p in range(n_pairs):
            qt = q_ref[0, rows, p * LANES:(p + 1) * LANES].astype(_F32).T
            ones_hi = jnp.where(sub < half + FOX_BIAS_PIECES, 1.0, 0.0)
            ones_lo = jnp.where(sub < FOX_BIAS_PIECES, 1.0, 0.0)
            qt_scr[i % 2, 2 * p] = jnp.where(sub < half, qt, ones_hi).astype(_BF16)
            qt_scr[i % 2, 2 * p + 1] = jnp.where(sub < half, ones_lo, qt).astype(_BF16)

    def scores(item, heads):
        kind, i, j, slot = item
        if kind == "meta":
            for h in heads:
                st = _dot(kam_scr[h], qt_scr[i % 2, h])
                sm_scr[h] = st
                pmm_scr[h] = group_max(st)
            return
        krows = slice(j * FOX_TK, (j + 1) * FOX_TK)
        for h in heads:
            st = _dot(ka_scr[h, krows, :], qt_scr[i % 2, h])
            if i == j:
                st = jnp.where(causal, st, NEG_INF)
            s_scr[slot, h] = st
            pm_scr[slot, h] = group_max(st)

    def consume(item, heads):
        kind, i, j, slot = item
        for h in heads:
            if kind == "meta":
                update(h, i, True, sm_scr[h], pmm_scr[h], vtm_scr[h])
            else:
                update(h, i, False, s_scr[slot, h], pm_scr[slot, h], vt_scr[h, j])

    def finalize(i):
        rows = slice(i * FOX_TQ, (i + 1) * FOX_TQ)
        for p in range(n_pairs):
            lanes = slice(p * LANES, (p + 1) * LANES)
            outs = []
            for hh in range(2):
                acc = acc_scr[i % 2, 2 * p + hh]
                outs.append(acc[0:half, :] / acc[half:half + 1, :])
            y = jnp.concatenate(outs, axis=0).T
            gate = g_ref[0, rows, lanes].astype(_F32)
            o_ref[0, rows, lanes] = (y * _silu(gate)).astype(_BF16)

    items = []
    for i in range(n_tiles):
        items.append(("meta", i, None, None))
        for j in [i] + list(range(i)):
            items.append(("blk", i, j, sum(it[0] == "blk" for it in items) % 2))
    build_meta_operands()
    build_block_operands(0)
    prepare_queries(0)
    scores(items[0], range(n_heads))
    for t, item in enumerate(items):
        if t + 1 < len(items):
            nxt = items[t + 1]
            if nxt[0] == "meta":
                build_block_operands(nxt[1])
                prepare_queries(nxt[1])
            scores(nxt, range(n_heads))
        consume(item, range(n_heads))
        if t + 1 == len(items) or items[t + 1][1] != item[1]:
            finalize(item[1])


def _fox(z3, c4, zm, cm3):
    b, s, _ = z3.shape
    n_pairs = FOX_PAIRS
    n_groups = FOX_HEADS // 2 // n_pairs
    n_tiles = s // FOX_TQ
    w = n_pairs * LANES
    kern = functools.partial(_fox_kernel, n_tiles=n_tiles, n_pairs=n_pairs, seq=s)
    return pl.pallas_call(
        kern,
        grid=(b, n_groups),
        in_specs=[
            pl.BlockSpec((1, s, w), lambda i, p: (i, 0, COL_FQ // n_pairs + p)),
            pl.BlockSpec((1, s, w), lambda i, p: (i, 0, COL_FK // n_pairs + p)),
            pl.BlockSpec((1, s, w), lambda i, p: (i, 0, COL_FV // n_pairs + p)),
            pl.BlockSpec((1, s, w), lambda i, p: (i, 0, COL_FG // n_pairs + p)),
            pl.BlockSpec((1, n_pairs, 2, s), lambda i, p: (i, p, 0, 0)),
            pl.BlockSpec((CHUNK, w), lambda i, p: (0, COL_FK // n_pairs + p)),
            pl.BlockSpec((CHUNK, w), lambda i, p: (0, COL_FV // n_pairs + p)),
            pl.BlockSpec((n_pairs, 2, LANES), lambda i, p: (p, 0, 0)),
        ],
        out_specs=pl.BlockSpec((1, s, w), lambda i, p: (i, 0, p)),
        out_shape=jax.ShapeDtypeStruct((b, s, FOX_HEADS * FOX_HEAD_DIM), _BF16),
        scratch_shapes=[
            pltpu.VMEM((2 * n_pairs, n_tiles, FOX_ACC_ROWS, FOX_TK), _BF16),
            pltpu.VMEM((2 * n_pairs, FOX_ACC_ROWS, LANES), _BF16),
            pltpu.VMEM((2 * n_pairs, s, LANES), _BF16),
            pltpu.VMEM((2 * n_pairs, N_META, LANES), _BF16),
            pltpu.VMEM((2, 2 * n_pairs, LANES, FOX_TQ), _BF16),
            pltpu.VMEM((2, 2 * n_pairs, FOX_ACC_ROWS, FOX_TQ), _F32),
            pltpu.VMEM((2, 2 * n_pairs, 1, FOX_TQ), _F32),
            pltpu.VMEM((2, 2 * n_pairs, FOX_TK, FOX_TQ), _F32),
            pltpu.VMEM((2, 2 * n_pairs, SUBLANES, FOX_TQ), _F32),
            pltpu.VMEM((2 * n_pairs, N_META, FOX_TQ), _F32),
            pltpu.VMEM((2 * n_pairs, SUBLANES, FOX_TQ), _F32),
        ],
        compiler_params=pltpu.CompilerParams(
            dimension_semantics=("arbitrary", "arbitrary"),
            vmem_limit_bytes=VMEM_LIMIT),
        name="fox",
    )(z3, z3, z3, z3, c4, zm, zm, cm3)


def _outproj_kernel(yr_ref, yf_ref, x_ref, w_ref, g_ref, o_ref):
    half = w_ref.shape[0] // 2
    for r in range(x_ref.shape[0] // OUT_SUB):
        rows = pl.ds(r * OUT_SUB, OUT_SUB)
        hres = (x_ref[rows, :] + _dot(yr_ref[rows, :], w_ref[:half, :])
                + _dot(yf_ref[rows, :], w_ref[half:, :]))
        ms = jnp.mean(hres * hres, axis=-1, keepdims=True)
        o_ref[rows, :] = hres * lax.rsqrt(ms + EPS) * g_ref[...]


def _outproj(yr, yf, x2d, w_out, g, *, tm):
    rows = x2d.shape[0]
    return pl.pallas_call(
        _outproj_kernel,
        grid=(rows // tm,),
        in_specs=[
            pl.BlockSpec((tm, yr.shape[1]), lambda i: (i, 0)),
            pl.BlockSpec((tm, yf.shape[1]), lambda i: (i, 0)),
            pl.BlockSpec((tm, D_MODEL), lambda i: (i, 0)),
            pl.BlockSpec(w_out.shape, lambda i: (0, 0)),
            pl.BlockSpec((1, D_MODEL), lambda i: (0, 0)),
        ],
        out_specs=pl.BlockSpec((tm, D_MODEL), lambda i: (i, 0)),
        out_shape=jax.ShapeDtypeStruct((rows, D_MODEL), _F32),
        compiler_params=pltpu.CompilerParams(
            dimension_semantics=("arbitrary",),
            vmem_limit_bytes=VMEM_LIMIT),
        name="outproj",
    )(yr, yf, x2d, w_out, g)


def _rope_tables(pos):
    inv = ROPE_BASE ** (-jnp.arange(0, RET_DK, 2, dtype=_F32) / RET_DK)
    ang = pos[:, None] * inv[None, :]
    cos, sin = jnp.cos(ang), jnp.sin(ang)
    return jnp.concatenate([cos, cos], axis=-1), jnp.concatenate([-sin, sin], axis=-1)


def kernel(x, meta_tokens, norm_g, w_in, b_f, w_out, final_g):
    b, s, d = x.shape
    assert norm_g.shape[0] == 1 and d == D_MODEL and s % FOX_TQ == 0
    x2d = x.reshape(b * s, d)
    col_scale = jnp.ones((D_MAIN, 1), _F32)
    col_scale = col_scale.at[COL_RK * LANES:COL_RV * LANES].set(RET_DK ** -0.5)
    col_scale = col_scale.at[COL_FQ * LANES:COL_FK * LANES].set(FOX_HEAD_DIM ** -0.5 * LOG2E)
    w_all_t = jnp.swapaxes(w_in, 1, 2)[0]
    w_ff_t = w_all_t[D_MAIN:, :]
    g = norm_g[0].reshape(1, d)
    bf = b_f[0].reshape(FOX_HEADS, 1)
    meta_pad = jnp.pad(meta_tokens.astype(_F32), ((0, CHUNK - N_META), (0, 0)))
    cos2, sin2 = _rope_tables(jnp.arange(s, dtype=_F32) + float(N_META))
    cosm, sinm = _rope_tables(jnp.arange(CHUNK, dtype=_F32))

    zm, cm, w_t = _inproj(meta_pad, g, w_all_t, col_scale, w_ff_t, bf, cosm, sinm,
                          tm=CHUNK, tn=1792, n_valid=N_META)
    z, c, _ = _inproj(x2d, g, w_t, col_scale, w_ff_t, bf, cos2, sin2, tm=s, tn=1792, n_valid=s)

    z3 = z.reshape(b, s, D_MAIN)
    lg = jnp.log1p(-jnp.exp2(-5.0 - jnp.arange(RET_HEADS, dtype=_F32)))
    y_r = _retention(lg, z3, zm)

    c4 = c.reshape(b, FOX_HEADS // 2, 2, s)
    cm3 = cm.reshape(FOX_HEADS // 2, 2, CHUNK)
    y_f = _fox(z3, c4, zm, cm3)

    out = _outproj(y_r.reshape(b * s, -1), y_f.reshape(b * s, -1), x2d,
                   w_out[0].astype(_BF16), final_g.reshape(1, d), tm=1024)
    return out.reshape(b, s, d)
```

```python
import functools

import jax
import jax.numpy as jnp
import numpy as np
from jax import lax
from jax.experimental import pallas as pl
from jax.experimental.pallas import tpu as pltpu

D_MODEL = 1024
N_META = 16
CHUNK = 128
RET_HEADS = 4
RET_DK = 128
RET_DV = 256
FOX_HEADS = 16
FOX_HEAD_DIM = 64
ROPE_BASE = 10000.0
EPS = 1e-6
NEG_INF = -1e30
LOG2E = 1.4426950408889634

LANES = 128
SUBLANES = 8
BF16_SUBLANES = 16
D_MAIN = 7168
COL_RQ, COL_RK, COL_RV, COL_RG = 0, 4, 8, 16
COL_FQ, COL_FK, COL_FV, COL_FG = 24, 32, 40, 48
FOX_TQ = 256
FOX_TK = 256
FOX_PAIRS = 2
FOX_ACC_ROWS = FOX_HEAD_DIM + BF16_SUBLANES
FOX_BIAS_PIECES = 3
PROJ_SUB = 512
OUT_SUB = 256
RET_HEADS_PER_STEP = 4
VMEM_LIMIT = 56 * 1024 * 1024

_F32 = jnp.float32
_BF16 = jnp.bfloat16


def _dot(a, b):
    return jnp.dot(a, b, preferred_element_type=_F32)


def _dot_nt(a, b):
    return lax.dot_general(a, b, (((1,), (1,)), ((), ())), preferred_element_type=_F32)


def _silu(g):
    h = 0.5 * g
    return h + h * jnp.tanh(h)


def _rotary(x, cos2, sin2):
    return x * cos2 + pltpu.roll(x, RET_DK // 2, axis=1) * sin2


def _inproj_kernel(x_ref, g_ref, w_ref, cs_ref, wff_ref, bf_ref, cos_ref, sin_ref,
                   z_ref, c_ref, wcast_ref, u_scr, lf_scr, *, tm, tn, sub, n_valid, cast_weights):
    j = pl.program_id(1)
    n_sub = tm // sub

    def weight_tile():
        if not cast_weights:
            wcast_ref[...] = jnp.zeros(wcast_ref.shape, _BF16)
            return w_ref[...]
        wcast_ref[...] = (w_ref[...] * cs_ref[...]).astype(_BF16)
        return wcast_ref[...]

    def normalize(r):
        rows = pl.ds(r * sub, sub)
        xf = x_ref[rows, :]
        ms = jnp.mean(xf * xf, axis=-1, keepdims=True)
        u = (xf * lax.rsqrt(ms + EPS) * g_ref[...]).astype(_BF16)
        u_scr[rows, :] = u
        lf_scr[:, r * sub:(r + 1) * sub] = _dot_nt(wff_ref[...].astype(_BF16), u)

    def project(r, w_tile, rot_cols):
        rows = pl.ds(r * sub, sub)
        acc = _dot_nt(u_scr[rows, :], w_tile)
        if rot_cols:
            cos2, sin2 = cos_ref[rows, :], sin_ref[rows, :]
            slabs = [_rotary(acc[:, a:a + RET_DK], cos2, sin2) for a in range(0, rot_cols, RET_DK)]
            acc = jnp.concatenate(slabs + [acc[:, rot_cols:]], axis=1) if rot_cols < tn else (
                jnp.concatenate(slabs, axis=1))
        z_ref[rows, :] = acc.astype(_BF16)

    def forget_cumsum():
        lane = lax.broadcasted_iota(jnp.int32, (FOX_HEADS, LANES), 1)
        carry = jnp.zeros((FOX_HEADS, 1), _F32)
        bf = bf_ref[...]
        for ci in range(tm // LANES):
            v = lf_scr[:, ci * LANES:(ci + 1) * LANES] + bf
            blk = jnp.minimum(v, 0.0) - jnp.log1p(jnp.exp(-jnp.abs(v)))
            if (ci + 1) * LANES > n_valid:
                blk = jnp.where(lane + ci * LANES < n_valid, blk, 0.0)
            sh = 1
            while sh < LANES:
                rolled = pltpu.roll(blk, sh, axis=1)
                blk = blk + jnp.where(lane >= sh, rolled, 0.0)
                sh *= 2
            blk = blk + carry
            for p in range(FOX_HEADS // 2):
                c_ref[0, p, :, ci * LANES:(ci + 1) * LANES] = blk[2 * p:2 * p + 2, :]
            carry = blk[:, LANES - 1:LANES]

    rot_end = COL_RV * LANES
    rot_tiles = -(-rot_end // tn)

    for jj in range(rot_tiles):
        @pl.when(j == jj)
        def _(jj=jj):
            w_tile = weight_tile()
            for r in range(n_sub):
                if jj == 0:
                    normalize(r)
                project(r, w_tile, min(tn, rot_end - jj * tn))
            if jj == 0:
                forget_cumsum()

    @pl.when(j >= rot_tiles)
    def _():
        w_tile = weight_tile()
        for r in range(n_sub):
            project(r, w_tile, 0)


def _inproj(x2d, g, w_t, col_scale, w_ff_t, b_f, cos2, sin2, *, tm, tn, n_valid):
    rows = x2d.shape[0]
    sub = min(tm, PROJ_SUB)
    assert D_MAIN % tn == 0 and tn % RET_DK == 0
    cast = w_t.dtype != _BF16
    wrows = tn if cast else SUBLANES
    kern = functools.partial(_inproj_kernel, tm=tm, tn=tn, sub=sub, n_valid=n_valid,
                             cast_weights=cast)
    return pl.pallas_call(
        kern,
        grid=(rows // tm, D_MAIN // tn),
        in_specs=[
            pl.BlockSpec((tm, D_MODEL), lambda i, j: (i, 0)),
            pl.BlockSpec((1, D_MODEL), lambda i, j: (0, 0)),
            pl.BlockSpec((tn, D_MODEL), lambda i, j: (j, 0)),
            pl.BlockSpec((wrows, 1), lambda i, j: (j if cast else 0, 0)),
            pl.BlockSpec((FOX_HEADS, D_MODEL), lambda i, j: (0, 0)),
            pl.BlockSpec((FOX_HEADS, 1), lambda i, j: (0, 0)),
            pl.BlockSpec((tm, RET_DK), lambda i, j: (0, 0)),
            pl.BlockSpec((tm, RET_DK), lambda i, j: (0, 0)),
        ],
        out_specs=[
            pl.BlockSpec((tm, tn), lambda i, j: (i, j)),
            pl.BlockSpec((1, FOX_HEADS // 2, 2, tm), lambda i, j: (i, 0, 0, 0)),
            pl.BlockSpec((wrows, D_MODEL), lambda i, j: (j if cast else 0, 0)),
        ],
        out_shape=[
            jax.ShapeDtypeStruct((rows, D_MAIN), _BF16),
            jax.ShapeDtypeStruct((rows // tm, FOX_HEADS // 2, 2, tm), _F32),
            jax.ShapeDtypeStruct((D_MAIN if cast else SUBLANES, D_MODEL), _BF16),
        ],
        scratch_shapes=[
            pltpu.VMEM((tm, D_MODEL), _BF16),
            pltpu.VMEM((FOX_HEADS, tm), _F32),
        ],
        compiler_params=pltpu.CompilerParams(
            dimension_semantics=("arbitrary", "arbitrary"),
            vmem_limit_bytes=VMEM_LIMIT),
        name="inproj",
    )(x2d, g, w_t, col_scale, w_ff_t, b_f, cos2, sin2)


def _retention_kernel(lg_ref, q_ref, k_ref, v_ref, g_ref, km_ref, vm_ref, o_ref,
                      *, n_chunks, heads_per_step):
    ri = lax.broadcasted_iota(jnp.int32, (CHUNK, CHUNK), 0)
    ci = lax.broadcasted_iota(jnp.int32, (CHUNK, CHUNK), 1)
    diff = (ri - ci).astype(_F32)
    idx = lax.broadcasted_iota(jnp.int32, (CHUNK, 1), 0).astype(_F32)

    def one_head(hd):
        lg = lg_ref[pl.program_id(1) * heads_per_step + hd]
        kcols = slice(hd * RET_DK, (hd + 1) * RET_DK)
        vcols = slice(hd * RET_DV, (hd + 1) * RET_DV)
        dmask = jnp.where(diff >= 0, jnp.exp(lg * jnp.maximum(diff, 0.0)), 0.0)
        xi = jnp.exp(lg * (idx + 1.0))
        zeta = jnp.exp(lg * (CHUNK - 1.0 - idx))
        chunk_decay = jnp.exp(lg * jnp.full((1, 1), float(CHUNK), _F32))

        zeta_m = jnp.exp(lg * (N_META - 1.0 - idx))
        km = km_ref[:, kcols].astype(_F32)
        state = _dot((km * zeta_m).T.astype(_BF16), vm_ref[:, vcols])

        def chunk_products(n):
            rows = slice(n * CHUNK, (n + 1) * CHUNK)
            kb = k_ref[0, rows, kcols]
            v = v_ref[0, rows, vcols]
            scores = (_dot_nt(q_ref[0, rows, kcols], kb) * dmask).astype(_BF16)
            kv = _dot((kb.astype(_F32) * zeta).T.astype(_BF16), v)
            return scores, kv

        def chunk_output(n, scores, state):
            rows = slice(n * CHUNK, (n + 1) * CHUNK)
            out = _dot(scores, v_ref[0, rows, vcols])
            out = out + _dot(q_ref[0, rows, kcols], state.astype(_BF16)) * xi
            y = out * lax.rsqrt(jnp.mean(out * out, axis=-1, keepdims=True) + EPS)
            gate = g_ref[0, rows, vcols].astype(_F32)
            o_ref[0, rows, vcols] = (y * _silu(gate)).astype(_BF16)

        scores, kv = chunk_products(0)
        for n in range(n_chunks):
            if n + 1 < n_chunks:
                nxt = chunk_products(n + 1)
            chunk_output(n, scores, state)
            state = state * chunk_decay + kv
            if n + 1 < n_chunks:
                scores, kv = nxt

    for hd in range(heads_per_step):
        one_head(hd)


def _retention(lg, z3, zm):
    b, s, _ = z3.shape
    hps = RET_HEADS_PER_STEP
    wk, wv = hps * RET_DK, hps * RET_DV
    kern = functools.partial(_retention_kernel, n_chunks=s // CHUNK, heads_per_step=hps)
    return pl.pallas_call(
        kern,
        grid=(b, RET_HEADS // hps),
        in_specs=[
            pl.BlockSpec(memory_space=pltpu.SMEM),
            pl.BlockSpec((1, s, wk), lambda i, h: (i, 0, COL_RQ * LANES // wk + h)),
            pl.BlockSpec((1, s, wk), lambda i, h: (i, 0, COL_RK * LANES // wk + h)),
            pl.BlockSpec((1, s, wv), lambda i, h: (i, 0, COL_RV * LANES // wv + h)),
            pl.BlockSpec((1, s, wv), lambda i, h: (i, 0, COL_RG * LANES // wv + h)),
            pl.BlockSpec((CHUNK, wk), lambda i, h: (0, COL_RK * LANES // wk + h)),
            pl.BlockSpec((CHUNK, wv), lambda i, h: (0, COL_RV * LANES // wv + h)),
        ],
        out_specs=pl.BlockSpec((1, s, wv), lambda i, h: (i, 0, h)),
        out_shape=jax.ShapeDtypeStruct((b, s, RET_HEADS * RET_DV), _BF16),
        compiler_params=pltpu.CompilerParams(
            dimension_semantics=("arbitrary", "arbitrary"),
            vmem_limit_bytes=VMEM_LIMIT),
        name="retention",
    )(lg, z3, z3, z3, z3, zm, zm)


def _fox_kernel(q_ref, k_ref, v_ref, g_ref, c_ref, km_ref, vm_ref, cm_ref, o_ref,
                vt_scr, vtm_scr, ka_scr, kam_scr, qt_scr, acc_scr, m_scr, s_scr, pm_scr,
                sm_scr, pmm_scr, *, n_tiles, n_pairs, seq):
    half = FOX_HEAD_DIM
    n_heads = 2 * n_pairs
    sub = lax.broadcasted_iota(jnp.int32, (LANES, 1), 0)
    sub8 = lax.broadcasted_iota(jnp.int32, (SUBLANES, 1), 0)
    lane = lax.broadcasted_iota(jnp.int32, (1, LANES), 1)

    def pieces8(row):
        out = jnp.zeros((SUBLANES, LANES), _F32)
        rest = row
        for piece in range(FOX_BIAS_PIECES):
            part = rest.astype(_BF16).astype(_F32)
            out = jnp.where(sub8 == piece, part, out)
            rest = rest - part
        return out

    def augmented_keys(kblk, bias_lo, bias_hi):
        gap = jnp.zeros((half - SUBLANES, LANES), _F32)
        tile = jnp.concatenate([pieces8(bias_hi), gap, pieces8(bias_lo), gap], axis=0).T
        tile = tile.astype(_BF16)[0:kblk.shape[0], :]
        return jnp.where(lane < half, kblk, tile), jnp.where(lane < half, tile, kblk)

    def build_meta_operands():
        for p in range(n_pairs):
            lanes = slice(p * LANES, (p + 1) * LANES)
            cm0, cm1 = cm_ref[p, 0:1, :], cm_ref[p, 1:2, :]
            kam_scr[2 * p], kam_scr[2 * p + 1] = augmented_keys(
                km_ref[0:N_META, lanes],
                (cm0[:, LANES - 1:LANES] - cm0) * LOG2E, (cm1[:, LANES - 1:LANES] - cm1) * LOG2E)
            vtm = vm_ref[:, lanes].T
            for hh in range(2):
                h = 2 * p + hh
                vtm_scr[h, 0:half, :] = vtm[hh * half:(hh + 1) * half, :]
                vtm_scr[h, half:FOX_ACC_ROWS, :] = jnp.ones((FOX_ACC_ROWS - half, LANES), _BF16)

    def build_block_operands(jb):
        rows = slice(jb * FOX_TK, (jb + 1) * FOX_TK)
        for p in range(n_pairs):
            lanes = slice(p * LANES, (p + 1) * LANES)
            for cb in range(jb * FOX_TK // LANES, (jb + 1) * FOX_TK // LANES):
                ks = slice(cb * LANES, (cb + 1) * LANES)
                ka_scr[2 * p, ks, :], ka_scr[2 * p + 1, ks, :] = augmented_keys(
                    k_ref[0, ks, lanes],
                    c_ref[0, p, 0:1, ks] * -LOG2E, c_ref[0, p, 1:2, ks] * -LOG2E)
            vt = v_ref[0, rows, lanes].T
            for hh in range(2):
                h = 2 * p + hh
                vt_scr[h, jb, 0:half, :] = vt[hh * half:(hh + 1) * half, :]
                vt_scr[h, jb, half:FOX_ACC_ROWS, :] = jnp.ones(
                    (FOX_ACC_ROWS - half, FOX_TK), _BF16)

    ri = lax.broadcasted_iota(jnp.int32, (FOX_TK, FOX_TQ), 0)
    ci = lax.broadcasted_iota(jnp.int32, (FOX_TK, FOX_TQ), 1)
    causal = ri <= ci

    def group_max(st):
        parts = [st[r:r + SUBLANES, :] for r in range(0, st.shape[0], SUBLANES)]
        while len(parts) > 1:
            nxt = [jnp.maximum(parts[a], parts[a + 1]) for a in range(0, len(parts) - 1, 2)]
            if len(parts) % 2:
                nxt.append(parts[-1])
            parts = nxt
        return parts[0]

    def update(h, tile, first, st, pm, vt):
        buf = tile % 2
        m_blk = jnp.max(pm, axis=0, keepdims=True)
        if first:
            m_new = m_blk
        else:
            m_old = m_scr[buf, h]
            m_new = jnp.maximum(m_old, m_blk)
            alpha = jnp.exp2(m_old - m_new)
        pt = jnp.exp2(st - m_new).astype(_BF16)
        if pt.shape[0] < LANES:
            pt = jnp.concatenate(
                [pt, jnp.zeros((LANES - pt.shape[0], FOX_TQ), _BF16)], axis=0)
        pv = _dot(vt, pt)
        acc_scr[buf, h] = pv if first else acc_scr[buf, h] * alpha + pv
        m_scr[buf, h] = m_new

    def prepare_queries(i):
        rows = slice(i * FOX_TQ, (i + 1) * FOX_TQ)
        for p in range(n_pairs):
            qt = q_ref[0, rows, p * LANES:(p + 1) * LANES].astype(_F32).T
            ones_hi = jnp.where(sub < half + FOX_BIAS_PIECES, 1.0, 0.0)
            ones_lo = jnp.where(sub < FOX_BIAS_PIECES, 1.0, 0.0)
            qt_scr[i % 2, 2 * p] = jnp.where(sub < half, qt, ones_hi).astype(_BF16)
            qt_scr[i % 2, 2 * p + 1] = jnp.where(sub < half, ones_lo, qt).astype(_BF16)

    def scores(item, heads):
        kind, i, j, slot = item
        if kind == "meta":
            for h in heads:
                st = _dot(kam_scr[h], qt_scr[i % 2, h])
                sm_scr[h] = st
                pmm_scr[h] = group_max(st)
            return
        krows = slice(j * FOX_TK, (j + 1) * FOX_TK)
        for h in heads:
            st = _dot(ka_scr[h, krows, :], qt_scr[i % 2, h])
            if i == j:
                st = jnp.where(causal, st, NEG_INF)
            s_scr[slot, h] = st
            pm_scr[slot, h] = group_max(st)

    def consume(item, heads):
        kind, i, j, slot = item
        for h in heads:
            if kind == "meta":
                update(h, i, True, sm_scr[h], pmm_scr[h], vtm_scr[h])
            else:
                update(h, i, False, s_scr[slot, h], pm_scr[slot, h], vt_scr[h, j])

    def finalize(i):
        rows = slice(i * FOX_TQ, (i + 1) * FOX_TQ)
        for p in range(n_pairs):
            lanes = slice(p * LANES, (p + 1) * LANES)
            outs = []
            for hh in range(2):
                acc = acc_scr[i % 2, 2 * p + hh]
                outs.append(acc[0:half, :] / acc[half:half + 1, :])
            y = jnp.concatenate(outs, axis=0).T
            gate = g_ref[0, rows, lanes].astype(_F32)
            o_ref[0, rows, lanes] = (y * _silu(gate)).astype(_BF16)

    items = []
    for i in range(n_tiles):
        items.append(("meta", i, None, None))
        for j in [i] + list(range(i)):
            items.append(("blk", i, j, sum(it[0] == "blk" for it in items) % 2))
    build_meta_operands()
    build_block_operands(0)
    prepare_queries(0)
    scores(items[0], range(n_heads))
    for t, item in enumerate(items):
        if t + 1 < len(items):
            nxt = items[t + 1]
            if nxt[0] == "meta":
                build_block_operands(nxt[1])
                prepare_queries(nxt[1])
            scores(nxt, range(n_heads))
        consume(item, range(n_heads))
        if t + 1 == len(items) or items[t + 1][1] != item[1]:
            finalize(item[1])


def _fox(z3, c4, zm, cm3):
    b, s, _ = z3.shape
    n_pairs = FOX_PAIRS
    n_groups = FOX_HEADS // 2 // n_pairs
    n_tiles = s // FOX_TQ
    w = n_pairs * LANES
    kern = functools.partial(_fox_kernel, n_tiles=n_tiles, n_pairs=n_pairs, seq=s)
    return pl.pallas_call(
        kern,
        grid=(b, n_groups),
        in_specs=[
            pl.BlockSpec((1, s, w), lambda i, p: (i, 0, COL_FQ // n_pairs + p)),
            pl.BlockSpec((1, s, w), lambda i, p: (i, 0, COL_FK // n_pairs + p)),
            pl.BlockSpec((1, s, w), lambda i, p: (i, 0, COL_FV // n_pairs + p)),
            pl.BlockSpec((1, s, w), lambda i, p: (i, 0, COL_FG // n_pairs + p)),
            pl.BlockSpec((1, n_pairs, 2, s), lambda i, p: (i, p, 0, 0)),
            pl.BlockSpec((CHUNK, w), lambda i, p: (0, COL_FK // n_pairs + p)),
            pl.BlockSpec((CHUNK, w), lambda i, p: (0, COL_FV // n_pairs + p)),
            pl.BlockSpec((n_pairs, 2, LANES), lambda i, p: (p, 0, 0)),
        ],
        out_specs=pl.BlockSpec((1, s, w), lambda i, p: (i, 0, p)),
        out_shape=jax.ShapeDtypeStruct((b, s, FOX_HEADS * FOX_HEAD_DIM), _BF16),
        scratch_shapes=[
            pltpu.VMEM((2 * n_pairs, n_tiles, FOX_ACC_ROWS, FOX_TK), _BF16),
            pltpu.VMEM((2 * n_pairs, FOX_ACC_ROWS, LANES), _BF16),
            pltpu.VMEM((2 * n_pairs, s, LANES), _BF16),
            pltpu.VMEM((2 * n_pairs, N_META, LANES), _BF16),
            pltpu.VMEM((2, 2 * n_pairs, LANES, FOX_TQ), _BF16),
            pltpu.VMEM((2, 2 * n_pairs, FOX_ACC_ROWS, FOX_TQ), _F32),
            pltpu.VMEM((2, 2 * n_pairs, 1, FOX_TQ), _F32),
            pltpu.VMEM((2, 2 * n_pairs, FOX_TK, FOX_TQ), _F32),
            pltpu.VMEM((2, 2 * n_pairs, SUBLANES, FOX_TQ), _F32),
            pltpu.VMEM((2 * n_pairs, N_META, FOX_TQ), _F32),
            pltpu.VMEM((2 * n_pairs, SUBLANES, FOX_TQ), _F32),
        ],
        compiler_params=pltpu.CompilerParams(
            dimension_semantics=("arbitrary", "arbitrary"),
            vmem_limit_bytes=VMEM_LIMIT),
        name="fox",
    )(z3, z3, z3, z3, c4, zm, zm, cm3)


def _outproj_kernel(yr_ref, yf_ref, x_ref, w_ref, g_ref, o_ref):
    half = w_ref.shape[0] // 2
    for r in range(x_ref.shape[0] // OUT_SUB):
        rows = pl.ds(r * OUT_SUB, OUT_SUB)
        hres = (x_ref[rows, :] + _dot(yr_ref[rows, :], w_ref[:half, :])
                + _dot(yf_ref[rows, :], w_ref[half:, :]))
        ms = jnp.mean(hres * hres, axis=-1, keepdims=True)
        o_ref[rows, :] = hres * lax.rsqrt(ms + EPS) * g_ref[...]


def _outproj(yr, yf, x2d, w_out, g, *, tm):
    rows = x2d.shape[0]
    return pl.pallas_call(
        _outproj_kernel,
        grid=(rows // tm,),
        in_specs=[
            pl.BlockSpec((tm, yr.shape[1]), lambda i: (i, 0)),
            pl.BlockSpec((tm, yf.shape[1]), lambda i: (i, 0)),
            pl.BlockSpec((tm, D_MODEL), lambda i: (i, 0)),
            pl.BlockSpec(w_out.shape, lambda i: (0, 0)),
            pl.BlockSpec((1, D_MODEL), lambda i: (0, 0)),
        ],
        out_specs=pl.BlockSpec((tm, D_MODEL), lambda i: (i, 0)),
        out_shape=jax.ShapeDtypeStruct((rows, D_MODEL), _F32),
        compiler_params=pltpu.CompilerParams(
            dimension_semantics=("arbitrary",),
            vmem_limit_bytes=VMEM_LIMIT),
        name="outproj",
    )(yr, yf, x2d, w_out, g)


def _rope_tables(first_pos, n):
    pos = np.arange(n, dtype=np.float32) + np.float32(first_pos)
    inv = np.float32(ROPE_BASE) ** (-np.arange(0, RET_DK, 2, dtype=np.float32) / np.float32(RET_DK))
    ang = pos[:, None] * inv[None, :]
    cos, sin = np.cos(ang), np.sin(ang)
    return np.concatenate([cos, cos], axis=-1), np.concatenate([-sin, sin], axis=-1)


def kernel(x, meta_tokens, norm_g, w_in, b_f, w_out, final_g):
    b, s, d = x.shape
    assert norm_g.shape[0] == 1 and d == D_MODEL and s % FOX_TQ == 0
    x2d = x.reshape(b * s, d)
    col_scale = jnp.ones((D_MAIN, 1), _F32)
    col_scale = col_scale.at[COL_RK * LANES:COL_RV * LANES].set(RET_DK ** -0.5)
    col_scale = col_scale.at[COL_FQ * LANES:COL_FK * LANES].set(FOX_HEAD_DIM ** -0.5 * LOG2E)
    w_all_t = jnp.swapaxes(w_in, 1, 2)[0]
    w_ff_t = w_all_t[D_MAIN:, :]
    g = norm_g[0].reshape(1, d)
    bf = b_f[0].reshape(FOX_HEADS, 1)
    meta_pad = jnp.pad(meta_tokens.astype(_F32), ((0, CHUNK - N_META), (0, 0)))
    cos2, sin2 = _rope_tables(N_META, s)
    cosm, sinm = _rope_tables(0, CHUNK)

    zm, cm, w_t = _inproj(meta_pad, g, w_all_t, col_scale, w_ff_t, bf, cosm, sinm,
                          tm=CHUNK, tn=1792, n_valid=N_META)
    z, c, _ = _inproj(x2d, g, w_t, col_scale, w_ff_t, bf, cos2, sin2, tm=s, tn=1792, n_valid=s)

    z3 = z.reshape(b, s, D_MAIN)
    lg = jnp.log1p(-jnp.exp2(-5.0 - jnp.arange(RET_HEADS, dtype=_F32)))
    y_r = _retention(lg, z3, zm)

    y_f = _fox(z3, c, zm, cm[0])

    out = _outproj(y_r.reshape(b * s, -1), y_f.reshape(b * s, -1), x2d,
                   w_out[0].astype(_BF16), final_g.reshape(1, d), tm=1024)
    return out.reshape(b, s, d)
```

```python
import functools

import jax
import jax.numpy as jnp
import numpy as np
from jax import lax
from jax.experimental import pallas as pl
from jax.experimental.pallas import tpu as pltpu

D_MODEL = 1024
N_META = 16
CHUNK = 128
RET_HEADS = 4
RET_DK = 128
RET_DV = 256
FOX_HEADS = 16
FOX_HEAD_DIM = 64
ROPE_BASE = 10000.0
EPS = 1e-6
NEG_INF = -1e30
LOG2E = 1.4426950408889634

LANES = 128
SUBLANES = 8
BF16_SUBLANES = 16
D_MAIN = 7168
COL_RQ, COL_RK, COL_RV, COL_RG = 0, 4, 8, 16
COL_FQ, COL_FK, COL_FV, COL_FG = 24, 32, 40, 48
FOX_TQ = 256
FOX_TK = 256
FOX_PAIRS = 2
FOX_ACC_ROWS = FOX_HEAD_DIM + BF16_SUBLANES
FOX_BIAS_PIECES = 3
PROJ_SUB = 512
OUT_SUB = 256
RET_HEADS_PER_STEP = 4
VMEM_LIMIT = 56 * 1024 * 1024

_F32 = jnp.float32
_BF16 = jnp.bfloat16


def _dot(a, b):
    return jnp.dot(a, b, preferred_element_type=_F32)


def _dot_nt(a, b):
    return lax.dot_general(a, b, (((1,), (1,)), ((), ())), preferred_element_type=_F32)


def _silu(g):
    h = 0.5 * g
    return h + h * jnp.tanh(h)


def _rotary(x, cos2, sin2):
    return x * cos2 + pltpu.roll(x, RET_DK // 2, axis=1) * sin2


def _inproj_kernel(x_ref, g_ref, w_ref, cs_ref, wff_ref, bf_ref, cos_ref, sin_ref,
                   z_ref, c_ref, wcast_ref, u_scr, lf_scr, *, tm, tn, sub, n_valid, cast_weights):
    j = pl.program_id(1)
    n_sub = tm // sub

    def weight_tile():
        if not cast_weights:
            wcast_ref[...] = jnp.zeros(wcast_ref.shape, _BF16)
            return w_ref[...]
        wcast_ref[...] = (w_ref[...] * cs_ref[...]).astype(_BF16)
        return wcast_ref[...]

    def normalize(r):
        lo = r * sub
        n = max(0, min(sub, x_ref.shape[0] - lo))
        if n:
            xf = x_ref[lo:lo + n, :]
            ms = jnp.mean(xf * xf, axis=-1, keepdims=True)
            u_scr[lo:lo + n, :] = (xf * lax.rsqrt(ms + EPS) * g_ref[...]).astype(_BF16)
        if n < sub:
            u_scr[lo + n:lo + sub, :] = jnp.zeros((sub - n, D_MODEL), _BF16)
        lf_scr[:, lo:lo + sub] = _dot_nt(wff_ref[...].astype(_BF16), u_scr[lo:lo + sub, :])

    def project(r, w_tile, rot_cols):
        rows = pl.ds(r * sub, sub)
        acc = _dot_nt(u_scr[rows, :], w_tile)
        if rot_cols:
            cos2, sin2 = cos_ref[rows, :], sin_ref[rows, :]
            slabs = [_rotary(acc[:, a:a + RET_DK], cos2, sin2) for a in range(0, rot_cols, RET_DK)]
            acc = jnp.concatenate(slabs + [acc[:, rot_cols:]], axis=1) if rot_cols < tn else (
                jnp.concatenate(slabs, axis=1))
        z_ref[rows, :] = acc.astype(_BF16)

    def forget_cumsum():
        lane = lax.broadcasted_iota(jnp.int32, (FOX_HEADS, LANES), 1)
        carry = jnp.zeros((FOX_HEADS, 1), _F32)
        bf = bf_ref[...]
        for ci in range(tm // LANES):
            v = lf_scr[:, ci * LANES:(ci + 1) * LANES] + bf
            blk = jnp.minimum(v, 0.0) - jnp.log1p(jnp.exp(-jnp.abs(v)))
            if (ci + 1) * LANES > n_valid:
                blk = jnp.where(lane + ci * LANES < n_valid, blk, 0.0)
            sh = 1
            while sh < LANES:
                rolled = pltpu.roll(blk, sh, axis=1)
                blk = blk + jnp.where(lane >= sh, rolled, 0.0)
                sh *= 2
            blk = blk + carry
            for p in range(FOX_HEADS // 2):
                c_ref[0, p, :, ci * LANES:(ci + 1) * LANES] = blk[2 * p:2 * p + 2, :]
            carry = blk[:, LANES - 1:LANES]

    rot_end = COL_RV * LANES
    rot_tiles = -(-rot_end // tn)

    for jj in range(rot_tiles):
        @pl.when(j == jj)
        def _(jj=jj):
            w_tile = weight_tile()
            for r in range(n_sub):
                if jj == 0:
                    normalize(r)
                project(r, w_tile, min(tn, rot_end - jj * tn))
            if jj == 0:
                forget_cumsum()

    @pl.when(j >= rot_tiles)
    def _():
        w_tile = weight_tile()
        for r in range(n_sub):
            project(r, w_tile, 0)


def _inproj(x2d, g, w_t, col_scale, w_all_t, b_f, cos2, sin2, *, tm, tn, n_valid):
    x_rows = min(tm, x2d.shape[0])
    assert x2d.shape[0] % x_rows == 0 and x_rows % BF16_SUBLANES == 0
    rows = (x2d.shape[0] // x_rows) * tm
    sub = min(tm, PROJ_SUB)
    assert D_MAIN % tn == 0 and tn % RET_DK == 0 and D_MAIN % FOX_HEADS == 0
    cast = w_t.dtype != _BF16
    wrows = tn if cast else SUBLANES
    kern = functools.partial(_inproj_kernel, tm=tm, tn=tn, sub=sub, n_valid=n_valid,
                             cast_weights=cast)
    return pl.pallas_call(
        kern,
        grid=(rows // tm, D_MAIN // tn),
        in_specs=[
            pl.BlockSpec((x_rows, D_MODEL), lambda i, j: (i, 0)),
            pl.BlockSpec((1, D_MODEL), lambda i, j: (0, 0)),
            pl.BlockSpec((tn, D_MODEL), lambda i, j: (j, 0)),
            pl.BlockSpec((wrows, 1), lambda i, j: (j if cast else 0, 0)),
            pl.BlockSpec((FOX_HEADS, D_MODEL), lambda i, j: (D_MAIN // FOX_HEADS, 0)),
            pl.BlockSpec((FOX_HEADS, 1), lambda i, j: (0, 0)),
            pl.BlockSpec((tm, RET_DK), lambda i, j: (0, 0)),
            pl.BlockSpec((tm, RET_DK), lambda i, j: (0, 0)),
        ],
        out_specs=[
            pl.BlockSpec((tm, tn), lambda i, j: (i, j)),
            pl.BlockSpec((1, FOX_HEADS // 2, 2, tm), lambda i, j: (i, 0, 0, 0)),
            pl.BlockSpec((wrows, D_MODEL), lambda i, j: (j if cast else 0, 0)),
        ],
        out_shape=[
            jax.ShapeDtypeStruct((rows, D_MAIN), _BF16),
            jax.ShapeDtypeStruct((rows // tm, FOX_HEADS // 2, 2, tm), _F32),
            jax.ShapeDtypeStruct((D_MAIN if cast else SUBLANES, D_MODEL), _BF16),
        ],
        scratch_shapes=[
            pltpu.VMEM((tm, D_MODEL), _BF16),
            pltpu.VMEM((FOX_HEADS, tm), _F32),
        ],
        compiler_params=pltpu.CompilerParams(
            dimension_semantics=("arbitrary", "arbitrary"),
            vmem_limit_bytes=VMEM_LIMIT),
        name="inproj",
    )(x2d, g, w_t, col_scale, w_all_t, b_f, cos2, sin2)


def _retention_kernel(lg_ref, q_ref, k_ref, v_ref, g_ref, km_ref, vm_ref, o_ref,
                      *, n_chunks, heads_per_step):
    ri = lax.broadcasted_iota(jnp.int32, (CHUNK, CHUNK), 0)
    ci = lax.broadcasted_iota(jnp.int32, (CHUNK, CHUNK), 1)
    diff = (ri - ci).astype(_F32)
    idx = lax.broadcasted_iota(jnp.int32, (CHUNK, 1), 0).astype(_F32)

    def one_head(hd):
        lg = lg_ref[pl.program_id(1) * heads_per_step + hd]
        kcols = slice(hd * RET_DK, (hd + 1) * RET_DK)
        vcols = slice(hd * RET_DV, (hd + 1) * RET_DV)
        dmask = jnp.where(diff >= 0, jnp.exp(lg * jnp.maximum(diff, 0.0)), 0.0)
        xi = jnp.exp(lg * (idx + 1.0))
        zeta = jnp.exp(lg * (CHUNK - 1.0 - idx))
        chunk_decay = jnp.exp(lg * jnp.full((1, 1), float(CHUNK), _F32))

        zeta_m = jnp.exp(lg * (N_META - 1.0 - idx))
        km = km_ref[:, kcols].astype(_F32)
        state = _dot((km * zeta_m).T.astype(_BF16), vm_ref[:, vcols])

        def chunk_products(n):
            rows = slice(n * CHUNK, (n + 1) * CHUNK)
            kb = k_ref[0, rows, kcols]
            v = v_ref[0, rows, vcols]
            scores = (_dot_nt(q_ref[0, rows, kcols], kb) * dmask).astype(_BF16)
            kv = _dot((kb.astype(_F32) * zeta).T.astype(_BF16), v)
            return scores, kv

        def chunk_output(n, scores, state):
            rows = slice(n * CHUNK, (n + 1) * CHUNK)
            out = _dot(scores, v_ref[0, rows, vcols])
            out = out + _dot(q_ref[0, rows, kcols], state.astype(_BF16)) * xi
            y = out * lax.rsqrt(jnp.mean(out * out, axis=-1, keepdims=True) + EPS)
            gate = g_ref[0, rows, vcols].astype(_F32)
            o_ref[0, rows, vcols] = (y * _silu(gate)).astype(_BF16)

        scores, kv = chunk_products(0)
        for n in range(n_chunks):
            if n + 1 < n_chunks:
                nxt = chunk_products(n + 1)
            chunk_output(n, scores, state)
            state = state * chunk_decay + kv
            if n + 1 < n_chunks:
                scores, kv = nxt

    for hd in range(heads_per_step):
        one_head(hd)


def _retention(lg, z3, zm):
    b, s, _ = z3.shape
    hps = RET_HEADS_PER_STEP
    wk, wv = hps * RET_DK, hps * RET_DV
    kern = functools.partial(_retention_kernel, n_chunks=s // CHUNK, heads_per_step=hps)
    return pl.pallas_call(
        kern,
        grid=(b, RET_HEADS // hps),
        in_specs=[
            pl.BlockSpec(memory_space=pltpu.SMEM),
            pl.BlockSpec((1, s, wk), lambda i, h: (i, 0, COL_RQ * LANES // wk + h)),
            pl.BlockSpec((1, s, wk), lambda i, h: (i, 0, COL_RK * LANES // wk + h)),
            pl.BlockSpec((1, s, wv), lambda i, h: (i, 0, COL_RV * LANES // wv + h)),
            pl.BlockSpec((1, s, wv), lambda i, h: (i, 0, COL_RG * LANES // wv + h)),
            pl.BlockSpec((CHUNK, wk), lambda i, h: (0, COL_RK * LANES // wk + h)),
            pl.BlockSpec((CHUNK, wv), lambda i, h: (0, COL_RV * LANES // wv + h)),
        ],
        out_specs=pl.BlockSpec((1, s, wv), lambda i, h: (i, 0, h)),
        out_shape=jax.ShapeDtypeStruct((b, s, RET_HEADS * RET_DV), _BF16),
        compiler_params=pltpu.CompilerParams(
            dimension_semantics=("arbitrary", "arbitrary"),
            vmem_limit_bytes=VMEM_LIMIT),
        name="retention",
    )(lg, z3, z3, z3, z3, zm, zm)


def _fox_kernel(q_ref, k_ref, v_ref, g_ref, c_ref, km_ref, vm_ref, cm_ref, o_ref,
                vt_scr, vtm_scr, ka_scr, kam_scr, qt_scr, acc_scr, m_scr, s_scr, pm_scr,
                sm_scr, pmm_scr, *, n_tiles, n_pairs, seq):
    half = FOX_HEAD_DIM
    n_heads = 2 * n_pairs
    sub = lax.broadcasted_iota(jnp.int32, (LANES, 1), 0)
    sub8 = lax.broadcasted_iota(jnp.int32, (SUBLANES, 1), 0)
    lane = lax.broadcasted_iota(jnp.int32, (1, LANES), 1)

    def pieces8(row):
        out = jnp.zeros((SUBLANES, LANES), _F32)
        rest = row
        for piece in range(FOX_BIAS_PIECES):
            part = rest.astype(_BF16).astype(_F32)
            out = jnp.where(sub8 == piece, part, out)
            rest = rest - part
        return out

    def augmented_keys(kblk, bias_lo, bias_hi):
        gap = jnp.zeros((half - SUBLANES, LANES), _F32)
        tile = jnp.concatenate([pieces8(bias_hi), gap, pieces8(bias_lo), gap], axis=0).T
        tile = tile.astype(_BF16)[0:kblk.shape[0], :]
        return jnp.where(lane < half, kblk, tile), jnp.where(lane < half, tile, kblk)

    def build_meta_operands():
        for p in range(n_pairs):
            lanes = slice(p * LANES, (p + 1) * LANES)
            cm0, cm1 = cm_ref[p, 0:1, :], cm_ref[p, 1:2, :]
            kam_scr[2 * p], kam_scr[2 * p + 1] = augmented_keys(
                km_ref[0:N_META, lanes],
                (cm0[:, LANES - 1:LANES] - cm0) * LOG2E, (cm1[:, LANES - 1:LANES] - cm1) * LOG2E)
            vtm = vm_ref[:, lanes].T
            for hh in range(2):
                h = 2 * p + hh
                vtm_scr[h, 0:half, :] = vtm[hh * half:(hh + 1) * half, :]
                vtm_scr[h, half:FOX_ACC_ROWS, :] = jnp.ones((FOX_ACC_ROWS - half, LANES), _BF16)

    def build_block_operands(jb):
        rows = slice(jb * FOX_TK, (jb + 1) * FOX_TK)
        for p in range(n_pairs):
            lanes = slice(p * LANES, (p + 1) * LANES)
            for cb in range(jb * FOX_TK // LANES, (jb + 1) * FOX_TK // LANES):
                ks = slice(cb * LANES, (cb + 1) * LANES)
                ka_scr[2 * p, ks, :], ka_scr[2 * p + 1, ks, :] = augmented_keys(
                    k_ref[0, ks, lanes],
                    c_ref[0, p, 0:1, ks] * -LOG2E, c_ref[0, p, 1:2, ks] * -LOG2E)
            vt = v_ref[0, rows, lanes].T
            for hh in range(2):
                h = 2 * p + hh
                vt_scr[h, jb, 0:half, :] = vt[hh * half:(hh + 1) * half, :]
                vt_scr[h, jb, half:FOX_ACC_ROWS, :] = jnp.ones(
                    (FOX_ACC_ROWS - half, FOX_TK), _BF16)

    ri = lax.broadcasted_iota(jnp.int32, (FOX_TK, FOX_TQ), 0)
    ci = lax.broadcasted_iota(jnp.int32, (FOX_TK, FOX_TQ), 1)
    causal = ri <= ci

    def group_max(st):
        parts = [st[r:r + SUBLANES, :] for r in range(0, st.shape[0], SUBLANES)]
        while len(parts) > 1:
            nxt = [jnp.maximum(parts[a], parts[a + 1]) for a in range(0, len(parts) - 1, 2)]
            if len(parts) % 2:
                nxt.append(parts[-1])
            parts = nxt
        return parts[0]

    def update(h, tile, first, st, pm, vt):
        buf = tile % 2
        m_blk = jnp.max(pm, axis=0, keepdims=True)
        if first:
            m_new = m_blk
        else:
            m_old = m_scr[buf, h]
            m_new = jnp.maximum(m_old, m_blk)
            alpha = jnp.exp2(m_old - m_new)
        pt = jnp.exp2(st - m_new).astype(_BF16)
        if pt.shape[0] < LANES:
            pt = jnp.concatenate(
                [pt, jnp.zeros((LANES - pt.shape[0], FOX_TQ), _BF16)], axis=0)
        pv = _dot(vt, pt)
        acc_scr[buf, h] = pv if first else acc_scr[buf, h] * alpha + pv
        m_scr[buf, h] = m_new

    def prepare_queries(i):
        rows = slice(i * FOX_TQ, (i + 1) * FOX_TQ)
        for p in range(n_pairs):
            qt = q_ref[0, rows, p * LANES:(p + 1) * LANES].astype(_F32).T
            ones_hi = jnp.where(sub < half + FOX_BIAS_PIECES, 1.0, 0.0)
            ones_lo = jnp.where(sub < FOX_BIAS_PIECES, 1.0, 0.0)
            qt_scr[i % 2, 2 * p] = jnp.where(sub < half, qt, ones_hi).astype(_BF16)
            qt_scr[i % 2, 2 * p + 1] = jnp.where(sub < half, ones_lo, qt).astype(_BF16)

    def scores(item, heads):
        kind, i, j, slot = item
        if kind == "meta":
            for h in heads:
                st = _dot(kam_scr[h], qt_scr[i % 2, h])
                sm_scr[h] = st
                pmm_scr[h] = group_max(st)
            return
        krows = slice(j * FOX_TK, (j + 1) * FOX_TK)
        for h in heads:
            st = _dot(ka_scr[h, krows, :], qt_scr[i % 2, h])
            if i == j:
                st = jnp.where(causal, st, NEG_INF)
            s_scr[slot, h] = st
            pm_scr[slot, h] = group_max(st)

    def consume(item, heads):
        kind, i, j, slot = item
        for h in heads:
            if kind == "meta":
                update(h, i, True, sm_scr[h], pmm_scr[h], vtm_scr[h])
            else:
                update(h, i, False, s_scr[slot, h], pm_scr[slot, h], vt_scr[h, j])

    def finalize(i):
        rows = slice(i * FOX_TQ, (i + 1) * FOX_TQ)
        for p in range(n_pairs):
            lanes = slice(p * LANES, (p + 1) * LANES)
            outs = []
            for hh in range(2):
                acc = acc_scr[i % 2, 2 * p + hh]
                outs.append(acc[0:half, :] / acc[half:half + 1, :])
            y = jnp.concatenate(outs, axis=0).T
            gate = g_ref[0, rows, lanes].astype(_F32)
            o_ref[0, rows, lanes] = (y * _silu(gate)).astype(_BF16)

    items = []
    for i in range(n_tiles):
        items.append(("meta", i, None, None))
        for j in [i] + list(range(i)):
            items.append(("blk", i, j, sum(it[0] == "blk" for it in items) % 2))
    build_meta_operands()
    build_block_operands(0)
    prepare_queries(0)
    scores(items[0], range(n_heads))
    for t, item in enumerate(items):
        if t + 1 < len(items):
            nxt = items[t + 1]
            if nxt[0] == "meta":
                build_block_operands(nxt[1])
                prepare_queries(nxt[1])
            scores(nxt, range(n_heads))
        consume(item, range(n_heads))
        if t + 1 == len(items) or items[t + 1][1] != item[1]:
            finalize(item[1])


def _fox(z3, c4, zm, cm3):
    b, s, _ = z3.shape
    n_pairs = FOX_PAIRS
    n_groups = FOX_HEADS // 2 // n_pairs
    n_tiles = s // FOX_TQ
    w = n_pairs * LANES
    kern = functools.partial(_fox_kernel, n_tiles=n_tiles, n_pairs=n_pairs, seq=s)
    return pl.pallas_call(
        kern,
        grid=(b, n_groups),
        in_specs=[
            pl.BlockSpec((1, s, w), lambda i, p: (i, 0, COL_FQ // n_pairs + p)),
            pl.BlockSpec((1, s, w), lambda i, p: (i, 0, COL_FK // n_pairs + p)),
            pl.BlockSpec((1, s, w), lambda i, p: (i, 0, COL_FV // n_pairs + p)),
            pl.BlockSpec((1, s, w), lambda i, p: (i, 0, COL_FG // n_pairs + p)),
            pl.BlockSpec((1, n_pairs, 2, s), lambda i, p: (i, p, 0, 0)),
            pl.BlockSpec((CHUNK, w), lambda i, p: (0, COL_FK // n_pairs + p)),
            pl.BlockSpec((CHUNK, w), lambda i, p: (0, COL_FV // n_pairs + p)),
            pl.BlockSpec((n_pairs, 2, LANES), lambda i, p: (p, 0, 0)),
        ],
        out_specs=pl.BlockSpec((1, s, w), lambda i, p: (i, 0, p)),
        out_shape=jax.ShapeDtypeStruct((b, s, FOX_HEADS * FOX_HEAD_DIM), _BF16),
        scratch_shapes=[
            pltpu.VMEM((2 * n_pairs, n_tiles, FOX_ACC_ROWS, FOX_TK), _BF16),
            pltpu.VMEM((2 * n_pairs, FOX_ACC_ROWS, LANES), _BF16),
            pltpu.VMEM((2 * n_pairs, s, LANES), _BF16),
            pltpu.VMEM((2 * n_pairs, N_META, LANES), _BF16),
            pltpu.VMEM((2, 2 * n_pairs, LANES, FOX_TQ), _BF16),
            pltpu.VMEM((2, 2 * n_pairs, FOX_ACC_ROWS, FOX_TQ), _F32),
            pltpu.VMEM((2, 2 * n_pairs, 1, FOX_TQ), _F32),
            pltpu.VMEM((2, 2 * n_pairs, FOX_TK, FOX_TQ), _F32),
            pltpu.VMEM((2, 2 * n_pairs, SUBLANES, FOX_TQ), _F32),
            pltpu.VMEM((2 * n_pairs, N_META, FOX_TQ), _F32),
            pltpu.VMEM((2 * n_pairs, SUBLANES, FOX_TQ), _F32),
        ],
        compiler_params=pltpu.CompilerParams(
            dimension_semantics=("arbitrary", "arbitrary"),
            vmem_limit_bytes=VMEM_LIMIT),
        name="fox",
    )(z3, z3, z3, z3, c4, zm, zm, cm3)


def _outproj_kernel(yr_ref, yf_ref, x_ref, w_ref, g_ref, o_ref, wb_scr):
    half = w_ref.shape[0] // 2

    @pl.when(pl.program_id(0) == 0)
    def _():
        wb_scr[...] = w_ref[...].astype(_BF16)

    for r in range(x_ref.shape[0] // OUT_SUB):
        rows = pl.ds(r * OUT_SUB, OUT_SUB)
        hres = (x_ref[rows, :] + _dot(yr_ref[rows, :], wb_scr[:half, :])
                + _dot(yf_ref[rows, :], wb_scr[half:, :]))
        ms = jnp.mean(hres * hres, axis=-1, keepdims=True)
        o_ref[rows, :] = hres * lax.rsqrt(ms + EPS) * g_ref[...]


def _outproj(yr, yf, x2d, w_out, g, *, tm):
    rows = x2d.shape[0]
    return pl.pallas_call(
        _outproj_kernel,
        grid=(rows // tm,),
        in_specs=[
            pl.BlockSpec((tm, yr.shape[1]), lambda i: (i, 0)),
            pl.BlockSpec((tm, yf.shape[1]), lambda i: (i, 0)),
            pl.BlockSpec((tm, D_MODEL), lambda i: (i, 0)),
            pl.BlockSpec(w_out.shape, lambda i: (0, 0)),
            pl.BlockSpec((1, D_MODEL), lambda i: (0, 0)),
        ],
        out_specs=pl.BlockSpec((tm, D_MODEL), lambda i: (i, 0)),
        out_shape=jax.ShapeDtypeStruct((rows, D_MODEL), _F32),
        scratch_shapes=[pltpu.VMEM(w_out.shape, _BF16)],
        compiler_params=pltpu.CompilerParams(
            dimension_semantics=("arbitrary",),
            vmem_limit_bytes=VMEM_LIMIT),
        name="outproj",
    )(yr, yf, x2d, w_out, g)


def _rope_tables(first_pos, n):
    pos = np.arange(n, dtype=np.float32) + np.float32(first_pos)
    inv = np.float32(ROPE_BASE) ** (-np.arange(0, RET_DK, 2, dtype=np.float32) / np.float32(RET_DK))
    ang = pos[:, None] * inv[None, :]
    cos, sin = np.cos(ang), np.sin(ang)
    return np.concatenate([cos, cos], axis=-1), np.concatenate([-sin, sin], axis=-1)


def kernel(x, meta_tokens, norm_g, w_in, b_f, w_out, final_g):
    b, s, d = x.shape
    assert norm_g.shape[0] == 1 and d == D_MODEL and s % FOX_TQ == 0
    x2d = x.reshape(b * s, d)
    col_scale = jnp.ones((D_MAIN, 1), _F32)
    col_scale = col_scale.at[COL_RK * LANES:COL_RV * LANES].set(RET_DK ** -0.5)
    col_scale = col_scale.at[COL_FQ * LANES:COL_FK * LANES].set(FOX_HEAD_DIM ** -0.5 * LOG2E)
    w_all_t = jnp.swapaxes(w_in, 1, 2)[0]
    g = norm_g[0].reshape(1, d)
    bf = b_f[0].reshape(FOX_HEADS, 1)
    cos2, sin2 = _rope_tables(N_META, s)
    cosm, sinm = _rope_tables(0, CHUNK)

    zm, cm, w_t = _inproj(meta_tokens.astype(_F32), g, w_all_t, col_scale, w_all_t, bf, cosm, sinm,
                          tm=CHUNK, tn=1792, n_valid=N_META)
    z, c, _ = _inproj(x2d, g, w_t, col_scale, w_all_t, bf, cos2, sin2, tm=s, tn=1792, n_valid=s)

    z3 = z.reshape(b, s, D_MAIN)
    lg = jnp.log1p(-jnp.exp2(-5.0 - jnp.arange(RET_HEADS, dtype=_F32)))
    y_r = _retention(lg, z3, zm)

    y_f = _fox(z3, c, zm, cm[0])

    out = _outproj(y_r.reshape(b * s, -1), y_f.reshape(b * s, -1), x2d,
                   w_out[0], final_g.reshape(1, d), tm=1024)
    return out.reshape(b, s, d)
```

```python
import functools

import jax
import jax.numpy as jnp
import numpy as np
from jax import lax
from jax.experimental import pallas as pl
from jax.experimental.pallas import tpu as pltpu

D_MODEL = 1024
N_META = 16
CHUNK = 128
RET_HEADS = 4
RET_DK = 128
RET_DV = 256
FOX_HEADS = 16
FOX_HEAD_DIM = 64
ROPE_BASE = 10000.0
EPS = 1e-6
NEG_INF = -1e30
LOG2E = 1.4426950408889634

LANES = 128
SUBLANES = 8
BF16_SUBLANES = 16
D_MAIN = 7168
COL_RQ, COL_RK, COL_RV, COL_RG = 0, 4, 8, 16
COL_FQ, COL_FK, COL_FV, COL_FG = 24, 32, 40, 48
FOX_TQ = 256
FOX_TK = 256
FOX_PAIRS = 2
FOX_ACC_ROWS = FOX_HEAD_DIM + BF16_SUBLANES
FOX_BIAS_PIECES = 3
PROJ_SUB = 512
OUT_SUB = 256
RET_HEADS_PER_STEP = 4
VMEM_LIMIT = 56 * 1024 * 1024

_F32 = jnp.float32
_BF16 = jnp.bfloat16


def _dot(a, b):
    return jnp.dot(a, b, preferred_element_type=_F32)


def _dot_nt(a, b):
    return lax.dot_general(a, b, (((1,), (1,)), ((), ())), preferred_element_type=_F32)


def _silu(g):
    h = 0.5 * g
    return h + h * jnp.tanh(h)


def _rotary(x, cos2, sin2):
    return x * cos2 + pltpu.roll(x, RET_DK // 2, axis=1) * sin2


def _inproj_kernel(x_ref, g_ref, w_ref, cs_ref, wff_ref, bf_ref, cos_ref, sin_ref,
                   z_ref, c_ref, wcast_ref, u_scr, lf_scr, *, tm, tn, sub, n_valid, cast_weights):
    j = pl.program_id(1)
    n_sub = tm // sub

    def weight_tile():
        if not cast_weights:
            wcast_ref[...] = jnp.zeros(wcast_ref.shape, _BF16)
            return w_ref[...]
        wcast_ref[...] = (w_ref[...] * cs_ref[...]).astype(_BF16)
        return wcast_ref[...]

    def normalize(r):
        lo = r * sub
        n = max(0, min(sub, x_ref.shape[0] - lo))
        if n:
            xf = x_ref[lo:lo + n, :]
            ms = jnp.mean(xf * xf, axis=-1, keepdims=True)
            u_scr[lo:lo + n, :] = (xf * lax.rsqrt(ms + EPS) * g_ref[...]).astype(_BF16)
        if n < sub:
            u_scr[lo + n:lo + sub, :] = jnp.zeros((sub - n, D_MODEL), _BF16)
        lf_scr[:, lo:lo + sub] = _dot_nt(wff_ref[...].astype(_BF16), u_scr[lo:lo + sub, :])

    def project(r, w_tile, rot_cols):
        rows = pl.ds(r * sub, sub)
        acc = _dot_nt(u_scr[rows, :], w_tile)
        if rot_cols:
            cos2, sin2 = cos_ref[rows, :], sin_ref[rows, :]
            slabs = [_rotary(acc[:, a:a + RET_DK], cos2, sin2) for a in range(0, rot_cols, RET_DK)]
            acc = jnp.concatenate(slabs + [acc[:, rot_cols:]], axis=1) if rot_cols < tn else (
                jnp.concatenate(slabs, axis=1))
        z_ref[rows, :] = acc.astype(_BF16)

    def forget_cumsum():
        lane = lax.broadcasted_iota(jnp.int32, (FOX_HEADS, LANES), 1)
        carry = jnp.zeros((FOX_HEADS, 1), _F32)
        hh = lax.broadcasted_iota(jnp.int32, (FOX_HEADS, FOX_HEADS), 0)
        ll = lax.broadcasted_iota(jnp.int32, (FOX_HEADS, FOX_HEADS), 1)
        bf = jnp.sum(jnp.where(hh == ll, bf_ref[...], 0.0), axis=1, keepdims=True)
        for ci in range(tm // LANES):
            v = lf_scr[:, ci * LANES:(ci + 1) * LANES] + bf
            blk = jnp.minimum(v, 0.0) - jnp.log1p(jnp.exp(-jnp.abs(v)))
            if (ci + 1) * LANES > n_valid:
                blk = jnp.where(lane + ci * LANES < n_valid, blk, 0.0)
            sh = 1
            while sh < LANES:
                rolled = pltpu.roll(blk, sh, axis=1)
                blk = blk + jnp.where(lane >= sh, rolled, 0.0)
                sh *= 2
            blk = blk + carry
            for p in range(FOX_HEADS // 2):
                c_ref[0, p, :, ci * LANES:(ci + 1) * LANES] = blk[2 * p:2 * p + 2, :]
            carry = blk[:, LANES - 1:LANES]

    rot_end = COL_RV * LANES
    rot_tiles = -(-rot_end // tn)

    for jj in range(rot_tiles):
        @pl.when(j == jj)
        def _(jj=jj):
            w_tile = weight_tile()
            for r in range(n_sub):
                if jj == 0:
                    normalize(r)
                project(r, w_tile, min(tn, rot_end - jj * tn))
            if jj == 0:
                forget_cumsum()

    @pl.when(j >= rot_tiles)
    def _():
        w_tile = weight_tile()
        for r in range(n_sub):
            project(r, w_tile, 0)


def _inproj(x2d, g, w_t, col_scale, w_all_t, b_f, cos2, sin2, *, tm, tn, n_valid):
    x_rows = min(tm, x2d.shape[0])
    assert x2d.shape[0] % x_rows == 0 and x_rows % BF16_SUBLANES == 0
    rows = (x2d.shape[0] // x_rows) * tm
    sub = min(tm, PROJ_SUB)
    assert D_MAIN % tn == 0 and tn % RET_DK == 0 and D_MAIN % FOX_HEADS == 0
    cast = w_t.dtype != _BF16
    wrows = tn if cast else SUBLANES
    kern = functools.partial(_inproj_kernel, tm=tm, tn=tn, sub=sub, n_valid=n_valid,
                             cast_weights=cast)
    return pl.pallas_call(
        kern,
        grid=(rows // tm, D_MAIN // tn),
        in_specs=[
            pl.BlockSpec((x_rows, D_MODEL), lambda i, j: (i, 0)),
            pl.BlockSpec((1, D_MODEL), lambda i, j: (0, 0)),
            pl.BlockSpec((tn, D_MODEL), lambda i, j: (j, 0)),
            pl.BlockSpec((wrows, 1), lambda i, j: (j if cast else 0, 0)),
            pl.BlockSpec((FOX_HEADS, D_MODEL), lambda i, j: (D_MAIN // FOX_HEADS, 0)),
            pl.BlockSpec((1, FOX_HEADS), lambda i, j: (0, 0)),
            pl.BlockSpec((tm, RET_DK), lambda i, j: (0, 0)),
            pl.BlockSpec((tm, RET_DK), lambda i, j: (0, 0)),
        ],
        out_specs=[
            pl.BlockSpec((tm, tn), lambda i, j: (i, j)),
            pl.BlockSpec((1, FOX_HEADS // 2, 2, tm), lambda i, j: (i, 0, 0, 0)),
            pl.BlockSpec((wrows, D_MODEL), lambda i, j: (j if cast else 0, 0)),
        ],
        out_shape=[
            jax.ShapeDtypeStruct((rows, D_MAIN), _BF16),
            jax.ShapeDtypeStruct((rows // tm, FOX_HEADS // 2, 2, tm), _F32),
            jax.ShapeDtypeStruct((D_MAIN if cast else SUBLANES, D_MODEL), _BF16),
        ],
        scratch_shapes=[
            pltpu.VMEM((tm, D_MODEL), _BF16),
            pltpu.VMEM((FOX_HEADS, tm), _F32),
        ],
        compiler_params=pltpu.CompilerParams(
            dimension_semantics=("arbitrary", "arbitrary"),
            vmem_limit_bytes=VMEM_LIMIT),
        name="inproj",
    )(x2d, g, w_t, col_scale, w_all_t, b_f, cos2, sin2)


def _retention_kernel(lg_ref, q_ref, k_ref, v_ref, g_ref, km_ref, vm_ref, o_ref,
                      *, n_chunks, heads_per_step):
    ri = lax.broadcasted_iota(jnp.int32, (CHUNK, CHUNK), 0)
    ci = lax.broadcasted_iota(jnp.int32, (CHUNK, CHUNK), 1)
    diff = (ri - ci).astype(_F32)
    idx = lax.broadcasted_iota(jnp.int32, (CHUNK, 1), 0).astype(_F32)

    def one_head(hd):
        lg = lg_ref[pl.program_id(1) * heads_per_step + hd]
        kcols = slice(hd * RET_DK, (hd + 1) * RET_DK)
        vcols = slice(hd * RET_DV, (hd + 1) * RET_DV)
        dmask = jnp.where(diff >= 0, jnp.exp(lg * jnp.maximum(diff, 0.0)), 0.0)
        xi = jnp.exp(lg * (idx + 1.0))
        zeta = jnp.exp(lg * (CHUNK - 1.0 - idx))
        chunk_decay = jnp.exp(lg * jnp.full((1, 1), float(CHUNK), _F32))

        zeta_m = jnp.exp(lg * (N_META - 1.0 - idx))
        km = km_ref[:, kcols].astype(_F32)
        state = _dot((km * zeta_m).T.astype(_BF16), vm_ref[:, vcols])

        def chunk_products(n):
            rows = slice(n * CHUNK, (n + 1) * CHUNK)
            kb = k_ref[0, rows, kcols]
            v = v_ref[0, rows, vcols]
            scores = (_dot_nt(q_ref[0, rows, kcols], kb) * dmask).astype(_BF16)
            kv = _dot((kb.astype(_F32) * zeta).T.astype(_BF16), v)
            return scores, kv

        def chunk_output(n, scores, state):
            rows = slice(n * CHUNK, (n + 1) * CHUNK)
            out = _dot(scores, v_ref[0, rows, vcols])
            out = out + _dot(q_ref[0, rows, kcols], state.astype(_BF16)) * xi
            y = out * lax.rsqrt(jnp.mean(out * out, axis=-1, keepdims=True) + EPS)
            gate = g_ref[0, rows, vcols].astype(_F32)
            o_ref[0, rows, vcols] = (y * _silu(gate)).astype(_BF16)

        scores, kv = chunk_products(0)
        for n in range(n_chunks):
            if n + 1 < n_chunks:
                nxt = chunk_products(n + 1)
            chunk_output(n, scores, state)
            state = state * chunk_decay + kv
            if n + 1 < n_chunks:
                scores, kv = nxt

    for hd in range(heads_per_step):
        one_head(hd)


def _retention(lg, z3, zm):
    b, s, _ = z3.shape
    hps = RET_HEADS_PER_STEP
    wk, wv = hps * RET_DK, hps * RET_DV
    kern = functools.partial(_retention_kernel, n_chunks=s // CHUNK, heads_per_step=hps)
    return pl.pallas_call(
        kern,
        grid=(b, RET_HEADS // hps),
        in_specs=[
            pl.BlockSpec(memory_space=pltpu.SMEM),
            pl.BlockSpec((1, s, wk), lambda i, h: (i, 0, COL_RQ * LANES // wk + h)),
            pl.BlockSpec((1, s, wk), lambda i, h: (i, 0, COL_RK * LANES // wk + h)),
            pl.BlockSpec((1, s, wv), lambda i, h: (i, 0, COL_RV * LANES // wv + h)),
            pl.BlockSpec((1, s, wv), lambda i, h: (i, 0, COL_RG * LANES // wv + h)),
            pl.BlockSpec((CHUNK, wk), lambda i, h: (0, COL_RK * LANES // wk + h)),
            pl.BlockSpec((CHUNK, wv), lambda i, h: (0, COL_RV * LANES // wv + h)),
        ],
        out_specs=pl.BlockSpec((1, s, wv), lambda i, h: (i, 0, h)),
        out_shape=jax.ShapeDtypeStruct((b, s, RET_HEADS * RET_DV), _BF16),
        compiler_params=pltpu.CompilerParams(
            dimension_semantics=("arbitrary", "arbitrary"),
            vmem_limit_bytes=VMEM_LIMIT),
        name="retention",
    )(lg, z3, z3, z3, z3, zm, zm)


def _fox_kernel(q_ref, k_ref, v_ref, g_ref, c_ref, km_ref, vm_ref, cm_ref, o_ref,
                vt_scr, vtm_scr, ka_scr, kam_scr, qt_scr, acc_scr, m_scr, s_scr, pm_scr,
                sm_scr, pmm_scr, *, n_tiles, n_pairs, seq):
    half = FOX_HEAD_DIM
    n_heads = 2 * n_pairs
    sub = lax.broadcasted_iota(jnp.int32, (LANES, 1), 0)
    sub8 = lax.broadcasted_iota(jnp.int32, (SUBLANES, 1), 0)
    lane = lax.broadcasted_iota(jnp.int32, (1, LANES), 1)

    def pieces8(row):
        out = jnp.zeros((SUBLANES, LANES), _F32)
        rest = row
        for piece in range(FOX_BIAS_PIECES):
            part = rest.astype(_BF16).astype(_F32)
            out = jnp.where(sub8 == piece, part, out)
            rest = rest - part
        return out

    def augmented_keys(kblk, bias_lo, bias_hi):
        gap = jnp.zeros((half - SUBLANES, LANES), _F32)
        tile = jnp.concatenate([pieces8(bias_hi), gap, pieces8(bias_lo), gap], axis=0).T
        tile = tile.astype(_BF16)[0:kblk.shape[0], :]
        return jnp.where(lane < half, kblk, tile), jnp.where(lane < half, tile, kblk)

    def build_meta_operands():
        for p in range(n_pairs):
            lanes = slice(p * LANES, (p + 1) * LANES)
            cm0, cm1 = cm_ref[p, 0:1, :], cm_ref[p, 1:2, :]
            kam_scr[2 * p], kam_scr[2 * p + 1] = augmented_keys(
                km_ref[0:N_META, lanes],
                (cm0[:, LANES - 1:LANES] - cm0) * LOG2E, (cm1[:, LANES - 1:LANES] - cm1) * LOG2E)
            vtm = vm_ref[:, lanes].T
            for hh in range(2):
                h = 2 * p + hh
                vtm_scr[h, 0:half, :] = vtm[hh * half:(hh + 1) * half, :]
                vtm_scr[h, half:FOX_ACC_ROWS, :] = jnp.ones((FOX_ACC_ROWS - half, LANES), _BF16)

    def build_block_operands(jb):
        rows = slice(jb * FOX_TK, (jb + 1) * FOX_TK)
        for p in range(n_pairs):
            lanes = slice(p * LANES, (p + 1) * LANES)
            for cb in range(jb * FOX_TK // LANES, (jb + 1) * FOX_TK // LANES):
                ks = slice(cb * LANES, (cb + 1) * LANES)
                ka_scr[2 * p, ks, :], ka_scr[2 * p + 1, ks, :] = augmented_keys(
                    k_ref[0, ks, lanes],
                    c_ref[0, p, 0:1, ks] * -LOG2E, c_ref[0, p, 1:2, ks] * -LOG2E)
            vt = v_ref[0, rows, lanes].T
            for hh in range(2):
                h = 2 * p + hh
                vt_scr[h, jb, 0:half, :] = vt[hh * half:(hh + 1) * half, :]
                vt_scr[h, jb, half:FOX_ACC_ROWS, :] = jnp.ones(
                    (FOX_ACC_ROWS - half, FOX_TK), _BF16)

    ri = lax.broadcasted_iota(jnp.int32, (FOX_TK, FOX_TQ), 0)
    ci = lax.broadcasted_iota(jnp.int32, (FOX_TK, FOX_TQ), 1)
    causal = ri <= ci

    def group_max(st):
        parts = [st[r:r + SUBLANES, :] for r in range(0, st.shape[0], SUBLANES)]
        while len(parts) > 1:
            nxt = [jnp.maximum(parts[a], parts[a + 1]) for a in range(0, len(parts) - 1, 2)]
            if len(parts) % 2:
                nxt.append(parts[-1])
            parts = nxt
        return parts[0]

    def update(h, tile, first, st, pm, vt):
        buf = tile % 2
        m_blk = jnp.max(pm, axis=0, keepdims=True)
        if first:
            m_new = m_blk
        else:
            m_old = m_scr[buf, h]
            m_new = jnp.maximum(m_old, m_blk)
            alpha = jnp.exp2(m_old - m_new)
        pt = jnp.exp2(st - m_new).astype(_BF16)
        if pt.shape[0] < LANES:
            pt = jnp.concatenate(
                [pt, jnp.zeros((LANES - pt.shape[0], FOX_TQ), _BF16)], axis=0)
        pv = _dot(vt, pt)
        acc_scr[buf, h] = pv if first else acc_scr[buf, h] * alpha + pv
        m_scr[buf, h] = m_new

    def prepare_queries(i):
        rows = slice(i * FOX_TQ, (i + 1) * FOX_TQ)
        for p in range(n_pairs):
            qt = q_ref[0, rows, p * LANES:(p + 1) * LANES].astype(_F32).T
            ones_hi = jnp.where(sub < half + FOX_BIAS_PIECES, 1.0, 0.0)
            ones_lo = jnp.where(sub < FOX_BIAS_PIECES, 1.0, 0.0)
            qt_scr[i % 2, 2 * p] = jnp.where(sub < half, qt, ones_hi).astype(_BF16)
            qt_scr[i % 2, 2 * p + 1] = jnp.where(sub < half, ones_lo, qt).astype(_BF16)

    def scores(item, heads):
        kind, i, j, slot = item
        if kind == "meta":
            for h in heads:
                st = _dot(kam_scr[h], qt_scr[i % 2, h])
                sm_scr[h] = st
                pmm_scr[h] = group_max(st)
            return
        krows = slice(j * FOX_TK, (j + 1) * FOX_TK)
        for h in heads:
            st = _dot(ka_scr[h, krows, :], qt_scr[i % 2, h])
            if i == j:
                st = jnp.where(causal, st, NEG_INF)
            s_scr[slot, h] = st
            pm_scr[slot, h] = group_max(st)

    def consume(item, heads):
        kind, i, j, slot = item
        for h in heads:
            if kind == "meta":
                update(h, i, True, sm_scr[h], pmm_scr[h], vtm_scr[h])
            else:
                update(h, i, False, s_scr[slot, h], pm_scr[slot, h], vt_scr[h, j])

    def finalize(i):
        rows = slice(i * FOX_TQ, (i + 1) * FOX_TQ)
        for p in range(n_pairs):
            lanes = slice(p * LANES, (p + 1) * LANES)
            outs = []
            for hh in range(2):
                acc = acc_scr[i % 2, 2 * p + hh]
                outs.append(acc[0:half, :] / acc[half:half + 1, :])
            y = jnp.concatenate(outs, axis=0).T
            gate = g_ref[0, rows, lanes].astype(_F32)
            o_ref[0, rows, lanes] = (y * _silu(gate)).astype(_BF16)

    items = []
    for i in range(n_tiles):
        items.append(("meta", i, None, None))
        for j in [i] + list(range(i)):
            items.append(("blk", i, j, sum(it[0] == "blk" for it in items) % 2))
    build_meta_operands()
    build_block_operands(0)
    prepare_queries(0)
    scores(items[0], range(n_heads))
    for t, item in enumerate(items):
        if t + 1 < len(items):
            nxt = items[t + 1]
            if nxt[0] == "meta":
                build_block_operands(nxt[1])
                prepare_queries(nxt[1])
            scores(nxt, range(n_heads))
        consume(item, range(n_heads))
        if t + 1 == len(items) or items[t + 1][1] != item[1]:
            finalize(item[1])


def _fox(z3, c4, zm, cm3):
    b, s, _ = z3.shape
    n_pairs = FOX_PAIRS
    n_groups = FOX_HEADS // 2 // n_pairs
    n_tiles = s // FOX_TQ
    w = n_pairs * LANES
    kern = functools.partial(_fox_kernel, n_tiles=n_tiles, n_pairs=n_pairs, seq=s)
    return pl.pallas_call(
        kern,
        grid=(b, n_groups),
        in_specs=[
            pl.BlockSpec((1, s, w), lambda i, p: (i, 0, COL_FQ // n_pairs + p)),
            pl.BlockSpec((1, s, w), lambda i, p: (i, 0, COL_FK // n_pairs + p)),
            pl.BlockSpec((1, s, w), lambda i, p: (i, 0, COL_FV // n_pairs + p)),
            pl.BlockSpec((1, s, w), lambda i, p: (i, 0, COL_FG // n_pairs + p)),
            pl.BlockSpec((1, n_pairs, 2, s), lambda i, p: (i, p, 0, 0)),
            pl.BlockSpec((CHUNK, w), lambda i, p: (0, COL_FK // n_pairs + p)),
            pl.BlockSpec((CHUNK, w), lambda i, p: (0, COL_FV // n_pairs + p)),
            pl.BlockSpec((n_pairs, 2, LANES), lambda i, p: (p, 0, 0)),
        ],
        out_specs=pl.BlockSpec((1, s, w), lambda i, p: (i, 0, p)),
        out_shape=jax.ShapeDtypeStruct((b, s, FOX_HEADS * FOX_HEAD_DIM), _BF16),
        scratch_shapes=[
            pltpu.VMEM((2 * n_pairs, n_tiles, FOX_ACC_ROWS, FOX_TK), _BF16),
            pltpu.VMEM((2 * n_pairs, FOX_ACC_ROWS, LANES), _BF16),
            pltpu.VMEM((2 * n_pairs, s, LANES), _BF16),
            pltpu.VMEM((2 * n_pairs, N_META, LANES), _BF16),
            pltpu.VMEM((2, 2 * n_pairs, LANES, FOX_TQ), _BF16),
            pltpu.VMEM((2, 2 * n_pairs, FOX_ACC_ROWS, FOX_TQ), _F32),
            pltpu.VMEM((2, 2 * n_pairs, 1, FOX_TQ), _F32),
            pltpu.VMEM((2, 2 * n_pairs, FOX_TK, FOX_TQ), _F32),
            pltpu.VMEM((2, 2 * n_pairs, SUBLANES, FOX_TQ), _F32),
            pltpu.VMEM((2 * n_pairs, N_META, FOX_TQ), _F32),
            pltpu.VMEM((2 * n_pairs, SUBLANES, FOX_TQ), _F32),
        ],
        compiler_params=pltpu.CompilerParams(
            dimension_semantics=("arbitrary", "arbitrary"),
            vmem_limit_bytes=VMEM_LIMIT),
        name="fox",
    )(z3, z3, z3, z3, c4, zm, zm, cm3)


def _outproj_kernel(yr_ref, yf_ref, x_ref, w_ref, g_ref, o_ref, wb_scr):
    half = w_ref.shape[0] // 2

    @pl.when(pl.program_id(0) == 0)
    def _():
        wb_scr[...] = w_ref[...].astype(_BF16)

    for r in range(x_ref.shape[0] // OUT_SUB):
        rows = pl.ds(r * OUT_SUB, OUT_SUB)
        hres = (x_ref[rows, :] + _dot(yr_ref[rows, :], wb_scr[:half, :])
                + _dot(yf_ref[rows, :], wb_scr[half:, :]))
        ms = jnp.mean(hres * hres, axis=-1, keepdims=True)
        o_ref[rows, :] = hres * lax.rsqrt(ms + EPS) * g_ref[...]


def _outproj(yr, yf, x2d, w_out, g, *, tm):
    rows = x2d.shape[0]
    return pl.pallas_call(
        _outproj_kernel,
        grid=(rows // tm,),
        in_specs=[
            pl.BlockSpec((tm, yr.shape[1]), lambda i: (i, 0)),
            pl.BlockSpec((tm, yf.shape[1]), lambda i: (i, 0)),
            pl.BlockSpec((tm, D_MODEL), lambda i: (i, 0)),
            pl.BlockSpec(w_out.shape, lambda i: (0, 0)),
            pl.BlockSpec((1, D_MODEL), lambda i: (0, 0)),
        ],
        out_specs=pl.BlockSpec((tm, D_MODEL), lambda i: (i, 0)),
        out_shape=jax.ShapeDtypeStruct((rows, D_MODEL), _F32),
        scratch_shapes=[pltpu.VMEM(w_out.shape, _BF16)],
        compiler_params=pltpu.CompilerParams(
            dimension_semantics=("arbitrary",),
            vmem_limit_bytes=VMEM_LIMIT),
        name="outproj",
    )(yr, yf, x2d, w_out, g)


def _rope_tables(first_pos, n):
    pos = np.arange(n, dtype=np.float32) + np.float32(first_pos)
    inv = np.float32(ROPE_BASE) ** (-np.arange(0, RET_DK, 2, dtype=np.float32) / np.float32(RET_DK))
    ang = pos[:, None] * inv[None, :]
    cos, sin = np.cos(ang), np.sin(ang)
    return np.concatenate([cos, cos], axis=-1), np.concatenate([-sin, sin], axis=-1)


def kernel(x, meta_tokens, norm_g, w_in, b_f, w_out, final_g):
    b, s, d = x.shape
    assert norm_g.shape[0] == 1 and d == D_MODEL and s % FOX_TQ == 0
    x2d = x.reshape(b * s, d)
    col_scale = jnp.ones((D_MAIN, 1), _F32)
    col_scale = col_scale.at[COL_RK * LANES:COL_RV * LANES].set(RET_DK ** -0.5)
    col_scale = col_scale.at[COL_FQ * LANES:COL_FK * LANES].set(FOX_HEAD_DIM ** -0.5 * LOG2E)
    w_all_t = jnp.swapaxes(w_in, 1, 2)[0]
    g = norm_g[0].reshape(1, d)
    bf = b_f[0].reshape(1, FOX_HEADS)
    cos2, sin2 = _rope_tables(N_META, s)
    cosm, sinm = _rope_tables(0, CHUNK)

    zm, cm, w_t = _inproj(meta_tokens.astype(_F32), g, w_all_t, col_scale, w_all_t, bf, cosm, sinm,
                          tm=CHUNK, tn=1792, n_valid=N_META)
    z, c, _ = _inproj(x2d, g, w_t, col_scale, w_all_t, bf, cos2, sin2, tm=s, tn=1792, n_valid=s)

    z3 = z.reshape(b, s, D_MAIN)
    lg = np.log1p(-np.exp2(-5.0 - np.arange(RET_HEADS, dtype=np.float32))).astype(np.float32)
    y_r = _retention(lg, z3, zm)

    y_f = _fox(z3, c, zm, cm[0])

    out = _outproj(y_r.reshape(b * s, -1), y_f.reshape(b * s, -1), x2d,
                   w_out[0], final_g.reshape(1, d), tm=1024)
    return out.reshape(b, s, d)
```

```python
import functools

import jax
import jax.numpy as jnp
import numpy as np
from jax import lax
from jax.experimental import pallas as pl
from jax.experimental.pallas import tpu as pltpu

D_MODEL = 1024
N_META = 16
CHUNK = 128
RET_HEADS = 4
RET_DK = 128
RET_DV = 256
FOX_HEADS = 16
FOX_HEAD_DIM = 64
ROPE_BASE = 10000.0
EPS = 1e-6
NEG_INF = -1e30
LOG2E = 1.4426950408889634

LANES = 128
SUBLANES = 8
BF16_SUBLANES = 16
D_MAIN = 7168
COL_RQ, COL_RK, COL_RV, COL_RG = 0, 4, 8, 16
COL_FQ, COL_FK, COL_FV, COL_FG = 24, 32, 40, 48
FOX_TQ = 256
FOX_TK = 256
FOX_PAIRS = 2
FOX_ACC_ROWS = FOX_HEAD_DIM + BF16_SUBLANES
FOX_BIAS_PIECES = 3
PROJ_SUB = 512
OUT_SUB = 256
RET_HEADS_PER_STEP = 4
VMEM_LIMIT = 56 * 1024 * 1024

_F32 = jnp.float32
_BF16 = jnp.bfloat16


def _dot(a, b):
    return jnp.dot(a, b, preferred_element_type=_F32)


def _dot_nt(a, b):
    return lax.dot_general(a, b, (((1,), (1,)), ((), ())), preferred_element_type=_F32)


def _silu(g):
    h = 0.5 * g
    return h + h * jnp.tanh(h)


def _rotary(x, cos2, sin2):
    return x * cos2 + pltpu.roll(x, RET_DK // 2, axis=1) * sin2


def _inproj_kernel(x_ref, g_ref, w_ref, cs_ref, wff_ref, bf_ref, cos_ref, sin_ref,
                   z_ref, c_ref, wcast_ref, u_scr, lf_scr, *, tm, tn, sub, n_valid, cast_weights):
    j = pl.program_id(1)
    n_sub = tm // sub

    def weight_tile():
        if not cast_weights:
            wcast_ref[...] = jnp.zeros(wcast_ref.shape, _BF16)
            return w_ref[...]
        wcast_ref[...] = (w_ref[...] * cs_ref[...]).astype(_BF16)
        return wcast_ref[...]

    def normalize(r):
        lo = r * sub
        n = max(0, min(sub, x_ref.shape[0] - lo))
        if n:
            xf = x_ref[lo:lo + n, :]
            ms = jnp.mean(xf * xf, axis=-1, keepdims=True)
            u_scr[lo:lo + n, :] = (xf * lax.rsqrt(ms + EPS) * g_ref[...]).astype(_BF16)
        if n < sub:
            u_scr[lo + n:lo + sub, :] = jnp.zeros((sub - n, D_MODEL), _BF16)
        lf_scr[:, lo:lo + sub] = _dot_nt(wff_ref[...].astype(_BF16), u_scr[lo:lo + sub, :])

    def project(r, w_tile, rot_cols):
        rows = pl.ds(r * sub, sub)
        acc = _dot_nt(u_scr[rows, :], w_tile)
        if rot_cols:
            cos2, sin2 = cos_ref[rows, :], sin_ref[rows, :]
            slabs = [_rotary(acc[:, a:a + RET_DK], cos2, sin2) for a in range(0, rot_cols, RET_DK)]
            acc = jnp.concatenate(slabs + [acc[:, rot_cols:]], axis=1) if rot_cols < tn else (
                jnp.concatenate(slabs, axis=1))
        z_ref[rows, :] = acc.astype(_BF16)

    def forget_cumsum():
        lane = lax.broadcasted_iota(jnp.int32, (FOX_HEADS, LANES), 1)
        carry = jnp.zeros((FOX_HEADS, 1), _F32)
        hh = lax.broadcasted_iota(jnp.int32, (FOX_HEADS, FOX_HEADS), 0)
        ll = lax.broadcasted_iota(jnp.int32, (FOX_HEADS, FOX_HEADS), 1)
        bf = jnp.sum(jnp.where(hh == ll, bf_ref[...], 0.0), axis=1, keepdims=True)
        for ci in range(tm // LANES):
            v = lf_scr[:, ci * LANES:(ci + 1) * LANES] + bf
            blk = jnp.minimum(v, 0.0) - jnp.log1p(jnp.exp(-jnp.abs(v)))
            if (ci + 1) * LANES > n_valid:
                blk = jnp.where(lane + ci * LANES < n_valid, blk, 0.0)
            sh = 1
            while sh < LANES:
                rolled = pltpu.roll(blk, sh, axis=1)
                blk = blk + jnp.where(lane >= sh, rolled, 0.0)
                sh *= 2
            blk = blk + carry
            for p in range(FOX_HEADS // 2):
                c_ref[0, p, :, ci * LANES:(ci + 1) * LANES] = blk[2 * p:2 * p + 2, :]
            carry = blk[:, LANES - 1:LANES]

    rot_end = COL_RV * LANES
    rot_tiles = -(-rot_end // tn)

    for jj in range(rot_tiles):
        @pl.when(j == jj)
        def _(jj=jj):
            w_tile = weight_tile()
            for r in range(n_sub):
                if jj == 0:
                    normalize(r)
                project(r, w_tile, min(tn, rot_end - jj * tn))
            if jj == 0:
                forget_cumsum()

    @pl.when(j >= rot_tiles)
    def _():
        w_tile = weight_tile()
        for r in range(n_sub):
            project(r, w_tile, 0)


def _inproj(x2d, g, w_t, col_scale, w_all_t, b_f, cos2, sin2, *, tm, tn, n_valid):
    x_rows = min(tm, x2d.shape[0])
    assert x2d.shape[0] % x_rows == 0 and x_rows % BF16_SUBLANES == 0
    rows = (x2d.shape[0] // x_rows) * tm
    sub = min(tm, PROJ_SUB)
    assert D_MAIN % tn == 0 and tn % RET_DK == 0 and D_MAIN % FOX_HEADS == 0
    cast = w_t.dtype != _BF16
    wrows = tn if cast else SUBLANES
    kern = functools.partial(_inproj_kernel, tm=tm, tn=tn, sub=sub, n_valid=n_valid,
                             cast_weights=cast)
    return pl.pallas_call(
        kern,
        grid=(rows // tm, D_MAIN // tn),
        in_specs=[
            pl.BlockSpec((x_rows, D_MODEL), lambda i, j: (i, 0)),
            pl.BlockSpec((1, D_MODEL), lambda i, j: (0, 0)),
            pl.BlockSpec((tn, D_MODEL), lambda i, j: (j, 0)),
            pl.BlockSpec((wrows, 1), lambda i, j: (j if cast else 0, 0)),
            pl.BlockSpec((FOX_HEADS, D_MODEL), lambda i, j: (D_MAIN // FOX_HEADS, 0)),
            pl.BlockSpec((1, FOX_HEADS), lambda i, j: (0, 0)),
            pl.BlockSpec((tm, RET_DK), lambda i, j: (0, 0)),
            pl.BlockSpec((tm, RET_DK), lambda i, j: (0, 0)),
        ],
        out_specs=[
            pl.BlockSpec((tm, tn), lambda i, j: (i, j)),
            pl.BlockSpec((1, FOX_HEADS // 2, 2, tm), lambda i, j: (i, 0, 0, 0)),
            pl.BlockSpec((wrows, D_MODEL), lambda i, j: (j if cast else 0, 0)),
        ],
        out_shape=[
            jax.ShapeDtypeStruct((rows, D_MAIN), _BF16),
            jax.ShapeDtypeStruct((rows // tm, FOX_HEADS // 2, 2, tm), _F32),
            jax.ShapeDtypeStruct((D_MAIN if cast else SUBLANES, D_MODEL), _BF16),
        ],
        scratch_shapes=[
            pltpu.VMEM((tm, D_MODEL), _BF16),
            pltpu.VMEM((FOX_HEADS, tm), _F32),
        ],
        compiler_params=pltpu.CompilerParams(
            dimension_semantics=("arbitrary", "arbitrary"),
            vmem_limit_bytes=VMEM_LIMIT),
        name="inproj",
    )(x2d, g, w_t, col_scale, w_all_t, b_f, cos2, sin2)


def _retention_kernel(lg_ref, q_ref, k_ref, v_ref, g_ref, km_ref, vm_ref, o_ref,
                      *, n_chunks, heads_per_step):
    ri = lax.broadcasted_iota(jnp.int32, (CHUNK, CHUNK), 0)
    ci = lax.broadcasted_iota(jnp.int32, (CHUNK, CHUNK), 1)
    diff = (ri - ci).astype(_F32)
    idx = lax.broadcasted_iota(jnp.int32, (CHUNK, 1), 0).astype(_F32)

    def one_head(hd):
        lg = lg_ref[pl.program_id(1) * heads_per_step + hd]
        kcols = slice(hd * RET_DK, (hd + 1) * RET_DK)
        vcols = slice(hd * RET_DV, (hd + 1) * RET_DV)
        dmask = jnp.where(diff >= 0, jnp.exp(lg * jnp.maximum(diff, 0.0)), 0.0)
        xi = jnp.exp(lg * (idx + 1.0))
        zeta = jnp.exp(lg * (CHUNK - 1.0 - idx))
        chunk_decay = jnp.exp(lg * jnp.full((1, 1), float(CHUNK), _F32))

        zeta_m = jnp.exp(lg * (N_META - 1.0 - idx))
        km = km_ref[:, kcols].astype(_F32)
        state = _dot((km * zeta_m).T.astype(_BF16), vm_ref[:, vcols])

        def chunk_products(n):
            rows = slice(n * CHUNK, (n + 1) * CHUNK)
            kb = k_ref[0, rows, kcols]
            v = v_ref[0, rows, vcols]
            scores = (_dot_nt(q_ref[0, rows, kcols], kb) * dmask).astype(_BF16)
            kv = _dot((kb.astype(_F32) * zeta).T.astype(_BF16), v)
            return scores, kv

        def chunk_output(n, scores, state):
            rows = slice(n * CHUNK, (n + 1) * CHUNK)
            out = _dot(scores, v_ref[0, rows, vcols])
            out = out + _dot(q_ref[0, rows, kcols], state.astype(_BF16)) * xi
            y = out * lax.rsqrt(jnp.mean(out * out, axis=-1, keepdims=True) + EPS)
            gate = g_ref[0, rows, vcols].astype(_F32)
            o_ref[0, rows, vcols] = (y * _silu(gate)).astype(_BF16)

        scores, kv = chunk_products(0)
        for n in range(n_chunks):
            if n + 1 < n_chunks:
                nxt = chunk_products(n + 1)
            chunk_output(n, scores, state)
            state = state * chunk_decay + kv
            if n + 1 < n_chunks:
                scores, kv = nxt

    for hd in range(heads_per_step):
        one_head(hd)


def _retention(lg, z3, zm):
    b, s, _ = z3.shape
    hps = RET_HEADS_PER_STEP
    wk, wv = hps * RET_DK, hps * RET_DV
    kern = functools.partial(_retention_kernel, n_chunks=s // CHUNK, heads_per_step=hps)
    return pl.pallas_call(
        kern,
        grid=(b, RET_HEADS // hps),
        in_specs=[
            pl.BlockSpec(memory_space=pltpu.SMEM),
            pl.BlockSpec((1, s, wk), lambda i, h: (i, 0, COL_RQ * LANES // wk + h)),
            pl.BlockSpec((1, s, wk), lambda i, h: (i, 0, COL_RK * LANES // wk + h)),
            pl.BlockSpec((1, s, wv), lambda i, h: (i, 0, COL_RV * LANES // wv + h)),
            pl.BlockSpec((1, s, wv), lambda i, h: (i, 0, COL_RG * LANES // wv + h)),
            pl.BlockSpec((CHUNK, wk), lambda i, h: (0, COL_RK * LANES // wk + h)),
            pl.BlockSpec((CHUNK, wv), lambda i, h: (0, COL_RV * LANES // wv + h)),
        ],
        out_specs=pl.BlockSpec((1, s, wv), lambda i, h: (i, 0, h)),
        out_shape=jax.ShapeDtypeStruct((b, s, RET_HEADS * RET_DV), _BF16),
        compiler_params=pltpu.CompilerParams(
            dimension_semantics=("arbitrary", "arbitrary"),
            vmem_limit_bytes=VMEM_LIMIT),
        name="retention",
    )(lg, z3, z3, z3, z3, zm, zm)


def _fox_kernel(q_ref, k_ref, v_ref, g_ref, c_ref, km_ref, vm_ref, cm_ref, o_ref,
                vt_scr, vtm_scr, ka_scr, kam_scr, qt_scr, acc_scr, m_scr, s_scr, pm_scr,
                sm_scr, pmm_scr, *, n_tiles, n_pairs, seq):
    half = FOX_HEAD_DIM
    n_heads = 2 * n_pairs
    sub = lax.broadcasted_iota(jnp.int32, (LANES, 1), 0)
    sub8 = lax.broadcasted_iota(jnp.int32, (SUBLANES, 1), 0)
    lane = lax.broadcasted_iota(jnp.int32, (1, LANES), 1)

    def pieces8(row):
        out = jnp.zeros((SUBLANES, LANES), _F32)
        rest = row
        for piece in range(FOX_BIAS_PIECES):
            part = rest.astype(_BF16).astype(_F32)
            out = jnp.where(sub8 == piece, part, out)
            rest = rest - part
        return out

    def augmented_keys(kblk, bias_lo, bias_hi):
        gap = jnp.zeros((half - SUBLANES, LANES), _F32)
        tile = jnp.concatenate([pieces8(bias_hi), gap, pieces8(bias_lo), gap], axis=0).T
        tile = tile.astype(_BF16)[0:kblk.shape[0], :]
        return jnp.where(lane < half, kblk, tile), jnp.where(lane < half, tile, kblk)

    def build_meta_operands():
        for p in range(n_pairs):
            lanes = slice(p * LANES, (p + 1) * LANES)
            cm0, cm1 = cm_ref[p, 0:1, :], cm_ref[p, 1:2, :]
            kam_scr[2 * p], kam_scr[2 * p + 1] = augmented_keys(
                km_ref[0:N_META, lanes],
                (cm0[:, LANES - 1:LANES] - cm0) * LOG2E, (cm1[:, LANES - 1:LANES] - cm1) * LOG2E)
            vtm = vm_ref[:, lanes].T
            for hh in range(2):
                h = 2 * p + hh
                vtm_scr[h, 0:half, :] = vtm[hh * half:(hh + 1) * half, :]
                vtm_scr[h, half:FOX_ACC_ROWS, :] = jnp.ones((FOX_ACC_ROWS - half, LANES), _BF16)

    def build_block_operands(jb):
        rows = slice(jb * FOX_TK, (jb + 1) * FOX_TK)
        for p in range(n_pairs):
            lanes = slice(p * LANES, (p + 1) * LANES)
            for cb in range(jb * FOX_TK // LANES, (jb + 1) * FOX_TK // LANES):
                ks = slice(cb * LANES, (cb + 1) * LANES)
                ka_scr[2 * p, ks, :], ka_scr[2 * p + 1, ks, :] = augmented_keys(
                    k_ref[0, ks, lanes],
                    c_ref[0, p, 0:1, ks] * -LOG2E, c_ref[0, p, 1:2, ks] * -LOG2E)
            vt = v_ref[0, rows, lanes].T
            for hh in range(2):
                h = 2 * p + hh
                vt_scr[h, jb, 0:half, :] = vt[hh * half:(hh + 1) * half, :]
                vt_scr[h, jb, half:FOX_ACC_ROWS, :] = jnp.ones(
                    (FOX_ACC_ROWS - half, FOX_TK), _BF16)

    ri = lax.broadcasted_iota(jnp.int32, (FOX_TK, FOX_TQ), 0)
    ci = lax.broadcasted_iota(jnp.int32, (FOX_TK, FOX_TQ), 1)
    causal = ri <= ci

    def group_max(st):
        parts = [st[r:r + SUBLANES, :] for r in range(0, st.shape[0], SUBLANES)]
        while len(parts) > 1:
            nxt = [jnp.maximum(parts[a], parts[a + 1]) for a in range(0, len(parts) - 1, 2)]
            if len(parts) % 2:
                nxt.append(parts[-1])
            parts = nxt
        return parts[0]

    def update(h, tile, first, st, pm, vt):
        buf = tile % 2
        m_blk = jnp.max(pm, axis=0, keepdims=True)
        if first:
            m_new = m_blk
        else:
            m_old = m_scr[buf, h]
            m_new = jnp.maximum(m_old, m_blk)
            alpha = jnp.exp2(m_old - m_new)
        pt = jnp.exp2(st - m_new).astype(_BF16)
        if pt.shape[0] < LANES:
            pt = jnp.concatenate(
                [pt, jnp.zeros((LANES - pt.shape[0], FOX_TQ), _BF16)], axis=0)
        pv = _dot(vt, pt)
        acc_scr[buf, h] = pv if first else acc_scr[buf, h] * alpha + pv
        m_scr[buf, h] = m_new

    def prepare_queries(i):
        rows = slice(i * FOX_TQ, (i + 1) * FOX_TQ)
        for p in range(n_pairs):
            qt = q_ref[0, rows, p * LANES:(p + 1) * LANES].astype(_F32).T
            ones_hi = jnp.where(sub < half + FOX_BIAS_PIECES, 1.0, 0.0)
            ones_lo = jnp.where(sub < FOX_BIAS_PIECES, 1.0, 0.0)
            qt_scr[i % 2, 2 * p] = jnp.where(sub < half, qt, ones_hi).astype(_BF16)
            qt_scr[i % 2, 2 * p + 1] = jnp.where(sub < half, ones_lo, qt).astype(_BF16)

    def scores(item, heads):
        kind, i, j, slot = item
        if kind == "meta":
            for h in heads:
                st = _dot(kam_scr[h], qt_scr[i % 2, h])
                sm_scr[h] = st
                pmm_scr[h] = group_max(st)
            return
        krows = slice(j * FOX_TK, (j + 1) * FOX_TK)
        for h in heads:
            st = _dot(ka_scr[h, krows, :], qt_scr[i % 2, h])
            if i == j:
                st = jnp.where(causal, st, NEG_INF)
            s_scr[slot, h] = st
            pm_scr[slot, h] = group_max(st)

    def consume(item, heads):
        kind, i, j, slot = item
        for h in heads:
            if kind == "meta":
                update(h, i, True, sm_scr[h], pmm_scr[h], vtm_scr[h])
            else:
                update(h, i, False, s_scr[slot, h], pm_scr[slot, h], vt_scr[h, j])

    def finalize(i):
        rows = slice(i * FOX_TQ, (i + 1) * FOX_TQ)
        for p in range(n_pairs):
            lanes = slice(p * LANES, (p + 1) * LANES)
            outs = []
            for hh in range(2):
                acc = acc_scr[i % 2, 2 * p + hh]
                outs.append(acc[0:half, :] / acc[half:half + 1, :])
            y = jnp.concatenate(outs, axis=0).T
            gate = g_ref[0, rows, lanes].astype(_F32)
            o_ref[0, rows, lanes] = (y * _silu(gate)).astype(_BF16)

    items = []
    for i in range(n_tiles):
        items.append(("meta", i, None, None))
        for j in [i] + list(range(i)):
            items.append(("blk", i, j, sum(it[0] == "blk" for it in items) % 2))
    build_meta_operands()
    build_block_operands(0)
    prepare_queries(0)
    scores(items[0], range(n_heads))
    for t, item in enumerate(items):
        if t + 1 < len(items):
            nxt = items[t + 1]
            if nxt[0] == "meta":
                build_block_operands(nxt[1])
                prepare_queries(nxt[1])
            scores(nxt, range(n_heads))
        consume(item, range(n_heads))
        if t + 1 == len(items) or items[t + 1][1] != item[1]:
            finalize(item[1])


def _fox(z3, c4, zm, cm3):
    b, s, _ = z3.shape
    n_pairs = FOX_PAIRS
    n_groups = FOX_HEADS // 2 // n_pairs
    n_tiles = s // FOX_TQ
    w = n_pairs * LANES
    kern = functools.partial(_fox_kernel, n_tiles=n_tiles, n_pairs=n_pairs, seq=s)
    return pl.pallas_call(
        kern,
        grid=(b, n_groups),
        in_specs=[
            pl.BlockSpec((1, s, w), lambda i, p: (i, 0, COL_FQ // n_pairs + p)),
            pl.BlockSpec((1, s, w), lambda i, p: (i, 0, COL_FK // n_pairs + p)),
            pl.BlockSpec((1, s, w), lambda i, p: (i, 0, COL_FV // n_pairs + p)),
            pl.BlockSpec((1, s, w), lambda i, p: (i, 0, COL_FG // n_pairs + p)),
            pl.BlockSpec((1, n_pairs, 2, s), lambda i, p: (i, p, 0, 0)),
            pl.BlockSpec((CHUNK, w), lambda i, p: (0, COL_FK // n_pairs + p)),
            pl.BlockSpec((CHUNK, w), lambda i, p: (0, COL_FV // n_pairs + p)),
            pl.BlockSpec((n_pairs, 2, LANES), lambda i, p: (p, 0, 0)),
        ],
        out_specs=pl.BlockSpec((1, s, w), lambda i, p: (i, 0, p)),
        out_shape=jax.ShapeDtypeStruct((b, s, FOX_HEADS * FOX_HEAD_DIM), _BF16),
        scratch_shapes=[
            pltpu.VMEM((2 * n_pairs, n_tiles, FOX_ACC_ROWS, FOX_TK), _BF16),
            pltpu.VMEM((2 * n_pairs, FOX_ACC_ROWS, LANES), _BF16),
            pltpu.VMEM((2 * n_pairs, s, LANES), _BF16),
            pltpu.VMEM((2 * n_pairs, N_META, LANES), _BF16),
            pltpu.VMEM((2, 2 * n_pairs, LANES, FOX_TQ), _BF16),
            pltpu.VMEM((2, 2 * n_pairs, FOX_ACC_ROWS, FOX_TQ), _F32),
            pltpu.VMEM((2, 2 * n_pairs, 1, FOX_TQ), _F32),
            pltpu.VMEM((2, 2 * n_pairs, FOX_TK, FOX_TQ), _F32),
            pltpu.VMEM((2, 2 * n_pairs, SUBLANES, FOX_TQ), _F32),
            pltpu.VMEM((2 * n_pairs, N_META, FOX_TQ), _F32),
            pltpu.VMEM((2 * n_pairs, SUBLANES, FOX_TQ), _F32),
        ],
        compiler_params=pltpu.CompilerParams(
            dimension_semantics=("arbitrary", "arbitrary"),
            vmem_limit_bytes=VMEM_LIMIT),
        name="fox",
    )(z3, z3, z3, z3, c4, zm, zm, cm3)


def _outproj_kernel(yr_ref, yf_ref, x_ref, w_ref, g_ref, o_ref, wb_scr):
    half = w_ref.shape[0] // 2

    @pl.when(pl.program_id(0) == 0)
    def _():
        wb_scr[...] = w_ref[...].astype(_BF16)

    for r in range(x_ref.shape[0] // OUT_SUB):
        rows = pl.ds(r * OUT_SUB, OUT_SUB)
        hres = (x_ref[rows, :] + _dot(yr_ref[rows, :], wb_scr[:half, :])
                + _dot(yf_ref[rows, :], wb_scr[half:, :]))
        ms = jnp.mean(hres * hres, axis=-1, keepdims=True)
        o_ref[rows, :] = hres * lax.rsqrt(ms + EPS) * g_ref[...]


def _outproj(yr, yf, x2d, w_out, g, *, tm):
    rows = x2d.shape[0]
    return pl.pallas_call(
        _outproj_kernel,
        grid=(rows // tm,),
        in_specs=[
            pl.BlockSpec((tm, yr.shape[1]), lambda i: (i, 0)),
            pl.BlockSpec((tm, yf.shape[1]), lambda i: (i, 0)),
            pl.BlockSpec((tm, D_MODEL), lambda i: (i, 0)),
            pl.BlockSpec(w_out.shape, lambda i: (0, 0)),
            pl.BlockSpec((1, D_MODEL), lambda i: (0, 0)),
        ],
        out_specs=pl.BlockSpec((tm, D_MODEL), lambda i: (i, 0)),
        out_shape=jax.ShapeDtypeStruct((rows, D_MODEL), _F32),
        scratch_shapes=[pltpu.VMEM(w_out.shape, _BF16)],
        compiler_params=pltpu.CompilerParams(
            dimension_semantics=("arbitrary",),
            vmem_limit_bytes=VMEM_LIMIT),
        name="outproj",
    )(yr, yf, x2d, w_out, g)


def _rope_tables(first_pos, n):
    pos = np.arange(n, dtype=np.float32) + np.float32(first_pos)
    inv = np.float32(ROPE_BASE) ** (-np.arange(0, RET_DK, 2, dtype=np.float32) / np.float32(RET_DK))
    ang = pos[:, None] * inv[None, :]
    cos, sin = np.cos(ang), np.sin(ang)
    return np.concatenate([cos, cos], axis=-1), np.concatenate([-sin, sin], axis=-1)


def kernel(x, meta_tokens, norm_g, w_in, b_f, w_out, final_g):
    b, s, d = x.shape
    assert norm_g.shape[0] == 1 and d == D_MODEL and s % FOX_TQ == 0
    x2d = x.reshape(b * s, d)
    col_scale = jnp.ones((D_MAIN, 1), _F32)
    col_scale = col_scale.at[COL_RK * LANES:COL_RV * LANES].set(RET_DK ** -0.5)
    col_scale = col_scale.at[COL_FQ * LANES:COL_FK * LANES].set(FOX_HEAD_DIM ** -0.5 * LOG2E)
    w_all_t = jnp.swapaxes(w_in, 1, 2)[0]
    g = norm_g[0].reshape(1, d)
    bf = b_f[0].reshape(1, FOX_HEADS)
    cos2, sin2 = _rope_tables(N_META, s)
    cosm, sinm = _rope_tables(0, CHUNK)

    zm, cm, w_t = _inproj(meta_tokens.astype(_F32), g, w_all_t, col_scale, w_all_t, bf, cosm, sinm,
                          tm=CHUNK, tn=3584, n_valid=N_META)
    z, c, _ = _inproj(x2d, g, w_t, col_scale, w_all_t, bf, cos2, sin2, tm=s, tn=1792, n_valid=s)

    z3 = z.reshape(b, s, D_MAIN)
    lg = np.log1p(-np.exp2(-5.0 - np.arange(RET_HEADS, dtype=np.float32))).astype(np.float32)
    y_r = _retention(lg, z3, zm)

    y_f = _fox(z3, c, zm, cm[0])

    out = _outproj(y_r.reshape(b * s, -1), y_f.reshape(b * s, -1), x2d,
                   w_out[0], final_g.reshape(1, d), tm=1024)
    return out.reshape(b, s, d)
```

```python
import functools

import jax
import jax.numpy as jnp
import numpy as np
from jax import lax
from jax.experimental import pallas as pl
from jax.experimental.pallas import tpu as pltpu

D_MODEL = 1024
N_META = 16
CHUNK = 128
RET_HEADS = 4
RET_DK = 128
RET_DV = 256
FOX_HEADS = 16
FOX_HEAD_DIM = 64
ROPE_BASE = 10000.0
EPS = 1e-6
NEG_INF = -1e30
LOG2E = 1.4426950408889634

LANES = 128
SUBLANES = 8
BF16_SUBLANES = 16
D_MAIN = 7168
COL_RQ, COL_RK, COL_RV, COL_RG = 0, 4, 8, 16
COL_FQ, COL_FK, COL_FV, COL_FG = 24, 32, 40, 48
FOX_TQ = 256
FOX_TK = 256
FOX_PAIRS = 2
FOX_ACC_ROWS = FOX_HEAD_DIM + BF16_SUBLANES
FOX_BIAS_PIECES = 3
PROJ_SUB = 512
OUT_SUB = 256
RET_HEADS_PER_STEP = 4
VMEM_LIMIT = 56 * 1024 * 1024

_F32 = jnp.float32
_BF16 = jnp.bfloat16


def _dot(a, b):
    return jnp.dot(a, b, preferred_element_type=_F32)


def _dot_nt(a, b):
    return lax.dot_general(a, b, (((1,), (1,)), ((), ())), preferred_element_type=_F32)


def _silu(g):
    h = 0.5 * g
    return h + h * jnp.tanh(h)


def _rotary(x, cos2, sin2):
    return x * cos2 + pltpu.roll(x, RET_DK // 2, axis=1) * sin2


def _inproj_kernel(x_ref, g_ref, w_ref, cs_ref, wff_ref, bf_ref, cos_ref, sin_ref,
                   z_ref, c_ref, wcast_ref, u_scr, lf_scr, *, tm, tn, sub, n_valid, cast_weights):
    j = pl.program_id(1)
    n_sub = tm // sub

    def weight_tile():
        if not cast_weights:
            wcast_ref[...] = jnp.zeros(wcast_ref.shape, _BF16)
            return w_ref[...]
        wcast_ref[...] = (w_ref[...] * cs_ref[...]).astype(_BF16)
        return wcast_ref[...]

    def normalize(r):
        lo = r * sub
        n = max(0, min(sub, x_ref.shape[0] - lo))
        if n:
            xf = x_ref[lo:lo + n, :]
            ms = jnp.mean(xf * xf, axis=-1, keepdims=True)
            u_scr[lo:lo + n, :] = (xf * lax.rsqrt(ms + EPS) * g_ref[...]).astype(_BF16)
        if n < sub:
            u_scr[lo + n:lo + sub, :] = jnp.zeros((sub - n, D_MODEL), _BF16)
        lf_scr[:, lo:lo + sub] = _dot_nt(wff_ref[...].astype(_BF16), u_scr[lo:lo + sub, :])

    def project(r, w_tile, rot_cols):
        rows = pl.ds(r * sub, sub)
        acc = _dot_nt(u_scr[rows, :], w_tile)
        if rot_cols:
            cos2, sin2 = cos_ref[rows, :], sin_ref[rows, :]
            slabs = [_rotary(acc[:, a:a + RET_DK], cos2, sin2) for a in range(0, rot_cols, RET_DK)]
            acc = jnp.concatenate(slabs + [acc[:, rot_cols:]], axis=1) if rot_cols < tn else (
                jnp.concatenate(slabs, axis=1))
        z_ref[rows, :] = acc.astype(_BF16)

    def forget_cumsum():
        lane = lax.broadcasted_iota(jnp.int32, (FOX_HEADS, LANES), 1)
        carry = jnp.zeros((FOX_HEADS, 1), _F32)
        hh = lax.broadcasted_iota(jnp.int32, (FOX_HEADS, FOX_HEADS), 0)
        ll = lax.broadcasted_iota(jnp.int32, (FOX_HEADS, FOX_HEADS), 1)
        bf = jnp.sum(jnp.where(hh == ll, bf_ref[...], 0.0), axis=1, keepdims=True)
        for ci in range(tm // LANES):
            v = lf_scr[:, ci * LANES:(ci + 1) * LANES] + bf
            blk = jnp.minimum(v, 0.0) - jnp.log1p(jnp.exp(-jnp.abs(v)))
            if (ci + 1) * LANES > n_valid:
                blk = jnp.where(lane + ci * LANES < n_valid, blk, 0.0)
            sh = 1
            while sh < LANES:
                rolled = pltpu.roll(blk, sh, axis=1)
                blk = blk + jnp.where(lane >= sh, rolled, 0.0)
                sh *= 2
            blk = blk + carry
            for p in range(FOX_HEADS // 2):
                c_ref[0, p, :, ci * LANES:(ci + 1) * LANES] = blk[2 * p:2 * p + 2, :]
            carry = blk[:, LANES - 1:LANES]

    rot_end = COL_RV * LANES
    rot_tiles = -(-rot_end // tn)

    for jj in range(rot_tiles):
        @pl.when(j == jj)
        def _(jj=jj):
            w_tile = weight_tile()
            for r in range(n_sub):
                if jj == 0:
                    normalize(r)
                project(r, w_tile, min(tn, rot_end - jj * tn))
            if jj == 0:
                forget_cumsum()

    @pl.when(j >= rot_tiles)
    def _():
        w_tile = weight_tile()
        for r in range(n_sub):
            project(r, w_tile, 0)


def _inproj(x2d, g, w_t, col_scale, w_all_t, b_f, cos2, sin2, *, tm, tn, n_valid):
    x_rows = min(tm, x2d.shape[0])
    assert x2d.shape[0] % x_rows == 0 and x_rows % BF16_SUBLANES == 0
    rows = (x2d.shape[0] // x_rows) * tm
    sub = min(tm, PROJ_SUB)
    assert D_MAIN % tn == 0 and tn % RET_DK == 0 and D_MAIN % FOX_HEADS == 0
    cast = w_t.dtype != _BF16
    wrows = tn if cast else SUBLANES
    kern = functools.partial(_inproj_kernel, tm=tm, tn=tn, sub=sub, n_valid=n_valid,
                             cast_weights=cast)
    return pl.pallas_call(
        kern,
        grid=(rows // tm, D_MAIN // tn),
        in_specs=[
            pl.BlockSpec((x_rows, D_MODEL), lambda i, j: (i, 0)),
            pl.BlockSpec((1, D_MODEL), lambda i, j: (0, 0)),
            pl.BlockSpec((tn, D_MODEL), lambda i, j: (j, 0)),
            pl.BlockSpec((wrows, 1), lambda i, j: (j if cast else 0, 0)),
            pl.BlockSpec((FOX_HEADS, D_MODEL), lambda i, j: (D_MAIN // FOX_HEADS, 0)),
            pl.BlockSpec((1, FOX_HEADS), lambda i, j: (0, 0)),
            pl.BlockSpec((tm, RET_DK), lambda i, j: (0, 0)),
            pl.BlockSpec((tm, RET_DK), lambda i, j: (0, 0)),
        ],
        out_specs=[
            pl.BlockSpec((tm, tn), lambda i, j: (i, j)),
            pl.BlockSpec((1, FOX_HEADS // 2, 2, tm), lambda i, j: (i, 0, 0, 0)),
            pl.BlockSpec((wrows, D_MODEL), lambda i, j: (j if cast else 0, 0)),
        ],
        out_shape=[
            jax.ShapeDtypeStruct((rows, D_MAIN), _BF16),
            jax.ShapeDtypeStruct((rows // tm, FOX_HEADS // 2, 2, tm), _F32),
            jax.ShapeDtypeStruct((D_MAIN if cast else SUBLANES, D_MODEL), _BF16),
        ],
        scratch_shapes=[
            pltpu.VMEM((tm, D_MODEL), _BF16),
            pltpu.VMEM((FOX_HEADS, tm), _F32),
        ],
        compiler_params=pltpu.CompilerParams(
            dimension_semantics=("arbitrary", "arbitrary"),
            vmem_limit_bytes=VMEM_LIMIT),
        name="inproj",
    )(x2d, g, w_t, col_scale, w_all_t, b_f, cos2, sin2)


def _retention_kernel(lg_ref, q_ref, k_ref, v_ref, g_ref, km_ref, vm_ref, o_ref,
                      *, n_chunks, heads_per_step):
    ri = lax.broadcasted_iota(jnp.int32, (CHUNK, CHUNK), 0)
    ci = lax.broadcasted_iota(jnp.int32, (CHUNK, CHUNK), 1)
    diff = (ri - ci).astype(_F32)
    idx = lax.broadcasted_iota(jnp.int32, (CHUNK, 1), 0).astype(_F32)

    def one_head(hd):
        lg = lg_ref[pl.program_id(1) * heads_per_step + hd]
        kcols = slice(hd * RET_DK, (hd + 1) * RET_DK)
        vcols = slice(hd * RET_DV, (hd + 1) * RET_DV)
        dmask = jnp.where(diff >= 0, jnp.exp(lg * jnp.maximum(diff, 0.0)), 0.0)
        xi = jnp.exp(lg * (idx + 1.0))
        zeta = jnp.exp(lg * (CHUNK - 1.0 - idx))
        chunk_decay = jnp.exp(lg * jnp.full((1, 1), float(CHUNK), _F32))

        zeta_m = jnp.exp(lg * (N_META - 1.0 - idx))
        km = km_ref[:, kcols].astype(_F32)
        state = _dot((km * zeta_m).T.astype(_BF16), vm_ref[:, vcols])

        def chunk_products(n):
            rows = slice(n * CHUNK, (n + 1) * CHUNK)
            kb = k_ref[0, rows, kcols]
            v = v_ref[0, rows, vcols]
            scores = (_dot_nt(q_ref[0, rows, kcols], kb) * dmask).astype(_BF16)
            kv = _dot((kb.astype(_F32) * zeta).T.astype(_BF16), v)
            return scores, kv

        def chunk_output(n, scores, state):
            rows = slice(n * CHUNK, (n + 1) * CHUNK)
            q_decayed = (q_ref[0, rows, kcols].astype(_F32) * xi).astype(_BF16)
            out = _dot(jnp.concatenate([scores, q_decayed], axis=1),
                       jnp.concatenate([v_ref[0, rows, vcols], state.astype(_BF16)], axis=0))
            y = out * lax.rsqrt(jnp.mean(out * out, axis=-1, keepdims=True) + EPS)
            gate = g_ref[0, rows, vcols].astype(_F32)
            o_ref[0, rows, vcols] = (y * _silu(gate)).astype(_BF16)

        scores, kv = chunk_products(0)
        for n in range(n_chunks):
            if n + 1 < n_chunks:
                nxt = chunk_products(n + 1)
            chunk_output(n, scores, state)
            state = state * chunk_decay + kv
            if n + 1 < n_chunks:
                scores, kv = nxt

    for hd in range(heads_per_step):
        one_head(hd)


def _retention(lg, z3, zm):
    b, s, _ = z3.shape
    hps = RET_HEADS_PER_STEP
    wk, wv = hps * RET_DK, hps * RET_DV
    kern = functools.partial(_retention_kernel, n_chunks=s // CHUNK, heads_per_step=hps)
    return pl.pallas_call(
        kern,
        grid=(b, RET_HEADS // hps),
        in_specs=[
            pl.BlockSpec(memory_space=pltpu.SMEM),
            pl.BlockSpec((1, s, wk), lambda i, h: (i, 0, COL_RQ * LANES // wk + h)),
            pl.BlockSpec((1, s, wk), lambda i, h: (i, 0, COL_RK * LANES // wk + h)),
            pl.BlockSpec((1, s, wv), lambda i, h: (i, 0, COL_RV * LANES // wv + h)),
            pl.BlockSpec((1, s, wv), lambda i, h: (i, 0, COL_RG * LANES // wv + h)),
            pl.BlockSpec((CHUNK, wk), lambda i, h: (0, COL_RK * LANES // wk + h)),
            pl.BlockSpec((CHUNK, wv), lambda i, h: (0, COL_RV * LANES // wv + h)),
        ],
        out_specs=pl.BlockSpec((1, s, wv), lambda i, h: (i, 0, h)),
        out_shape=jax.ShapeDtypeStruct((b, s, RET_HEADS * RET_DV), _BF16),
        compiler_params=pltpu.CompilerParams(
            dimension_semantics=("arbitrary", "arbitrary"),
            vmem_limit_bytes=VMEM_LIMIT),
        name="retention",
    )(lg, z3, z3, z3, z3, zm, zm)


def _fox_kernel(q_ref, k_ref, v_ref, g_ref, c_ref, km_ref, vm_ref, cm_ref, o_ref,
                vt_scr, vtm_scr, ka_scr, kam_scr, qt_scr, acc_scr, m_scr, s_scr, pm_scr,
                sm_scr, pmm_scr, *, n_tiles, n_pairs, seq):
    half = FOX_HEAD_DIM
    n_heads = 2 * n_pairs
    sub = lax.broadcasted_iota(jnp.int32, (LANES, 1), 0)
    sub8 = lax.broadcasted_iota(jnp.int32, (SUBLANES, 1), 0)
    lane = lax.broadcasted_iota(jnp.int32, (1, LANES), 1)

    def pieces8(row):
        out = jnp.zeros((SUBLANES, LANES), _F32)
        rest = row
        for piece in range(FOX_BIAS_PIECES):
            part = rest.astype(_BF16).astype(_F32)
            out = jnp.where(sub8 == piece, part, out)
            rest = rest - part
        return out

    def augmented_keys(kblk, bias_lo, bias_hi):
        gap = jnp.zeros((half - SUBLANES, LANES), _F32)
        tile = jnp.concatenate([pieces8(bias_hi), gap, pieces8(bias_lo), gap], axis=0).T
        tile = tile.astype(_BF16)[0:kblk.shape[0], :]
        return jnp.where(lane < half, kblk, tile), jnp.where(lane < half, tile, kblk)

    def build_meta_operands():
        for p in range(n_pairs):
            lanes = slice(p * LANES, (p + 1) * LANES)
            cm0, cm1 = cm_ref[p, 0:1, :], cm_ref[p, 1:2, :]
            kam_scr[2 * p], kam_scr[2 * p + 1] = augmented_keys(
                km_ref[0:N_META, lanes],
                (cm0[:, LANES - 1:LANES] - cm0) * LOG2E, (cm1[:, LANES - 1:LANES] - cm1) * LOG2E)
            vtm = vm_ref[:, lanes].T
            for hh in range(2):
                h = 2 * p + hh
                vtm_scr[h, 0:half, :] = vtm[hh * half:(hh + 1) * half, :]
                vtm_scr[h, half:FOX_ACC_ROWS, :] = jnp.ones((FOX_ACC_ROWS - half, LANES), _BF16)

    def build_block_operands(jb):
        rows = slice(jb * FOX_TK, (jb + 1) * FOX_TK)
        for p in range(n_pairs):
            lanes = slice(p * LANES, (p + 1) * LANES)
            for cb in range(jb * FOX_TK // LANES, (jb + 1) * FOX_TK // LANES):
                ks = slice(cb * LANES, (cb + 1) * LANES)
                ka_scr[2 * p, ks, :], ka_scr[2 * p + 1, ks, :] = augmented_keys(
                    k_ref[0, ks, lanes],
                    c_ref[0, p, 0:1, ks] * -LOG2E, c_ref[0, p, 1:2, ks] * -LOG2E)
            vt = v_ref[0, rows, lanes].T
            for hh in range(2):
                h = 2 * p + hh
                vt_scr[h, jb, 0:half, :] = vt[hh * half:(hh + 1) * half, :]
                vt_scr[h, jb, half:FOX_ACC_ROWS, :] = jnp.ones(
                    (FOX_ACC_ROWS - half, FOX_TK), _BF16)

    ri = lax.broadcasted_iota(jnp.int32, (FOX_TK, FOX_TQ), 0)
    ci = lax.broadcasted_iota(jnp.int32, (FOX_TK, FOX_TQ), 1)
    causal = ri <= ci

    def group_max(st):
        parts = [st[r:r + SUBLANES, :] for r in range(0, st.shape[0], SUBLANES)]
        while len(parts) > 1:
            nxt = [jnp.maximum(parts[a], parts[a + 1]) for a in range(0, len(parts) - 1, 2)]
            if len(parts) % 2:
                nxt.append(parts[-1])
            parts = nxt
        return parts[0]

    def update(h, tile, first, st, pm, vt):
        buf = tile % 2
        m_blk = jnp.max(pm, axis=0, keepdims=True)
        if first:
            m_new = m_blk
        else:
            m_old = m_scr[buf, h]
            m_new = jnp.maximum(m_old, m_blk)
            alpha = jnp.exp2(m_old - m_new)
        pt = jnp.exp2(st - m_new).astype(_BF16)
        if pt.shape[0] < LANES:
            pt = jnp.concatenate(
                [pt, jnp.zeros((LANES - pt.shape[0], FOX_TQ), _BF16)], axis=0)
        pv = _dot(vt, pt)
        acc_scr[buf, h] = pv if first else acc_scr[buf, h] * alpha + pv
        m_scr[buf, h] = m_new

    def prepare_queries(i):
        rows = slice(i * FOX_TQ, (i + 1) * FOX_TQ)
        for p in range(n_pairs):
            qt = q_ref[0, rows, p * LANES:(p + 1) * LANES].astype(_F32).T
            ones_hi = jnp.where(sub < half + FOX_BIAS_PIECES, 1.0, 0.0)
            ones_lo = jnp.where(sub < FOX_BIAS_PIECES, 1.0, 0.0)
            qt_scr[i % 2, 2 * p] = jnp.where(sub < half, qt, ones_hi).astype(_BF16)
            qt_scr[i % 2, 2 * p + 1] = jnp.where(sub < half, ones_lo, qt).astype(_BF16)

    def scores(item, heads):
        kind, i, j, slot = item
        if kind == "meta":
            for h in heads:
                st = _dot(kam_scr[h], qt_scr[i % 2, h])
                sm_scr[h] = st
                pmm_scr[h] = group_max(st)
            return
        krows = slice(j * FOX_TK, (j + 1) * FOX_TK)
        for h in heads:
            st = _dot(ka_scr[h, krows, :], qt_scr[i % 2, h])
            if i == j:
                st = jnp.where(causal, st, NEG_INF)
            s_scr[slot, h] = st
            pm_scr[slot, h] = group_max(st)

    def consume(item, heads):
        kind, i, j, slot = item
        for h in heads:
            if kind == "meta":
                update(h, i, True, sm_scr[h], pmm_scr[h], vtm_scr[h])
            else:
                update(h, i, False, s_scr[slot, h], pm_scr[slot, h], vt_scr[h, j])

    def finalize(i):
        rows = slice(i * FOX_TQ, (i + 1) * FOX_TQ)
        for p in range(n_pairs):
            lanes = slice(p * LANES, (p + 1) * LANES)
            outs = []
            for hh in range(2):
                acc = acc_scr[i % 2, 2 * p + hh]
                outs.append(acc[0:half, :] / acc[half:half + 1, :])
            y = jnp.concatenate(outs, axis=0).T
            gate = g_ref[0, rows, lanes].astype(_F32)
            o_ref[0, rows, lanes] = (y * _silu(gate)).astype(_BF16)

    items = []
    for i in range(n_tiles):
        items.append(("meta", i, None, None))
        for j in [i] + list(range(i)):
            items.append(("blk", i, j, sum(it[0] == "blk" for it in items) % 2))
    build_meta_operands()
    build_block_operands(0)
    prepare_queries(0)
    scores(items[0], range(n_heads))
    for t, item in enumerate(items):
        if t + 1 < len(items):
            nxt = items[t + 1]
            if nxt[0] == "meta":
                build_block_operands(nxt[1])
                prepare_queries(nxt[1])
            scores(nxt, range(n_heads))
        consume(item, range(n_heads))
        if t + 1 == len(items) or items[t + 1][1] != item[1]:
            finalize(item[1])


def _fox(z3, c4, zm, cm3):
    b, s, _ = z3.shape
    n_pairs = FOX_PAIRS
    n_groups = FOX_HEADS // 2 // n_pairs
    n_tiles = s // FOX_TQ
    w = n_pairs * LANES
    kern = functools.partial(_fox_kernel, n_tiles=n_tiles, n_pairs=n_pairs, seq=s)
    return pl.pallas_call(
        kern,
        grid=(b, n_groups),
        in_specs=[
            pl.BlockSpec((1, s, w), lambda i, p: (i, 0, COL_FQ // n_pairs + p)),
            pl.BlockSpec((1, s, w), lambda i, p: (i, 0, COL_FK // n_pairs + p)),
            pl.BlockSpec((1, s, w), lambda i, p: (i, 0, COL_FV // n_pairs + p)),
            pl.BlockSpec((1, s, w), lambda i, p: (i, 0, COL_FG // n_pairs + p)),
            pl.BlockSpec((1, n_pairs, 2, s), lambda i, p: (i, p, 0, 0)),
            pl.BlockSpec((CHUNK, w), lambda i, p: (0, COL_FK // n_pairs + p)),
            pl.BlockSpec((CHUNK, w), lambda i, p: (0, COL_FV // n_pairs + p)),
            pl.BlockSpec((n_pairs, 2, LANES), lambda i, p: (p, 0, 0)),
        ],
        out_specs=pl.BlockSpec((1, s, w), lambda i, p: (i, 0, p)),
        out_shape=jax.ShapeDtypeStruct((b, s, FOX_HEADS * FOX_HEAD_DIM), _BF16),
        scratch_shapes=[
            pltpu.VMEM((2 * n_pairs, n_tiles, FOX_ACC_ROWS, FOX_TK), _BF16),
            pltpu.VMEM((2 * n_pairs, FOX_ACC_ROWS, LANES), _BF16),
            pltpu.VMEM((2 * n_pairs, s, LANES), _BF16),
            pltpu.VMEM((2 * n_pairs, N_META, LANES), _BF16),
            pltpu.VMEM((2, 2 * n_pairs, LANES, FOX_TQ), _BF16),
            pltpu.VMEM((2, 2 * n_pairs, FOX_ACC_ROWS, FOX_TQ), _F32),
            pltpu.VMEM((2, 2 * n_pairs, 1, FOX_TQ), _F32),
            pltpu.VMEM((2, 2 * n_pairs, FOX_TK, FOX_TQ), _F32),
            pltpu.VMEM((2, 2 * n_pairs, SUBLANES, FOX_TQ), _F32),
            pltpu.VMEM((2 * n_pairs, N_META, FOX_TQ), _F32),
            pltpu.VMEM((2 * n_pairs, SUBLANES, FOX_TQ), _F32),
        ],
        compiler_params=pltpu.CompilerParams(
            dimension_semantics=("arbitrary", "arbitrary"),
            vmem_limit_bytes=VMEM_LIMIT),
        name="fox",
    )(z3, z3, z3, z3, c4, zm, zm, cm3)


def _outproj_kernel(yr_ref, yf_ref, x_ref, w_ref, g_ref, o_ref, wb_scr):
    half = w_ref.shape[0] // 2

    @pl.when(pl.program_id(0) == 0)
    def _():
        wb_scr[...] = w_ref[...].astype(_BF16)

    for r in range(x_ref.shape[0] // OUT_SUB):
        rows = pl.ds(r * OUT_SUB, OUT_SUB)
        hres = (x_ref[rows, :] + _dot(yr_ref[rows, :], wb_scr[:half, :])
                + _dot(yf_ref[rows, :], wb_scr[half:, :]))
        ms = jnp.mean(hres * hres, axis=-1, keepdims=True)
        o_ref[rows, :] = hres * lax.rsqrt(ms + EPS) * g_ref[...]


def _outproj(yr, yf, x2d, w_out, g, *, tm):
    rows = x2d.shape[0]
    return pl.pallas_call(
        _outproj_kernel,
        grid=(rows // tm,),
        in_specs=[
            pl.BlockSpec((tm, yr.shape[1]), lambda i: (i, 0)),
            pl.BlockSpec((tm, yf.shape[1]), lambda i: (i, 0)),
            pl.BlockSpec((tm, D_MODEL), lambda i: (i, 0)),
            pl.BlockSpec(w_out.shape, lambda i: (0, 0)),
            pl.BlockSpec((1, D_MODEL), lambda i: (0, 0)),
        ],
        out_specs=pl.BlockSpec((tm, D_MODEL), lambda i: (i, 0)),
        out_shape=jax.ShapeDtypeStruct((rows, D_MODEL), _F32),
        scratch_shapes=[pltpu.VMEM(w_out.shape, _BF16)],
        compiler_params=pltpu.CompilerParams(
            dimension_semantics=("arbitrary",),
            vmem_limit_bytes=VMEM_LIMIT),
        name="outproj",
    )(yr, yf, x2d, w_out, g)


def _rope_tables(first_pos, n):
    pos = np.arange(n, dtype=np.float32) + np.float32(first_pos)
    inv = np.float32(ROPE_BASE) ** (-np.arange(0, RET_DK, 2, dtype=np.float32) / np.float32(RET_DK))
    ang = pos[:, None] * inv[None, :]
    cos, sin = np.cos(ang), np.sin(ang)
    return np.concatenate([cos, cos], axis=-1), np.concatenate([-sin, sin], axis=-1)


def kernel(x, meta_tokens, norm_g, w_in, b_f, w_out, final_g):
    b, s, d = x.shape
    assert norm_g.shape[0] == 1 and d == D_MODEL and s % FOX_TQ == 0
    x2d = x.reshape(b * s, d)
    col_scale = jnp.ones((D_MAIN, 1), _F32)
    col_scale = col_scale.at[COL_RK * LANES:COL_RV * LANES].set(RET_DK ** -0.5)
    col_scale = col_scale.at[COL_FQ * LANES:COL_FK * LANES].set(FOX_HEAD_DIM ** -0.5 * LOG2E)
    w_all_t = jnp.swapaxes(w_in, 1, 2)[0]
    g = norm_g[0].reshape(1, d)
    bf = b_f[0].reshape(1, FOX_HEADS)
    cos2, sin2 = _rope_tables(N_META, s)
    cosm, sinm = _rope_tables(0, CHUNK)

    zm, cm, w_t = _inproj(meta_tokens.astype(_F32), g, w_all_t, col_scale, w_all_t, bf, cosm, sinm,
                          tm=CHUNK, tn=1792, n_valid=N_META)
    z, c, _ = _inproj(x2d, g, w_t, col_scale, w_all_t, bf, cos2, sin2, tm=s, tn=1792, n_valid=s)

    z3 = z.reshape(b, s, D_MAIN)
    lg = np.log1p(-np.exp2(-5.0 - np.arange(RET_HEADS, dtype=np.float32))).astype(np.float32)
    y_r = _retention(lg, z3, zm)

    y_f = _fox(z3, c, zm, cm[0])

    out = _outproj(y_r.reshape(b * s, -1), y_f.reshape(b * s, -1), x2d,
                   w_out[0], final_g.reshape(1, d), tm=1024)
    return out.reshape(b, s, d)
```
